```python
import math
import jax, jax.numpy as jnp
from jax import lax
import numpy as np

D_MODEL = 2048
BATCH = 4
SEQ = 8192
DEPTH = 4
DEC_BATCH = 8
DEC_SEQ = 64
PAST_LEN = 1024

CHUNK = 64
Q_BLOCK = 128
N_AB = (DEPTH + 1) // 2
N_C = DEPTH // 2
MLA_V = 128
MLA_HEADS = (D_MODEL // 2) // MLA_V
MLA_NOPE = 128
MLA_ROPE = 64
Q_LORA = 512
KV_LORA = 256
MLA_SCALE = (MLA_NOPE + MLA_ROPE) ** -0.5
S5_WIDTH = D_MODEL // 2
S5_GROUP = 16
S5_GROUPS = S5_WIDTH // S5_GROUP
S5_STATE = 64
DT_MIN = 1e-3
DT_MAX = 1e-1
IN_AB = Q_LORA + KV_LORA + MLA_ROPE + S5_WIDTH
MIX_AB = MLA_HEADS * MLA_V + S5_WIDTH
RET_HEADS = 8
RET_DK = D_MODEL // RET_HEADS
RET_DV = 2 * RET_DK
IN_C = 2 * RET_HEADS * RET_DK + 2 * RET_HEADS * RET_DV
D_FF = 4 * D_MODEL
ROPE_THETA = 10000.0
EPS = 1e-6
GN_EPS = 1e-5

kernel_name = 'streaming_mla_s5_retention_step'


def _normal(k, shape, scale):
    return jax.random.normal(k, shape, jnp.float32) * scale


def _rmsnorm(x, g):
    xf = x.astype(jnp.float32)
    y = xf * lax.rsqrt(jnp.mean(xf * xf, axis=-1, keepdims=True) + EPS)
    return (y * g.astype(jnp.float32)).astype(x.dtype)


def _rope(x, pos):
    half = x.shape[-1] // 2
    inv = ROPE_THETA ** (-jnp.arange(half, dtype=jnp.float32) / half)
    ang = pos.astype(jnp.float32)[:, None] * inv[None, :]
    ang = ang.reshape((ang.shape[0],) + (1,) * (x.ndim - 3) + (half,))
    cos, sin = jnp.cos(ang), jnp.sin(ang)
    xf = x.astype(jnp.float32)
    x1, x2 = xf[..., :half], xf[..., half:]
    return jnp.concatenate([x1 * cos - x2 * sin, x1 * sin + x2 * cos], axis=-1).astype(x.dtype)


def _mlp(h, w_up, w_down):
    return jnp.square(jax.nn.relu(h @ w_up)) @ w_down


def _mla_attention(q_nope, q_rope, k_nope, k_rope, v, q_pos):
    Bn, Sq = q_nope.shape[:2]
    k_chunk = jnp.arange(k_nope.shape[1]) // CHUNK

    def attend(args):
        qn, qr, qp = args
        s = (jnp.einsum('bqhd,bkhd->bhqk', qn, k_nope)
             + jnp.einsum('bqhr,bkr->bhqk', qr, k_rope)).astype(jnp.float32) * MLA_SCALE
        allowed = k_chunk[None, :] <= (qp[:, None] // CHUNK)
        s = jnp.where(allowed, s, -jnp.inf)
        p = jax.nn.softmax(s, axis=-1).astype(v.dtype)
        return jnp.einsum('bhqk,bkhd->bqhd', p, v)

    if Sq <= Q_BLOCK:
        return attend((q_nope, q_rope, q_pos))
    nb = Sq // Q_BLOCK

    def to_blocks(t):
        return jnp.moveaxis(t.reshape((Bn, nb, Q_BLOCK) + t.shape[2:]), 1, 0)

    out = lax.map(attend, (to_blocks(q_nope), to_blocks(q_rope), q_pos.reshape(nb, Q_BLOCK)))
    return jnp.moveaxis(out, 0, 1).reshape((Bn, Sq) + out.shape[3:])


def _complex_affine_combine(left, right):
    a1r, a1i, b1r, b1i = left
    a2r, a2i, b2r, b2i = right
    return (a2r * a1r - a2i * a1i,
            a2r * a1i + a2i * a1r,
            a2r * b1r - a2i * b1i + b2r,
            a2r * b1i + a2i * b1r + b2i)


def _s5_scan(u, h_re, h_im, lam_re, lam_im, log_dt, b_re, b_im, c_re, c_im, d_skip):
    f32 = jnp.float32
    Bn, S, _ = u.shape
    ug = u.reshape(Bn, S, S5_GROUPS, S5_GROUP).astype(f32)
    lr = jnp.minimum(lam_re.astype(f32), -1e-4)
    li = lam_im.astype(f32)
    dt = jnp.exp(log_dt.astype(f32))[:, None]
    mag = jnp.exp(lr * dt)
    a_re = mag * jnp.cos(li * dt)
    a_im = mag * jnp.sin(li * dt)
    den = lr * lr + li * li
    f_re = ((a_re - 1.0) * lr + a_im * li) / den
    f_im = (a_im * lr - (a_re - 1.0) * li) / den
    br = b_re.astype(f32)
    bi = b_im.astype(f32)
    bb_re = f_re[..., None] * br - f_im[..., None] * bi
    bb_im = f_re[..., None] * bi + f_im[..., None] * br
    bu_re = jnp.einsum('gnp,bsgp->bsgn', bb_re, ug)
    bu_im = jnp.einsum('gnp,bsgp->bsgn', bb_im, ug)
    hr = h_re.astype(f32)
    hi = h_im.astype(f32)
    bu_re = bu_re.at[:, 0].add(a_re * hr - a_im * hi)
    bu_im = bu_im.at[:, 0].add(a_re * hi + a_im * hr)
    A_re = jnp.broadcast_to(a_re, (1, S) + a_re.shape)
    A_im = jnp.broadcast_to(a_im, (1, S) + a_im.shape)
    _, _, s_re, s_im = lax.associative_scan(_complex_affine_combine, (A_re, A_im, bu_re, bu_im), axis=1)
    y = (jnp.einsum('gpn,bsgn->bsgp', c_re.astype(f32), s_re)
         - jnp.einsum('gpn,bsgn->bsgp', c_im.astype(f32), s_im)
         + d_skip.astype(f32) * ug)
    return y.reshape(Bn, S, S5_WIDTH).astype(u.dtype), s_re[:, -1], s_im[:, -1]


def _mixer_ab(h, pos, past_ckv, past_krope, s5_re, s5_im, w_in, q_a_norm, kv_a_norm, w_q_b, w_kv_b,
              lam_re, lam_im, log_dt, b_re, b_im, c_re, c_im, d_skip, w_glu, b_glu, w_out):
    Bn, S, _ = h.shape
    c_q, c_kv, k_rope, u = jnp.split(h @ w_in, [Q_LORA, Q_LORA + KV_LORA, Q_LORA + KV_LORA + MLA_ROPE], axis=-1)
    q = (_rmsnorm(c_q, q_a_norm) @ w_q_b).reshape(Bn, S, MLA_HEADS, MLA_NOPE + MLA_ROPE)
    q_nope = q[..., :MLA_NOPE]
    q_rope = _rope(q[..., MLA_NOPE:], pos)
    c_kv = _rmsnorm(c_kv, kv_a_norm)
    k_rope = _rope(k_rope, pos)
    all_ckv = jnp.concatenate([past_ckv.astype(c_kv.dtype), c_kv], axis=1)
    all_krope = jnp.concatenate([past_krope.astype(k_rope.dtype), k_rope], axis=1)
    kv = (all_ckv @ w_kv_b).reshape(Bn, all_ckv.shape[1], MLA_HEADS, MLA_NOPE + MLA_V)
    attn = _mla_attention(q_nope, q_rope, kv[..., :MLA_NOPE], all_krope, kv[..., MLA_NOPE:], pos)
    y, s_re, s_im = _s5_scan(u, s5_re, s5_im, lam_re, lam_im, log_dt, b_re, b_im, c_re, c_im, d_skip)
    z = jax.nn.gelu(y)
    ssm = z * jax.nn.sigmoid(z @ w_glu + b_glu)
    out = jnp.concatenate([attn.reshape(Bn, S, MLA_HEADS * MLA_V), ssm], axis=-1) @ w_out
    return out, c_kv, k_rope, s_re, s_im


def _retention_chunk(q, k, v, state, log_g):
    L = q.shape[2]
    dt = q.dtype
    idx = jnp.arange(L, dtype=jnp.float32)
    diff = idx[:, None] - idx[None, :]
    decay = jnp.where(diff >= 0, jnp.exp(log_g[:, None, None] * jnp.maximum(diff, 0.0)), 0.0).astype(dt)
    inner = jnp.einsum('bhnm,bhmv->bhnv', jnp.einsum('bhnd,bhmd->bhnm', q, k) * decay, v)
    q_dec = jnp.exp(log_g[:, None] * (idx + 1.0)).astype(dt)[None, :, :, None]
    cross = jnp.einsum('bhnd,bhdv->bhnv', q * q_dec, state)
    k_dec = jnp.exp(log_g[:, None] * (L - 1.0 - idx)).astype(dt)[None, :, :, None]
    new_state = (jnp.exp(log_g * L).astype(dt)[None, :, None, None] * state
                 + jnp.einsum('bhmd,bhmv->bhdv', k * k_dec, v))
    return inner + cross, new_state


def _mixer_c(h, pos, state, w_in, gn_gain, w_out):
    Bn, S, _ = h.shape
    qk = RET_HEADS * RET_DK
    q, k, v, g = jnp.split(h @ w_in, [qk, 2 * qk, 2 * qk + RET_HEADS * RET_DV], axis=-1)
    q = _rope(q.reshape(Bn, S, RET_HEADS, RET_DK), pos)
    k = _rope(k.reshape(Bn, S, RET_HEADS, RET_DK), pos) * RET_DK ** -0.5
    v = v.reshape(Bn, S, RET_HEADS, RET_DV)
    q, k, v = (jnp.swapaxes(t, 1, 2) for t in (q, k, v))
    log_g = jnp.log(1.0 - 2.0 ** (-5.0 - jnp.arange(RET_HEADS, dtype=jnp.float32)))
    state = state.astype(q.dtype)
    if S <= CHUNK:
        o, new_state = _retention_chunk(q, k, v, state, log_g)
    else:
        nc = S // CHUNK

        def to_chunks(t):
            return jnp.moveaxis(t.reshape(Bn, RET_HEADS, nc, CHUNK, t.shape[-1]), 2, 0)

        def step(st, qkv):
            o_c, st = _retention_chunk(qkv[0], qkv[1], qkv[2], st, log_g)
            return st, o_c

        new_state, o = lax.scan(step, state, (to_chunks(q), to_chunks(k), to_chunks(v)))
        o = jnp.moveaxis(o, 0, 2).reshape(Bn, RET_HEADS, S, RET_DV)
    of = jnp.swapaxes(o, 1, 2).astype(jnp.float32)
    mu = jnp.mean(of, axis=-1, keepdims=True)
    var = jnp.mean(jnp.square(of - mu), axis=-1, keepdims=True)
    on = (of - mu) * lax.rsqrt(var + GN_EPS) * gn_gain.astype(jnp.float32).reshape(RET_HEADS, RET_DV)
    out = (on.astype(h.dtype).reshape(Bn, S, RET_HEADS * RET_DV) * jax.nn.silu(g)) @ w_out
    return out, new_state


def _trunk(x, pos, past_ckv, past_krope, s5_re, s5_im, ret_state,
           norm_mix, norm_mlp, norm_final, w_in_ab, q_a_norm, kv_a_norm, w_q_b, w_kv_b,
           s5_lam_re, s5_lam_im, s5_log_dt, s5_b_re, s5_b_im, s5_c_re, s5_c_im, s5_d, w_glu, b_glu,
           w_out_ab, w_in_c, ret_gn, w_out_c, w_up, w_down):
    ckv_rows, krope_rows, s5_re_new, s5_im_new, ret_new = [], [], [], [], []
    for layer in range(DEPTH):
        i = layer // 2
        h = _rmsnorm(x, norm_mix[layer])
        if layer % 2 == 0:
            mix, ckv, kr, sr, si = _mixer_ab(h, pos, past_ckv[i], past_krope[i], s5_re[i], s5_im[i],
                                             w_in_ab[i], q_a_norm[i], kv_a_norm[i], w_q_b[i], w_kv_b[i],
                                             s5_lam_re[i], s5_lam_im[i], s5_log_dt[i], s5_b_re[i], s5_b_im[i],
                                             s5_c_re[i], s5_c_im[i], s5_d[i], w_glu[i], b_glu[i], w_out_ab[i])
            ckv_rows.append(ckv)
            krope_rows.append(kr)
            s5_re_new.append(sr)
            s5_im_new.append(si)
        else:
            mix, rs = _mixer_c(h, pos, ret_state[i], w_in_c[i], ret_gn[i], w_out_c[i])
            ret_new.append(rs)
        x = x + mix
        x = x + _mlp(_rmsnorm(x, norm_mlp[layer]), w_up[layer], w_down[layer])
    return (_rmsnorm(x, norm_final), jnp.stack(ckv_rows), jnp.stack(krope_rows),
            jnp.stack(s5_re_new), jnp.stack(s5_im_new), jnp.stack(ret_new))


def setup_inputs(seed: int = 0) -> dict:
    key = jax.random.key(seed)
    k = jax.random.split(key, 31)
    f32 = jnp.float32
    G, N, P = S5_GROUPS, S5_STATE, S5_GROUP
    return {
        'x_prompt': _normal(k[0], (BATCH, SEQ, D_MODEL), 1.0),
        'x_sample': _normal(k[1], (DEC_BATCH, DEC_SEQ, D_MODEL), 1.0),
        'cache_mla_ckv': _normal(k[2], (N_AB, DEC_BATCH, PAST_LEN, KV_LORA), 1.0),
        'cache_mla_krope': _normal(k[3], (N_AB, DEC_BATCH, PAST_LEN, MLA_ROPE), 1.0),
        'state_s5_re': _normal(k[4], (N_AB, DEC_BATCH, G, N), 0.1),
        'state_s5_im': _normal(k[5], (N_AB, DEC_BATCH, G, N), 0.1),
        'state_ret': _normal(k[6], (N_C, DEC_BATCH, RET_HEADS, RET_DK, RET_DV), 1.0),
        'norm_mix': 1.0 + _normal(k[7], (DEPTH, D_MODEL), 0.02),
        'norm_mlp': 1.0 + _normal(k[8], (DEPTH, D_MODEL), 0.02),
        'norm_final': 1.0 + _normal(k[9], (D_MODEL,), 0.02),
        'w_in_ab': _normal(k[10], (N_AB, D_MODEL, IN_AB), D_MODEL ** -0.5),
        'q_a_norm': 1.0 + _normal(k[11], (N_AB, Q_LORA), 0.02),
        'kv_a_norm': 1.0 + _normal(k[12], (N_AB, KV_LORA), 0.02),
        'w_q_b': _normal(k[13], (N_AB, Q_LORA, MLA_HEADS * (MLA_NOPE + MLA_ROPE)), Q_LORA ** -0.5),
        'w_kv_b': _normal(k[14], (N_AB, KV_LORA, MLA_HEADS * (MLA_NOPE + MLA_V)), KV_LORA ** -0.5),
        's5_lam_re': -0.5 + _normal(k[15], (N_AB, G, N), 0.01),
        's5_lam_im': math.pi * jnp.arange(N, dtype=f32) + _normal(k[16], (N_AB, G, N), 0.01),
        's5_log_dt': jax.random.uniform(k[17], (N_AB, G), f32, math.log(DT_MIN), math.log(DT_MAX)),
        's5_b_re': _normal(k[18], (N_AB, G, N, P), (2 * P) ** -0.5),
        's5_b_im': _normal(k[19], (N_AB, G, N, P), (2 * P) ** -0.5),
        's5_c_re': _normal(k[20], (N_AB, G, P, N), N ** -0.5),
        's5_c_im': _normal(k[21], (N_AB, G, P, N), N ** -0.5),
        's5_d': _normal(k[22], (N_AB, G, P), 1.0),
        'w_glu': _normal(k[23], (N_AB, S5_WIDTH, S5_WIDTH), S5_WIDTH ** -0.5),
        'b_glu': _normal(k[24], (N_AB, S5_WIDTH), 0.01),
        'w_out_ab': _normal(k[25], (N_AB, MIX_AB, D_MODEL), MIX_AB ** -0.5),
        'w_in_c': _normal(k[26], (N_C, D_MODEL, IN_C), D_MODEL ** -0.5),
        'ret_gn': 1.0 + _normal(k[27], (N_C, RET_HEADS * RET_DV), 0.02),
        'w_out_c': _normal(k[28], (N_C, RET_HEADS * RET_DV, D_MODEL), (RET_HEADS * RET_DV) ** -0.5),
        'w_up': _normal(k[29], (DEPTH, D_MODEL, D_FF), D_MODEL ** -0.5),
        'w_down': _normal(k[30], (DEPTH, D_FF, D_MODEL), D_FF ** -0.5),
    }


def reference(x_prompt, x_sample, cache_mla_ckv, cache_mla_krope, state_s5_re, state_s5_im, state_ret,
              norm_mix, norm_mlp, norm_final, w_in_ab, q_a_norm, kv_a_norm, w_q_b, w_kv_b,
              s5_lam_re, s5_lam_im, s5_log_dt, s5_b_re, s5_b_im, s5_c_re, s5_c_im, s5_d, w_glu, b_glu,
              w_out_ab, w_in_c, ret_gn, w_out_c, w_up, w_down):
    weights = (norm_mix, norm_mlp, norm_final, w_in_ab, q_a_norm, kv_a_norm, w_q_b, w_kv_b,
               s5_lam_re, s5_lam_im, s5_log_dt, s5_b_re, s5_b_im, s5_c_re, s5_c_im, s5_d, w_glu, b_glu,
               w_out_ab, w_in_c, ret_gn, w_out_c, w_up, w_down)
    bp, sp, _ = x_prompt.shape
    dt = x_prompt.dtype
    empty_ckv = jnp.zeros((N_AB, bp, 0, KV_LORA), dt)
    empty_krope = jnp.zeros((N_AB, bp, 0, MLA_ROPE), dt)
    zero_s5 = jnp.zeros((N_AB, bp, S5_GROUPS, S5_STATE), jnp.float32)
    zero_ret = jnp.zeros((N_C, bp, RET_HEADS, RET_DK, RET_DV), dt)
    pos_p = jnp.arange(sp)
    y_prompt, ckv_p, krope_p, s5_re_p, s5_im_p, ret_p = _trunk(
        x_prompt, pos_p, empty_ckv, empty_krope, zero_s5, zero_s5, zero_ret, *weights)
    past = cache_mla_ckv.shape[2]
    pos_s = past + jnp.arange(x_sample.shape[1])
    y_sample, ckv_s, krope_s, s5_re_s, s5_im_s, ret_s = _trunk(
        x_sample, pos_s, cache_mla_ckv, cache_mla_krope, state_s5_re, state_s5_im, state_ret, *weights)
    return (y_prompt, y_sample, ckv_p, krope_p, s5_re_p, s5_im_p, ret_p,
            ckv_s, krope_s, s5_re_s, s5_im_s, ret_s)
```

```python
import functools
import math

import jax
import jax.numpy as jnp
import numpy as np
from jax import lax
from jax.experimental import pallas as pl
from jax.experimental.pallas import tpu as pltpu

F32 = jnp.float32
BF16 = jnp.bfloat16

CHUNK = 64
MLA_V = 128
MLA_NOPE = 128
MLA_ROPE = 64
MLA_HEADS = 8
Q_LORA = 512
KV_LORA = 256
MLA_SCALE = (MLA_NOPE + MLA_ROPE) ** -0.5
S5_GROUP = 16
S5_STATE = 64
RET_HEADS = 8
ROPE_THETA = 10000.0
EPS = 1e-6
GN_EPS = 1e-5
LOG2E = 1.4426950408889634

LANE = 128
QK_PAD = 256
S5_L = 64
NEG_BIG = -1e30
VMEM_LIMIT = 56 * 2 ** 20


def _cp(sem):
    return pltpu.CompilerParams(dimension_semantics=sem, vmem_limit_bytes=VMEM_LIMIT)


def _rms(x, g):
    return x * lax.rsqrt(jnp.mean(x * x, axis=-1, keepdims=True) + EPS) * g


def _dot(a, b):
    return jnp.dot(a, b, preferred_element_type=F32)


def _dot_nt(a, b):
    return lax.dot_general(a, b, (((1,), (1,)), ((), ())), preferred_element_type=F32)


def _dot_tn(a, b):
    return lax.dot_general(a, b, (((0,), (0,)), ((), ())), preferred_element_type=F32)


def _ab_in_kernel(x_ref, g_ref, win_ref, qn_ref, kvn_ref, wq_ref, cs1_ref, cs2_ref,
                  q_out, ckv_out, kr_out, u_out):
    xn = _rms(x_ref[...], g_ref[...]).astype(BF16)
    proj = _dot(xn, win_ref[...])
    cqn = _rms(proj[:, :Q_LORA], qn_ref[...]).astype(BF16)
    o = Q_LORA
    ckv_out[...] = _rms(proj[:, o:o + KV_LORA], kvn_ref[...])
    o += KV_LORA
    cs1 = cs1_ref[...]
    cs2 = cs2_ref[...]
    kr_out[...] = proj[:, o:o + LANE] * cs1 + proj[:, o + LANE:o + 2 * LANE] * cs2
    o += 2 * LANE
    u_out[...] = proj[:, o:]
    q = _dot(cqn, wq_ref[...])
    qs = MLA_SCALE * LOG2E
    for h in range(MLA_HEADS):
        b = h * 3 * LANE
        q_out[:, h * QK_PAD:h * QK_PAD + LANE] = (q[:, b:b + LANE] * qs).astype(BF16)
        rope = q[:, b + LANE:b + 2 * LANE] * cs1 + q[:, b + 2 * LANE:b + 3 * LANE] * cs2
        q_out[:, h * QK_PAD + LANE:(h + 1) * QK_PAD] = (rope * qs).astype(BF16)


def _ab_in(x, g, win, qn, kvn, wq, cs1, cs2, tm):
    T, D = x.shape
    row = lambda i: (i, 0)
    fix = lambda i: (0, 0)
    s5w = win.shape[1] - Q_LORA - KV_LORA - 2 * LANE
    return pl.pallas_call(
        _ab_in_kernel,
        grid=(T // tm,),
        in_specs=[pl.BlockSpec((tm, D), row), pl.BlockSpec((1, D), fix),
                  pl.BlockSpec(win.shape, fix), pl.BlockSpec((1, Q_LORA), fix),
                  pl.BlockSpec((1, KV_LORA), fix), pl.BlockSpec(wq.shape, fix),
                  pl.BlockSpec((tm, LANE), row), pl.BlockSpec((tm, LANE), row)],
        out_specs=[pl.BlockSpec((tm, MLA_HEADS * QK_PAD), row), pl.BlockSpec((tm, KV_LORA), row),
                   pl.BlockSpec((tm, LANE), row), pl.BlockSpec((tm, s5w), row)],
        out_shape=[jax.ShapeDtypeStruct((T, MLA_HEADS * QK_PAD), BF16),
                   jax.ShapeDtypeStruct((T, KV_LORA), F32),
                   jax.ShapeDtypeStruct((T, LANE), F32),
                   jax.ShapeDtypeStruct((T, s5w), F32)],
        compiler_params=_cp(("parallel",)),
        name="ab_in",
    )(x, g, win, qn, kvn, wq, cs1, cs2)


def _kv_kernel(ckv_ref, kr_ref, w_ref, k_out, v_out):
    kv = _dot(ckv_ref[...].astype(BF16), w_ref[...])
    kr = kr_ref[...].astype(BF16)
    for h in range(MLA_HEADS):
        k_out[:, h * QK_PAD:h * QK_PAD + LANE] = kv[:, h * LANE:(h + 1) * LANE].astype(BF16)
        k_out[:, h * QK_PAD + LANE:(h + 1) * QK_PAD] = kr
    v_out[...] = kv[:, MLA_HEADS * MLA_NOPE:].astype(BF16)


def _kv_proj(ckv, kr, w, tm):
    R = ckv.shape[0]
    row = lambda i: (i, 0)
    return pl.pallas_call(
        _kv_kernel,
        grid=(R // tm,),
        in_specs=[pl.BlockSpec((tm, KV_LORA), row), pl.BlockSpec((tm, LANE), row),
                  pl.BlockSpec(w.shape, lambda i: (0, 0))],
        out_specs=[pl.BlockSpec((tm, MLA_HEADS * QK_PAD), row),
                   pl.BlockSpec((tm, MLA_HEADS * MLA_V), row)],
        out_shape=[jax.ShapeDtypeStruct((R, MLA_HEADS * QK_PAD), BF16),
                   jax.ShapeDtypeStruct((R, MLA_HEADS * MLA_V), BF16)],
        compiler_params=_cp(("parallel",)),
        name="kv_proj",
    )(ckv, kr, w)


def _attn_long_kernel(q_ref, k_ref, v_ref, o_ref, *, t):
    qi = pl.program_id(2)
    q = q_ref[...]

    def step(kt, carry, masked):
        m, l, acc = carry
        start = pl.multiple_of(kt * t, t)
        k = k_ref[pl.ds(start, t), :]
        v = v_ref[pl.ds(start, t), :]
        s = _dot_nt(q, k)
        if masked:
            r = lax.broadcasted_iota(jnp.int32, (t, t), 0) // CHUNK
            c = lax.broadcasted_iota(jnp.int32, (t, t), 1) // CHUNK
            s = jnp.where(c <= r, s, NEG_BIG)
        m_new = jnp.maximum(m, jnp.max(s, axis=1, keepdims=True))
        alpha = jnp.exp2(m - m_new)
        p = jnp.exp2(s - m_new)
        l = alpha * l + jnp.sum(p, axis=1, keepdims=True)
        acc = alpha * acc + _dot(p.astype(BF16), v)
        return m_new, l, acc

    init = (jnp.full((t, 1), NEG_BIG, F32), jnp.zeros((t, 1), F32), jnp.zeros((t, MLA_V), F32))
    carry = lax.fori_loop(0, qi, lambda kt, c: step(kt, c, False), init)
    _, l, acc = step(qi, carry, True)
    o_ref[...] = (acc / l).astype(BF16)


def _attn_long(q, k, v, nb, s, t):
    nq = s // t
    return pl.pallas_call(
        functools.partial(_attn_long_kernel, t=t),
        grid=(nb, MLA_HEADS, nq),
        in_specs=[pl.BlockSpec((t, QK_PAD), lambda b, h, i: (b * nq + i, h)),
                  pl.BlockSpec((s, QK_PAD), lambda b, h, i: (b, h)),
                  pl.BlockSpec((s, MLA_V), lambda b, h, i: (b, h))],
        out_specs=pl.BlockSpec((t, MLA_V), lambda b, h, i: (b * nq + i, h)),
        out_shape=jax.ShapeDtypeStruct((nb * s, MLA_HEADS * MLA_V), BF16),
        compiler_params=_cp(("parallel", "parallel", "arbitrary")),
        name="attn_long",
    )(q, k, v)


def _attn_short_kernel(q_ref, k_ref, v_ref, o_ref, *, q_pos0, n_keys):
    sq = q_ref.shape[0]
    skp = k_ref.shape[0]
    s = _dot_nt(q_ref[...], k_ref[...])
    qpos = q_pos0 + lax.broadcasted_iota(jnp.int32, (sq, skp), 0)
    kpos = lax.broadcasted_iota(jnp.int32, (sq, skp), 1)
    ok = (kpos // CHUNK <= qpos // CHUNK) & (kpos < n_keys)
    s = jnp.where(ok, s, NEG_BIG)
    m = jnp.max(s, axis=1, keepdims=True)
    p = jnp.exp2(s - m)
    l = jnp.sum(p, axis=1, keepdims=True)
    o_ref[...] = (_dot(p.astype(BF16), v_ref[...]) / l).astype(BF16)


def _attn_short(q, k, v, nb, sq, skp, q_row0, k_row0, q_pos0, n_keys):
    qb0 = q_row0 // sq
    kb0 = k_row0 // skp
    return pl.pallas_call(
        functools.partial(_attn_short_kernel, q_pos0=q_pos0, n_keys=n_keys),
        grid=(nb, MLA_HEADS),
        in_specs=[pl.BlockSpec((sq, QK_PAD), lambda b, h: (qb0 + b, h)),
                  pl.BlockSpec((skp, QK_PAD), lambda b, h: (kb0 + b, h)),
                  pl.BlockSpec((skp, MLA_V), lambda b, h: (kb0 + b, h))],
        out_specs=pl.BlockSpec((sq, MLA_V), lambda b, h: (b, h)),
        out_shape=jax.ShapeDtypeStruct((nb * sq, MLA_HEADS * MLA_V), BF16),
        compiler_params=_cp(("parallel", "parallel")),
        name="attn_short",
    )(q, k, v)


def _s5_param_kernel(lr_ref, li_ref, ldt_ref, btr_ref, bti_ref, cr_ref, ci_ref,
                     m_out, wr_out, wi_out, vr_out, vi_out, alr_out, ali_out,
                     car_sc, cai_sc):
    L, P = S5_L, S5_GROUP
    lr = jnp.minimum(lr_ref[...], -1e-4)
    li = li_ref[...]
    dt = jnp.exp(ldt_ref[...])
    mag = jnp.exp(lr * dt)
    a_re = mag * jnp.cos(li * dt)
    a_im = mag * jnp.sin(li * dt)
    den = lr * lr + li * li
    f_re = ((a_re - 1.0) * lr + a_im * li) / den
    f_im = (a_im * lr - (a_re - 1.0) * li) / den
    btr = btr_ref[...]
    bti = bti_ref[...]
    bb_re = f_re * btr - f_im * bti
    bb_im = f_re * bti + f_im * btr
    cr = cr_ref[...]
    ci = ci_ref[...]

    def apow(t):
        m = jnp.exp((lr * dt) * t)
        ang = (li * dt) * t
        return m * jnp.cos(ang), m * jnp.sin(ang)

    for i in range(L):
        rows = slice(i * P, (i + 1) * P)
        pr, pi = apow(float(L - 1 - i))
        wr_out[rows, :] = (pr * bb_re - pi * bb_im).astype(BF16)
        wi_out[rows, :] = (pr * bb_im + pi * bb_re).astype(BF16)
        pr, pi = apow(float(i + 1))
        vr_out[rows, :] = (cr * pr - ci * pi).astype(BF16)
        vi_out[rows, :] = (-(cr * pi + ci * pr)).astype(BF16)
        pr, pi = apow(float(i))
        car_sc[rows, :] = cr * pr - ci * pi
        cai_sc[rows, :] = cr * pi + ci * pr
    pr, pi = apow(float(L))
    alr_out[...] = pr
    ali_out[...] = pi
    hp = lax.Precision.HIGHEST
    nt = (((1,), (1,)), ((), ()))
    krow = (lax.dot_general(bb_re, car_sc[...], nt, precision=hp, preferred_element_type=F32)
            - lax.dot_general(bb_im, cai_sc[...], nt, precision=hp, preferred_element_type=F32))
    lane = lax.broadcasted_iota(jnp.int32, (P, L * P), 1)
    for i in range(L):
        shifted = krow if i == 0 else pltpu.roll(krow, i * P, axis=1)
        m_out[i * P:(i + 1) * P, :] = jnp.where(lane >= i * P, shifted, 0.0).astype(BF16)


def _s5_params(lam_re, lam_im, log_dt, b_re, b_im, c_re, c_im):
    G, N = lam_re.shape
    P, L = S5_GROUP, S5_L
    LP = L * P
    v3 = lambda a: a.reshape(G, 1, N)
    ldt = jnp.broadcast_to(log_dt[:, None, None], (G, 1, N))
    btr = jnp.swapaxes(b_re, 1, 2)
    bti = jnp.swapaxes(b_im, 1, 2)
    vec = pl.BlockSpec((None, 1, N), lambda g: (g, 0, 0))
    mat = pl.BlockSpec((None, P, N), lambda g: (g, 0, 0))
    big = pl.BlockSpec((None, LP, N), lambda g: (g, 0, 0))
    return pl.pallas_call(
        _s5_param_kernel,
        grid=(G,),
        in_specs=[vec, vec, vec, mat, mat, mat, mat],
        out_specs=[pl.BlockSpec((None, LP, LP), lambda g: (g, 0, 0)), big, big, big, big, vec, vec],
        out_shape=[jax.ShapeDtypeStruct((G, LP, LP), BF16)]
        + [jax.ShapeDtypeStruct((G, LP, N), BF16)] * 4
        + [jax.ShapeDtypeStruct((G, 1, N), F32)] * 2,
        scratch_shapes=[pltpu.VMEM((LP, N), F32), pltpu.VMEM((LP, N), F32)],
        compiler_params=_cp(("parallel",)),
        name="s5_params",
    )(v3(lam_re), v3(lam_im), ldt, btr, bti, c_re, c_im)


def _s5_main_kernel(*refs, nseq, nchunk, has_state):
    if has_state:
        (u_ref, m_ref, wr_ref, wi_ref, vr_ref, vi_ref, alr_ref, ali_ref, d_ref, h0r_ref, h0i_ref,
         y_out, hfr_out, hfi_out, xr_sc, xi_sc, hr_sc, hi_sc) = refs
    else:
        (u_ref, m_ref, wr_ref, wi_ref, vr_ref, vi_ref, alr_ref, ali_ref, d_ref,
         y_out, hfr_out, hfi_out, xr_sc, xi_sc, hr_sc, hi_sc) = refs
    u = u_ref[...]
    ub = u.astype(BF16)
    xr_sc[...] = _dot(ub, wr_ref[...])
    xi_sc[...] = _dot(ub, wi_ref[...])
    ar = alr_ref[...]
    ai = ali_ref[...]
    n = ar.shape[-1]
    if has_state:
        init = (h0r_ref[...], h0i_ref[...])
    else:
        init = (jnp.zeros((nseq, n), F32), jnp.zeros((nseq, n), F32))

    def body(c, carry):
        sr, si = carry
        rows = pl.ds(pl.multiple_of(c * nseq, nseq), nseq)
        hr_sc[rows, :] = sr
        hi_sc[rows, :] = si
        return (ar * sr - ai * si + xr_sc[rows, :], ar * si + ai * sr + xi_sc[rows, :])

    sr, si = lax.fori_loop(0, nchunk, body, init)
    hfr_out[...] = sr
    hfi_out[...] = si
    y_out[...] = (_dot(ub, m_ref[...])
                  + _dot_nt(hr_sc[...].astype(BF16), vr_ref[...])
                  + _dot_nt(hi_sc[...].astype(BF16), vi_ref[...])
                  + d_ref[...] * u)


def _s5_main(u, params, d_tiled, h0, nseq, nchunk):
    m, wr, wi, vr, vi, alr, ali = params
    G, R, LP = u.shape
    N = alr.shape[-1]
    has_state = h0 is not None
    g3 = lambda g: (g, 0, 0)
    big = pl.BlockSpec((None, LP, N), g3)
    vec = pl.BlockSpec((None, 1, N), g3)
    st = pl.BlockSpec((None, nseq, N), g3)
    in_specs = [pl.BlockSpec((None, R, LP), g3), pl.BlockSpec((None, LP, LP), g3),
                big, big, big, big, vec, vec, pl.BlockSpec((None, 1, LP), g3)]
    args = [u, m, wr, wi, vr, vi, alr, ali, d_tiled]
    if has_state:
        in_specs += [st, st]
        args += list(h0)
    return pl.pallas_call(
        functools.partial(_s5_main_kernel, nseq=nseq, nchunk=nchunk, has_state=has_state),
        grid=(G,),
        in_specs=in_specs,
        out_specs=[pl.BlockSpec((None, R, LP), g3), st, st],
        out_shape=[jax.ShapeDtypeStruct((G, R, LP), F32),
                   jax.ShapeDtypeStruct((G, nseq, N), F32),
                   jax.ShapeDtypeStruct((G, nseq, N), F32)],
        scratch_shapes=[pltpu.VMEM((R, N), F32)] * 4,
        compiler_params=_cp(("parallel",)),
        name="s5_main",
    )(*args)


def _ab_out_kernel(x_ref, attn_ref, y_ref, wglu_ref, bglu_ref, wo_ref, o_ref):
    y = y_ref[...]
    z = y * (0.5 * (1.0 + jnp.tanh(math.sqrt(2.0 / math.pi) * (y + 0.044715 * (y * y * y)))))
    gate = 1.0 / (1.0 + jnp.exp(-(_dot(z.astype(BF16), wglu_ref[...]) + bglu_ref[...])))
    ssm = (z * gate).astype(BF16)
    na = attn_ref.shape[1]
    o_ref[...] = x_ref[...] + _dot(attn_ref[...], wo_ref[:na, :]) + _dot(ssm, wo_ref[na:, :])


def _ab_out(x, attn, y, wglu, bglu, wo, tm):
    T, D = x.shape
    row = lambda i: (i, 0)
    fix = lambda i: (0, 0)
    return pl.pallas_call(
        _ab_out_kernel,
        grid=(T // tm,),
        in_specs=[pl.BlockSpec((tm, D), row), pl.BlockSpec((tm, attn.shape[1]), row),
                  pl.BlockSpec((tm, y.shape[1]), row), pl.BlockSpec(wglu.shape, fix),
                  pl.BlockSpec(bglu.shape, fix), pl.BlockSpec(wo.shape, fix)],
        out_specs=pl.BlockSpec((tm, D), row),
        out_shape=jax.ShapeDtypeStruct((T, D), F32),
        compiler_params=_cp(("parallel",)),
        name="ab_out",
    )(x, attn, y, wglu, bglu, wo)


def _mlp_kernel(x_ref, g_ref, wup_ref, wdn_ref, gf_ref, o_ref, xn_sc, acc_sc, *, final_norm):
    f = pl.program_id(1)

    @pl.when(f == 0)
    def _():
        xn_sc[...] = _rms(x_ref[...], g_ref[...]).astype(BF16)
        acc_sc[...] = jnp.zeros_like(acc_sc)

    a = jnp.maximum(_dot(xn_sc[...], wup_ref[...]), 0.0)
    acc_sc[...] += _dot((a * a).astype(BF16), wdn_ref[...])

    @pl.when(f == pl.num_programs(1) - 1)
    def _():
        r = x_ref[...] + acc_sc[...]
        if final_norm:
            r = _rms(r, gf_ref[...])
        o_ref[...] = r


def _mlp(x, g, wup, wdn, gf, final_norm, tm, tf):
    T, D = x.shape
    FF = wup.shape[1]
    return pl.pallas_call(
        functools.partial(_mlp_kernel, final_norm=final_norm),
        grid=(T // tm, FF // tf),
        in_specs=[pl.BlockSpec((tm, D), lambda i, f: (i, 0)), pl.BlockSpec((1, D), lambda i, f: (0, 0)),
                  pl.BlockSpec((D, tf), lambda i, f: (0, f)), pl.BlockSpec((tf, D), lambda i, f: (f, 0)),
                  pl.BlockSpec((1, D), lambda i, f: (0, 0))],
        out_specs=pl.BlockSpec((tm, D), lambda i, f: (i, 0)),
        out_shape=jax.ShapeDtypeStruct((T, D), F32),
        scratch_shapes=[pltpu.VMEM((tm, D), BF16), pltpu.VMEM((tm, D), F32)],
        compiler_params=_cp(("parallel", "arbitrary")),
        name="mlp",
    )(x, g, wup, wdn, gf)


def _c_in_kernel(x_ref, g_ref, w_ref, cos_ref, sin_ref, o_ref, xn_sc, *, n_rot, n_q, n_lin, dk):
    j = pl.program_id(1)

    @pl.when(j == 0)
    def _():
        xn_sc[...] = _rms(x_ref[...], g_ref[...]).astype(BF16)

    acc = _dot(xn_sc[...], w_ref[...])
    tn = acc.shape[1]
    half = dk // 2

    @pl.when(j < n_rot)
    def _():
        cos = cos_ref[...]
        sin = sin_ref[...]
        scale = jnp.where(j < n_q, 1.0, dk ** -0.5)
        for s in range(0, tn, dk):
            x1 = acc[:, s:s + half]
            x2 = acc[:, s + half:s + dk]
            o_ref[:, s:s + half] = ((x1 * cos - x2 * sin) * scale).astype(BF16)
            o_ref[:, s + half:s + dk] = ((x1 * sin + x2 * cos) * scale).astype(BF16)

    @pl.when((j >= n_rot) & (j < n_lin))
    def _():
        o_ref[...] = acc.astype(BF16)

    @pl.when(j >= n_lin)
    def _():
        o_ref[...] = (acc / (1.0 + jnp.exp(-acc))).astype(BF16)


def _c_in(x, g, w, cos, sin, dk, dv, tm, tn):
    T, D = x.shape
    N = w.shape[1]
    qk = RET_HEADS * dk
    n_q = qk // tn
    n_rot = 2 * qk // tn
    n_lin = (2 * qk + RET_HEADS * dv) // tn
    return pl.pallas_call(
        functools.partial(_c_in_kernel, n_rot=n_rot, n_q=n_q, n_lin=n_lin, dk=dk),
        grid=(T // tm, N // tn),
        in_specs=[pl.BlockSpec((tm, D), lambda i, j: (i, 0)), pl.BlockSpec((1, D), lambda i, j: (0, 0)),
                  pl.BlockSpec((D, tn), lambda i, j: (0, j)),
                  pl.BlockSpec((tm, dk // 2), lambda i, j: (i, 0)),
                  pl.BlockSpec((tm, dk // 2), lambda i, j: (i, 0))],
        out_specs=pl.BlockSpec((tm, tn), lambda i, j: (i, j)),
        out_shape=jax.ShapeDtypeStruct((T, N), BF16),
        scratch_shapes=[pltpu.VMEM((tm, D), BF16)],
        compiler_params=_cp(("parallel", "arbitrary")),
        name="c_in",
    )(x, g, w, cos, sin)


def _ret_kernel(*refs, c_len, has_state):
    if has_state:
        q_ref, k_ref, v_ref, g_ref, gn_ref, s0_ref, o_ref, sf_ref, st_sc = refs
    else:
        q_ref, k_ref, v_ref, g_ref, gn_ref, o_ref, sf_ref, st_sc = refs
    h = pl.program_id(1)
    c = pl.program_id(2)

    @pl.when(c == 0)
    def _():
        if has_state:
            st_sc[...] = s0_ref[...]
        else:
            st_sc[...] = jnp.zeros_like(st_sc)

    hf = jnp.full((1, 1), h, jnp.int32).astype(F32)
    lg = jnp.log(1.0 - jnp.exp2(-5.0 - hf))
    q = q_ref[...]
    k = k_ref[...]
    v = v_ref[...]
    row = lax.broadcasted_iota(jnp.int32, (c_len, c_len), 0)
    col = lax.broadcasted_iota(jnp.int32, (c_len, c_len), 1)
    diff = (row - col).astype(F32)
    decay = jnp.where(diff >= 0.0, jnp.exp(lg * jnp.maximum(diff, 0.0)), 0.0)
    inner = _dot((_dot_nt(q, k) * decay).astype(BF16), v)
    idx = lax.broadcasted_iota(jnp.int32, (c_len, 1), 0).astype(F32)
    qd = (q.astype(F32) * jnp.exp(lg * (idx + 1.0))).astype(BF16)
    st = st_sc[...]
    o = inner + _dot(qd, st.astype(BF16))
    kd = (k.astype(F32) * jnp.exp(lg * (c_len - 1.0 - idx))).astype(BF16)
    st_sc[...] = jnp.exp(lg * float(c_len)) * st + _dot_tn(kd, v)
    mu = jnp.mean(o, axis=-1, keepdims=True)
    d = o - mu
    var = jnp.mean(d * d, axis=-1, keepdims=True)
    on = d * lax.rsqrt(var + GN_EPS) * gn_ref[...]
    o_ref[...] = (on * g_ref[...].astype(F32)).astype(BF16)

    @pl.when(c == pl.num_programs(2) - 1)
    def _():
        sf_ref[...] = st_sc[...]


def _retention(proj, gn, state0, o_prev, nb, s, c_len, row0, dk, dv):
    T = proj.shape[0]
    H = RET_HEADS
    nc = s // c_len
    rb0 = row0 // c_len
    has_state = state0 is not None
    rowblk = lambda b, h, c: rb0 + b * nc + c
    kb = H * dk // dk
    vb = 2 * H * dk // dv
    in_specs = [pl.BlockSpec((c_len, dk), lambda b, h, c: (rowblk(b, h, c), h)),
                pl.BlockSpec((c_len, dk), lambda b, h, c: (rowblk(b, h, c), kb + h)),
                pl.BlockSpec((c_len, dv), lambda b, h, c: (rowblk(b, h, c), vb + h)),
                pl.BlockSpec((c_len, dv), lambda b, h, c: (rowblk(b, h, c), vb + H + h)),
                pl.BlockSpec((1, dv), lambda b, h, c: (0, h))]
    args = [proj, proj, proj, proj, gn]
    if has_state:
        in_specs.append(pl.BlockSpec((None, None, dk, dv), lambda b, h, c: (b, h, 0, 0)))
        args.append(state0)
    aliases = {}
    if o_prev is not None:
        in_specs.append(pl.BlockSpec(memory_space=pl.ANY))
        args.append(o_prev)
        aliases = {len(args) - 1: 0}

    def kern(*refs):
        if o_prev is not None:
            n_in = len(args)
            refs = refs[:n_in - 1] + refs[n_in:]
        _ret_kernel(*refs, c_len=c_len, has_state=has_state)

    return pl.pallas_call(
        kern,
        grid=(nb, H, nc),
        in_specs=in_specs,
        out_specs=[pl.BlockSpec((c_len, dv), lambda b, h, c: (rowblk(b, h, c), h)),
                   pl.BlockSpec((None, None, dk, dv), lambda b, h, c: (b, h, 0, 0))],
        out_shape=[jax.ShapeDtypeStruct((T, H * dv), BF16),
                   jax.ShapeDtypeStruct((nb, H, dk, dv), F32)],
        scratch_shapes=[pltpu.VMEM((dk, dv), F32)],
        input_output_aliases=aliases,
        compiler_params=_cp(("parallel", "parallel", "arbitrary")),
        name="retention",
    )(*args)


def _c_out_kernel(x_ref, o_ref, w_ref, y_ref):
    y_ref[...] = x_ref[...] + _dot(o_ref[...], w_ref[...])


def _c_out(x, o, w, tm, tn):
    T, D = x.shape
    K = o.shape[1]
    return pl.pallas_call(
        _c_out_kernel,
        grid=(T // tm, D // tn),
        in_specs=[pl.BlockSpec((tm, tn), lambda i, j: (i, j)), pl.BlockSpec((tm, K), lambda i, j: (i, 0)),
                  pl.BlockSpec((K, tn), lambda i, j: (0, j))],
        out_specs=pl.BlockSpec((tm, tn), lambda i, j: (i, j)),
        out_shape=jax.ShapeDtypeStruct((T, D), F32),
        compiler_params=_cp(("parallel", "arbitrary")),
        name="c_out",
    )(x, o, w)


def _rope_tables(pos, half):
    inv = ROPE_THETA ** (-jnp.arange(half, dtype=F32) / half)
    ang = pos.astype(F32)[:, None] * inv[None, :]
    return jnp.cos(ang), jnp.sin(ang)


def _swap_halves(w):
    half = w.shape[-1] // 2
    return jnp.concatenate([w[..., half:], w[..., :half]], axis=-1)


def _prep_w_in_ab(w):
    o1 = Q_LORA + KV_LORA
    o2 = o1 + MLA_ROPE
    kr = w[:, o1:o2]
    ks = _swap_halves(kr)
    return jnp.concatenate([w[:, :o1], kr, ks, ks, kr, w[:, o2:]], axis=1).astype(BF16)


def _prep_w_q_b(w):
    w = w.reshape(Q_LORA, MLA_HEADS, MLA_NOPE + MLA_ROPE)
    rope = w[..., MLA_NOPE:]
    swap = _swap_halves(rope)
    out = jnp.concatenate([w[..., :MLA_NOPE], rope, swap, swap, rope], axis=-1)
    return out.reshape(Q_LORA, -1).astype(BF16)


def _prep_w_kv_b(w):
    w = w.reshape(KV_LORA, MLA_HEADS, MLA_NOPE + MLA_V)
    return jnp.concatenate([w[..., :MLA_NOPE].reshape(KV_LORA, -1),
                            w[..., MLA_NOPE:].reshape(KV_LORA, -1)], axis=1).astype(BF16)


def kernel(x_prompt, x_sample, cache_mla_ckv, cache_mla_krope, state_s5_re, state_s5_im, state_ret,
           norm_mix, norm_mlp, norm_final, w_in_ab, q_a_norm, kv_a_norm, w_q_b, w_kv_b,
           s5_lam_re, s5_lam_im, s5_log_dt, s5_b_re, s5_b_im, s5_c_re, s5_c_im, s5_d, w_glu, b_glu,
           w_out_ab, w_in_c, ret_gn, w_out_c, w_up, w_down):
    BP, SP, D = x_prompt.shape
    BS, SS, _ = x_sample.shape
    past = cache_mla_ckv.shape[2]
    depth = norm_mix.shape[0]
    G, N = s5_lam_re.shape[1:]
    P, L = S5_GROUP, S5_L
    dk, dv = state_ret.shape[3:]
    TP, TS = BP * SP, BS * SS
    T = TP + TS
    TM = 512

    x = jnp.concatenate([x_prompt.reshape(TP, D), x_sample.reshape(TS, D)], axis=0)
    pos = jnp.concatenate([jnp.tile(jnp.arange(SP), BP), jnp.tile(past + jnp.arange(SS), BS)])

    c32, s32 = _rope_tables(pos, MLA_ROPE // 2)
    z64 = jnp.zeros((T, LANE - MLA_ROPE), F32)
    cs1 = jnp.concatenate([c32, c32, z64], axis=1)
    cs2 = jnp.concatenate([-s32, s32, z64], axis=1)
    cos_r, sin_r = _rope_tables(pos, dk // 2)

    skp = -(-(past + SS) // LANE) * LANE
    ckv_p, kr_p, s5r_p, s5i_p, ret_p = [], [], [], [], []
    ckv_s, kr_s, s5r_s, s5i_s, ret_s = [], [], [], [], []

    for layer in range(depth):
        i = layer // 2
        g_mix = norm_mix[layer].reshape(1, D)
        if layer % 2 == 0:
            q, ckv, kr, u = _ab_in(x, g_mix, _prep_w_in_ab(w_in_ab[i]), q_a_norm[i].reshape(1, -1),
                                   kv_a_norm[i].reshape(1, -1), _prep_w_q_b(w_q_b[i]), cs1, cs2, 256)
            ckv_p.append(ckv[:TP].reshape(BP, SP, KV_LORA))
            kr_p.append(kr[:TP, :MLA_ROPE].reshape(BP, SP, MLA_ROPE))
            ckv_s.append(ckv[TP:].reshape(BS, SS, KV_LORA))
            kr_s.append(kr[TP:, :MLA_ROPE].reshape(BS, SS, MLA_ROPE))
            pad = skp - past - SS
            ckv_hist = jnp.concatenate([cache_mla_ckv[i], ckv[TP:].reshape(BS, SS, KV_LORA),
                                        jnp.zeros((BS, pad, KV_LORA), F32)], axis=1)
            kr_past = jnp.pad(cache_mla_krope[i], ((0, 0), (0, 0), (0, LANE - MLA_ROPE)))
            kr_hist = jnp.concatenate([kr_past, kr[TP:].reshape(BS, SS, LANE),
                                       jnp.zeros((BS, pad, LANE), F32)], axis=1)
            w_kv = _prep_w_kv_b(w_kv_b[i])
            k_p, v_p = _kv_proj(ckv, kr, w_kv, 512)
            k_s, v_s = _kv_proj(ckv_hist.reshape(BS * skp, KV_LORA), kr_hist.reshape(BS * skp, LANE),
                                w_kv, skp)
            attn = jnp.concatenate([
                _attn_long(q, k_p, v_p, BP, SP, 256),
                _attn_short(q, k_s, v_s, BS, SS, skp, TP, 0, past, past + SS)], axis=0)
            params = _s5_params(s5_lam_re[i], s5_lam_im[i], s5_log_dt[i], s5_b_re[i], s5_b_im[i],
                                s5_c_re[i], s5_c_im[i])
            d_t = jnp.tile(s5_d[i], (1, L)).reshape(G, 1, L * P)
            ncp, ncs = SP // L, SS // L
            u_p = u[:TP].reshape(BP, ncp, L, G, P).transpose(3, 1, 0, 2, 4).reshape(G, ncp * BP, L * P)
            u_s = u[TP:].reshape(BS, ncs, L, G, P).transpose(3, 1, 0, 2, 4).reshape(G, ncs * BS, L * P)
            y_p, hr_p, hi_p = _s5_main(u_p, params, d_t, None, BP, ncp)
            h0 = (jnp.swapaxes(state_s5_re[i], 0, 1), jnp.swapaxes(state_s5_im[i], 0, 1))
            y_s, hr_s, hi_s = _s5_main(u_s, params, d_t, h0, BS, ncs)
            y_p = y_p.reshape(G, ncp, BP, L, P).transpose(2, 1, 3, 0, 4).reshape(TP, G * P)
            y_s = y_s.reshape(G, ncs, BS, L, P).transpose(2, 1, 3, 0, 4).reshape(TS, G * P)
            y = jnp.concatenate([y_p, y_s], axis=0)
            s5r_p.append(jnp.swapaxes(hr_p, 0, 1))
            s5i_p.append(jnp.swapaxes(hi_p, 0, 1))
            s5r_s.append(jnp.swapaxes(hr_s, 0, 1))
            s5i_s.append(jnp.swapaxes(hi_s, 0, 1))
            x = _ab_out(x, attn, y, w_glu[i].astype(BF16), b_glu[i].reshape(1, -1),
                        w_out_ab[i].astype(BF16), 256)
        else:
            proj = _c_in(x, g_mix, w_in_c[i].astype(BF16), cos_r, sin_r, dk, dv, TM, 512)
            gn = ret_gn[i].reshape(1, -1)
            o, st_p = _retention(proj, gn, None, None, BP, SP, 256, 0, dk, dv)
            o, st_s = _retention(proj, gn, state_ret[i], o, BS, SS, SS, TP, dk, dv)
            ret_p.append(st_p)
            ret_s.append(st_s)
            x = _c_out(x, o, w_out_c[i].astype(BF16), TM, 512)
        x = _mlp(x, norm_mlp[layer].reshape(1, D), w_up[layer].astype(BF16), w_down[layer].astype(BF16),
                 norm_final.reshape(1, D), layer == depth - 1, TM, 512)

    return (x[:TP].reshape(BP, SP, D), x[TP:].reshape(BS, SS, D),
            jnp.stack(ckv_p), jnp.stack(kr_p), jnp.stack(s5r_p), jnp.stack(s5i_p), jnp.stack(ret_p),
            jnp.stack(ckv_s), jnp.stack(kr_s), jnp.stack(s5r_s), jnp.stack(s5i_s), jnp.stack(ret_s))
```

```python
import functools
import math

import jax
import jax.numpy as jnp
import numpy as np
from jax import lax
from jax.experimental import pallas as pl
from jax.experimental.pallas import tpu as pltpu

F32 = jnp.float32
BF16 = jnp.bfloat16

CHUNK = 64
MLA_V = 128
MLA_NOPE = 128
MLA_ROPE = 64
MLA_HEADS = 8
Q_LORA = 512
KV_LORA = 256
MLA_SCALE = (MLA_NOPE + MLA_ROPE) ** -0.5
S5_GROUP = 16
S5_STATE = 64
RET_HEADS = 8
ROPE_THETA = 10000.0
EPS = 1e-6
GN_EPS = 1e-5
LOG2E = 1.4426950408889634

LANE = 128
QK_PAD = 256
S5_L = 64
RET_C = 256
RET_HPB = 4
ATT_TQ = 512
ATT_TK = 512
NEG_BIG = -1e30
VMEM_LIMIT = 56 * 2 ** 20


def _cp(sem):
    return pltpu.CompilerParams(dimension_semantics=sem, vmem_limit_bytes=VMEM_LIMIT)


def _rms(x, g):
    return x * lax.rsqrt(jnp.mean(x * x, axis=-1, keepdims=True) + EPS) * g


def _dot(a, b):
    return jnp.dot(a, b, preferred_element_type=F32)


def _dot_nt(a, b):
    return lax.dot_general(a, b, (((1,), (1,)), ((), ())), preferred_element_type=F32)


def _dot_tn(a, b):
    return lax.dot_general(a, b, (((0,), (0,)), ((), ())), preferred_element_type=F32)


def _ab_in_kernel(x_ref, g_ref, win_ref, qn_ref, kvn_ref, wq_ref, cs1_ref, cs2_ref,
                  q_out, ckv_out, kr_out, u_out):
    xn = _rms(x_ref[...], g_ref[...]).astype(BF16)
    proj = _dot(xn, win_ref[...])
    cqn = _rms(proj[:, :Q_LORA], qn_ref[...]).astype(BF16)
    o = Q_LORA
    ckv_out[...] = _rms(proj[:, o:o + KV_LORA], kvn_ref[...])
    o += KV_LORA
    cs1 = cs1_ref[...]
    cs2 = cs2_ref[...]
    kr_out[...] = proj[:, o:o + LANE] * cs1 + proj[:, o + LANE:o + 2 * LANE] * cs2
    o += 2 * LANE
    u_out[...] = proj[:, o:]
    q = _dot(cqn, wq_ref[...])
    qs = MLA_SCALE * LOG2E
    for h in range(MLA_HEADS):
        b = h * 3 * LANE
        q_out[:, h * QK_PAD:h * QK_PAD + LANE] = (q[:, b:b + LANE] * qs).astype(BF16)
        rope = q[:, b + LANE:b + 2 * LANE] * cs1 + q[:, b + 2 * LANE:b + 3 * LANE] * cs2
        q_out[:, h * QK_PAD + LANE:(h + 1) * QK_PAD] = (rope * qs).astype(BF16)


def _ab_in(x, g, win, qn, kvn, wq, cs1, cs2, tm):
    T, D = x.shape
    row = lambda i: (i, 0)
    fix = lambda i: (0, 0)
    s5w = win.shape[1] - Q_LORA - KV_LORA - 2 * LANE
    return pl.pallas_call(
        _ab_in_kernel,
        grid=(T // tm,),
        in_specs=[pl.BlockSpec((tm, D), row), pl.BlockSpec((1, D), fix),
                  pl.BlockSpec(win.shape, fix), pl.BlockSpec((1, Q_LORA), fix),
                  pl.BlockSpec((1, KV_LORA), fix), pl.BlockSpec(wq.shape, fix),
                  pl.BlockSpec((tm, LANE), row), pl.BlockSpec((tm, LANE), row)],
        out_specs=[pl.BlockSpec((tm, MLA_HEADS * QK_PAD), row), pl.BlockSpec((tm, KV_LORA), row),
                   pl.BlockSpec((tm, LANE), row), pl.BlockSpec((tm, s5w), row)],
        out_shape=[jax.ShapeDtypeStruct((T, MLA_HEADS * QK_PAD), BF16),
                   jax.ShapeDtypeStruct((T, KV_LORA), F32),
                   jax.ShapeDtypeStruct((T, LANE), F32),
                   jax.ShapeDtypeStruct((T, s5w), F32)],
        compiler_params=_cp(("parallel",)),
        name="ab_in",
    )(x, g, win, qn, kvn, wq, cs1, cs2)


def _kv_kernel(ckv_ref, kr_ref, w_ref, k_out, v_out):
    kv = _dot(ckv_ref[...].astype(BF16), w_ref[...])
    kr = kr_ref[...].astype(BF16)
    for h in range(MLA_HEADS):
        k_out[:, h * QK_PAD:h * QK_PAD + LANE] = kv[:, h * LANE:(h + 1) * LANE].astype(BF16)
        k_out[:, h * QK_PAD + LANE:(h + 1) * QK_PAD] = kr
    v_out[...] = kv[:, MLA_HEADS * MLA_NOPE:].astype(BF16)


def _kv_proj(ckv, kr, w, tm):
    R = ckv.shape[0]
    row = lambda i: (i, 0)
    return pl.pallas_call(
        _kv_kernel,
        grid=(R // tm,),
        in_specs=[pl.BlockSpec((tm, KV_LORA), row), pl.BlockSpec((tm, LANE), row),
                  pl.BlockSpec(w.shape, lambda i: (0, 0))],
        out_specs=[pl.BlockSpec((tm, MLA_HEADS * QK_PAD), row),
                   pl.BlockSpec((tm, MLA_HEADS * MLA_V), row)],
        out_shape=[jax.ShapeDtypeStruct((R, MLA_HEADS * QK_PAD), BF16),
                   jax.ShapeDtypeStruct((R, MLA_HEADS * MLA_V), BF16)],
        compiler_params=_cp(("parallel",)),
        name="kv_proj",
    )(ckv, kr, w)


def _attn_long_kernel(q_ref, k_ref, v_ref, o_ref, s_sc, m_sc, l_sc, acc_sc, *, tq, tk):
    qi = pl.program_id(2)
    n_full = (qi * tq) // tk

    def scores_to(slot, kt):
        s_sc[slot] = _dot_nt(q_ref[...], k_ref[pl.ds(pl.multiple_of(kt * tk, tk), tk), :])

    def consume(slot, kt, masked):
        s = s_sc[slot]
        if masked:
            qpos = qi * tq + lax.broadcasted_iota(jnp.int32, (tq, tk), 0)
            kpos = kt * tk + lax.broadcasted_iota(jnp.int32, (tq, tk), 1)
            s = jnp.where(kpos // CHUNK <= qpos // CHUNK, s, NEG_BIG)
        m = m_sc[...]
        m_new = jnp.maximum(m, jnp.max(s, axis=1, keepdims=True))
        alpha = jnp.exp2(m - m_new)
        p = jnp.exp2(s - pltpu.repeat(m_new, tk // LANE, axis=1))
        m_sc[...] = m_new
        l_sc[...] = alpha * l_sc[...] + jnp.sum(p, axis=1, keepdims=True)
        v = v_ref[pl.ds(pl.multiple_of(kt * tk, tk), tk), :]
        acc_sc[...] = alpha * acc_sc[...] + _dot(p.astype(BF16), v)

    m_sc[...] = jnp.full_like(m_sc, NEG_BIG)
    l_sc[...] = jnp.zeros_like(l_sc)
    acc_sc[...] = jnp.zeros_like(acc_sc)
    scores_to(0, 0)

    def pair(j, carry):
        scores_to(1, 2 * j + 1)
        consume(0, 2 * j, False)
        scores_to(0, 2 * j + 2)
        consume(1, 2 * j + 1, False)
        return carry

    lax.fori_loop(0, n_full // 2, pair, 0)

    @pl.when(n_full % 2 == 1)
    def _():
        scores_to(1, n_full)
        consume(0, n_full - 1, False)
        consume(1, n_full, True)

    @pl.when(n_full % 2 == 0)
    def _():
        consume(0, n_full, True)

    o_ref[...] = (acc_sc[...] / l_sc[...]).astype(BF16)


def _attn_long(q, k, v, nb, s, tq, tk):
    assert tk in (tq, 2 * tq) and s % tk == 0
    nq = s // tq
    return pl.pallas_call(
        functools.partial(_attn_long_kernel, tq=tq, tk=tk),
        grid=(nb, MLA_HEADS, nq),
        in_specs=[pl.BlockSpec((tq, QK_PAD), lambda b, h, i: (b * nq + i, h)),
                  pl.BlockSpec((s, QK_PAD), lambda b, h, i: (b, h)),
                  pl.BlockSpec((s, MLA_V), lambda b, h, i: (b, h))],
        out_specs=pl.BlockSpec((tq, MLA_V), lambda b, h, i: (b * nq + i, h)),
        out_shape=jax.ShapeDtypeStruct((nb * s, MLA_HEADS * MLA_V), BF16),
        scratch_shapes=[pltpu.VMEM((2, tq, tk), F32), pltpu.VMEM((tq, LANE), F32),
                        pltpu.VMEM((tq, LANE), F32), pltpu.VMEM((tq, MLA_V), F32)],
        compiler_params=_cp(("parallel", "parallel", "arbitrary")),
        name="attn_long",
    )(q, k, v)


def _attn_short_kernel(q_ref, k_ref, v_ref, o_ref, *, q_pos0, n_keys):
    sq = q_ref.shape[0]
    skp = k_ref.shape[0]
    s = _dot_nt(q_ref[...], k_ref[...])
    qpos = q_pos0 + lax.broadcasted_iota(jnp.int32, (sq, skp), 0)
    kpos = lax.broadcasted_iota(jnp.int32, (sq, skp), 1)
    ok = (kpos // CHUNK <= qpos // CHUNK) & (kpos < n_keys)
    s = jnp.where(ok, s, NEG_BIG)
    m = jnp.max(s, axis=1, keepdims=True)
    p = jnp.exp2(s - m)
    l = jnp.sum(p, axis=1, keepdims=True)
    o_ref[...] = (_dot(p.astype(BF16), v_ref[...]) / l).astype(BF16)


def _attn_short(q, k, v, nb, sq, skp, q_row0, k_row0, q_pos0, n_keys):
    qb0 = q_row0 // sq
    kb0 = k_row0 // skp
    return pl.pallas_call(
        functools.partial(_attn_short_kernel, q_pos0=q_pos0, n_keys=n_keys),
        grid=(nb, MLA_HEADS),
        in_specs=[pl.BlockSpec((sq, QK_PAD), lambda b, h: (qb0 + b, h)),
                  pl.BlockSpec((skp, QK_PAD), lambda b, h: (kb0 + b, h)),
                  pl.BlockSpec((skp, MLA_V), lambda b, h: (kb0 + b, h))],
        out_specs=pl.BlockSpec((sq, MLA_V), lambda b, h: (b, h)),
        out_shape=jax.ShapeDtypeStruct((nb * sq, MLA_HEADS * MLA_V), BF16),
        compiler_params=_cp(("parallel", "parallel")),
        name="attn_short",
    )(q, k, v)


def _s5_param_kernel(lr_ref, li_ref, ldt_ref, btr_ref, bti_ref, cr_ref, ci_ref,
                     m_out, wr_out, wi_out, vr_out, vi_out, alr_out, ali_out,
                     car_sc, cai_sc):
    L, P = S5_L, S5_GROUP
    lr = jnp.minimum(lr_ref[...], -1e-4)
    li = li_ref[...]
    dt = jnp.exp(ldt_ref[...])
    mag = jnp.exp(lr * dt)
    a_re = mag * jnp.cos(li * dt)
    a_im = mag * jnp.sin(li * dt)
    den = lr * lr + li * li
    f_re = ((a_re - 1.0) * lr + a_im * li) / den
    f_im = (a_im * lr - (a_re - 1.0) * li) / den
    btr = btr_ref[...]
    bti = bti_ref[...]
    bb_re = f_re * btr - f_im * bti
    bb_im = f_re * bti + f_im * btr
    cr = cr_ref[...]
    ci = ci_ref[...]

    def apow(t):
        m = jnp.exp((lr * dt) * t)
        ang = (li * dt) * t
        return m * jnp.cos(ang), m * jnp.sin(ang)

    for i in range(L):
        rows = slice(i * P, (i + 1) * P)
        pr, pi = apow(float(L - 1 - i))
        wr_out[rows, :] = (pr * bb_re - pi * bb_im).astype(BF16)
        wi_out[rows, :] = (pr * bb_im + pi * bb_re).astype(BF16)
        pr, pi = apow(float(i + 1))
        vr_out[rows, :] = (cr * pr - ci * pi).astype(BF16)
        vi_out[rows, :] = (-(cr * pi + ci * pr)).astype(BF16)
        pr, pi = apow(float(i))
        car_sc[rows, :] = cr * pr - ci * pi
        cai_sc[rows, :] = cr * pi + ci * pr
    pr, pi = apow(float(L))
    alr_out[...] = pr
    ali_out[...] = pi
    hp = lax.Precision.HIGHEST
    nt = (((1,), (1,)), ((), ()))
    krow = (lax.dot_general(bb_re, car_sc[...], nt, precision=hp, preferred_element_type=F32)
            - lax.dot_general(bb_im, cai_sc[...], nt, precision=hp, preferred_element_type=F32))
    lane = lax.broadcasted_iota(jnp.int32, (P, L * P), 1)
    for i in range(L):
        shifted = krow if i == 0 else pltpu.roll(krow, i * P, axis=1)
        m_out[i * P:(i + 1) * P, :] = jnp.where(lane >= i * P, shifted, 0.0).astype(BF16)


def _s5_params(lam_re, lam_im, log_dt, b_re, b_im, c_re, c_im):
    G, N = lam_re.shape
    P, L = S5_GROUP, S5_L
    LP = L * P
    v3 = lambda a: a.reshape(G, 1, N)
    ldt = jnp.broadcast_to(log_dt[:, None, None], (G, 1, N))
    btr = jnp.swapaxes(b_re, 1, 2)
    bti = jnp.swapaxes(b_im, 1, 2)
    vec = pl.BlockSpec((None, 1, N), lambda g: (g, 0, 0))
    mat = pl.BlockSpec((None, P, N), lambda g: (g, 0, 0))
    big = pl.BlockSpec((None, LP, N), lambda g: (g, 0, 0))
    return pl.pallas_call(
        _s5_param_kernel,
        grid=(G,),
        in_specs=[vec, vec, vec, mat, mat, mat, mat],
        out_specs=[pl.BlockSpec((None, LP, LP), lambda g: (g, 0, 0)), big, big, big, big, vec, vec],
        out_shape=[jax.ShapeDtypeStruct((G, LP, LP), BF16)]
        + [jax.ShapeDtypeStruct((G, LP, N), BF16)] * 4
        + [jax.ShapeDtypeStruct((G, 1, N), F32)] * 2,
        scratch_shapes=[pltpu.VMEM((LP, N), F32), pltpu.VMEM((LP, N), F32)],
        compiler_params=_cp(("parallel",)),
        name="s5_params",
    )(v3(lam_re), v3(lam_im), ldt, btr, bti, c_re, c_im)


def _s5_main_kernel(*refs, nseq, nchunk, has_state):
    if has_state:
        (u_ref, m_ref, wr_ref, wi_ref, vr_ref, vi_ref, alr_ref, ali_ref, h0r_ref, h0i_ref,
         y_out, hfr_out, hfi_out, xr_sc, xi_sc, hr_sc, hi_sc) = refs
    else:
        (u_ref, m_ref, wr_ref, wi_ref, vr_ref, vi_ref, alr_ref, ali_ref,
         y_out, hfr_out, hfi_out, xr_sc, xi_sc, hr_sc, hi_sc) = refs
    ub = u_ref[...]
    xr_sc[...] = _dot(ub, wr_ref[...])
    xi_sc[...] = _dot(ub, wi_ref[...])
    ar = alr_ref[...]
    ai = ali_ref[...]
    n = ar.shape[-1]
    if has_state:
        init = (h0r_ref[...], h0i_ref[...])
    else:
        init = (jnp.zeros((nseq, n), F32), jnp.zeros((nseq, n), F32))

    def body(c, carry):
        sr, si = carry
        rows = pl.ds(pl.multiple_of(c * nseq, nseq), nseq)
        hr_sc[rows, :] = sr
        hi_sc[rows, :] = si
        return (ar * sr - ai * si + xr_sc[rows, :], ar * si + ai * sr + xi_sc[rows, :])

    sr, si = lax.fori_loop(0, nchunk, body, init)
    hfr_out[...] = sr
    hfi_out[...] = si
    y_out[...] = (_dot(ub, m_ref[...])
                  + _dot_nt(hr_sc[...].astype(BF16), vr_ref[...])
                  + _dot_nt(hi_sc[...].astype(BF16), vi_ref[...]))


def _s5_main(u, params, h0, nseq, nchunk):
    m, wr, wi, vr, vi, alr, ali = params
    G, R, LP = u.shape
    N = alr.shape[-1]
    has_state = h0 is not None
    g3 = lambda g: (g, 0, 0)
    big = pl.BlockSpec((None, LP, N), g3)
    vec = pl.BlockSpec((None, 1, N), g3)
    st = pl.BlockSpec((None, nseq, N), g3)
    in_specs = [pl.BlockSpec((None, R, LP), g3), pl.BlockSpec((None, LP, LP), g3),
                big, big, big, big, vec, vec]
    args = [u, m, wr, wi, vr, vi, alr, ali]
    if has_state:
        in_specs += [st, st]
        args += list(h0)
    return pl.pallas_call(
        functools.partial(_s5_main_kernel, nseq=nseq, nchunk=nchunk, has_state=has_state),
        grid=(G,),
        in_specs=in_specs,
        out_specs=[pl.BlockSpec((None, R, LP), g3), st, st],
        out_shape=[jax.ShapeDtypeStruct((G, R, LP), F32),
                   jax.ShapeDtypeStruct((G, nseq, N), F32),
                   jax.ShapeDtypeStruct((G, nseq, N), F32)],
        scratch_shapes=[pltpu.VMEM((R, N), F32)] * 4,
        compiler_params=_cp(("parallel",)),
        name="s5_main",
    )(*args)


def _ab_out_kernel(x_ref, attn_ref, y_ref, u_ref, d_ref, wglu_ref, bglu_ref, wo_ref, o_ref):
    y = y_ref[...] + d_ref[...] * u_ref[...]
    z = y * (0.5 * (1.0 + jnp.tanh(math.sqrt(2.0 / math.pi) * (y + 0.044715 * (y * y * y)))))
    gate = 1.0 / (1.0 + jnp.exp(-(_dot(z.astype(BF16), wglu_ref[...]) + bglu_ref[...])))
    ssm = (z * gate).astype(BF16)
    na = attn_ref.shape[1]
    o_ref[...] = x_ref[...] + _dot(attn_ref[...], wo_ref[:na, :]) + _dot(ssm, wo_ref[na:, :])


def _ab_out(x, attn, y, u, d, wglu, bglu, wo, tm):
    T, D = x.shape
    row = lambda i: (i, 0)
    fix = lambda i: (0, 0)
    return pl.pallas_call(
        _ab_out_kernel,
        grid=(T // tm,),
        in_specs=[pl.BlockSpec((tm, D), row), pl.BlockSpec((tm, attn.shape[1]), row),
                  pl.BlockSpec((tm, y.shape[1]), row), pl.BlockSpec((tm, u.shape[1]), row),
                  pl.BlockSpec(d.shape, fix), pl.BlockSpec(wglu.shape, fix),
                  pl.BlockSpec(bglu.shape, fix), pl.BlockSpec(wo.shape, fix)],
        out_specs=pl.BlockSpec((tm, D), row),
        out_shape=jax.ShapeDtypeStruct((T, D), F32),
        compiler_params=_cp(("parallel",)),
        name="ab_out",
    )(x, attn, y, u, d, wglu, bglu, wo)


def _mlp_kernel(x_ref, g_ref, wup_ref, wdn_ref, gf_ref, o_ref, xn_sc, acc_sc, *, final_norm):
    f = pl.program_id(1)

    @pl.when(f == 0)
    def _():
        xn_sc[...] = _rms(x_ref[...], g_ref[...]).astype(BF16)
        acc_sc[...] = jnp.zeros_like(acc_sc)

    a = jnp.maximum(_dot(xn_sc[...], wup_ref[...]), 0.0)
    acc_sc[...] += _dot((a * a).astype(BF16), wdn_ref[...])

    @pl.when(f == pl.num_programs(1) - 1)
    def _():
        r = x_ref[...] + acc_sc[...]
        if final_norm:
            r = _rms(r, gf_ref[...])
        o_ref[...] = r


def _mlp(x, g, wup, wdn, gf, final_norm, tm, tf):
    T, D = x.shape
    FF = wup.shape[1]
    return pl.pallas_call(
        functools.partial(_mlp_kernel, final_norm=final_norm),
        grid=(T // tm, FF // tf),
        in_specs=[pl.BlockSpec((tm, D), lambda i, f: (i, 0)), pl.BlockSpec((1, D), lambda i, f: (0, 0)),
                  pl.BlockSpec((D, tf), lambda i, f: (0, f)), pl.BlockSpec((tf, D), lambda i, f: (f, 0)),
                  pl.BlockSpec((1, D), lambda i, f: (0, 0))],
        out_specs=pl.BlockSpec((tm, D), lambda i, f: (i, 0)),
        out_shape=jax.ShapeDtypeStruct((T, D), F32),
        scratch_shapes=[pltpu.VMEM((tm, D), BF16), pltpu.VMEM((tm, D), F32)],
        compiler_params=_cp(("parallel", "arbitrary")),
        name="mlp",
    )(x, g, wup, wdn, gf)


def _ret_log_decay(head):
    hf = jnp.full((1, 1), head, jnp.int32).astype(F32)
    return jnp.log(1.0 - jnp.exp2(-5.0 - hf))


def _c_in_kernel(x_ref, g_ref, w_ref, cos_ref, sin_ref, o_ref, xn_sc, *, n_rot, n_q, n_lin, dk,
                 chunk_split_row, chunk_lens):
    i = pl.program_id(0)
    j = pl.program_id(1)

    @pl.when(j == 0)
    def _():
        xn_sc[...] = _rms(x_ref[...], g_ref[...]).astype(BF16)

    tm = x_ref.shape[0]
    tn = w_ref.shape[1]
    half = dk // 2

    def chunks():
        for s in range(0, tn, dk):
            yield s, _dot(xn_sc[...], w_ref[:, s:s + dk])

    @pl.when(j < n_rot)
    def _():
        cos = cos_ref[...]
        sin = sin_ref[...]
        c_len = jnp.where(i * tm >= chunk_split_row, chunk_lens[1], chunk_lens[0])
        pos = (lax.broadcasted_iota(jnp.int32, (tm, 1), 0) & (c_len - 1)).astype(F32) + 1.0
        sign = jnp.where(j < n_q, 1.0, -1.0)
        scale = jnp.where(j < n_q, 1.0, dk ** -0.5)
        for s, acc in chunks():
            lg = _ret_log_decay(((j * tn + s) // dk) % RET_HEADS)
            dec = jnp.exp((sign * lg) * pos) * scale
            x1 = acc[:, :half]
            x2 = acc[:, half:]
            o_ref[:, s:s + half] = ((x1 * cos - x2 * sin) * dec).astype(BF16)
            o_ref[:, s + half:s + dk] = ((x1 * sin + x2 * cos) * dec).astype(BF16)

    @pl.when((j >= n_rot) & (j < n_lin))
    def _():
        for s, acc in chunks():
            o_ref[:, s:s + dk] = acc.astype(BF16)

    @pl.when(j >= n_lin)
    def _():
        for s, acc in chunks():
            o_ref[:, s:s + dk] = (acc / (1.0 + jnp.exp(-acc))).astype(BF16)


def _c_in(x, g, w, cos, sin, dk, dv, tm, tn, chunk_split_row, chunk_lens):
    T, D = x.shape
    N = w.shape[1]
    qk = RET_HEADS * dk
    n_q = qk // tn
    n_rot = 2 * qk // tn
    n_lin = (2 * qk + RET_HEADS * dv) // tn
    assert chunk_split_row % tm == 0
    assert all(tm % c == 0 and c & (c - 1) == 0 for c in chunk_lens)
    return pl.pallas_call(
        functools.partial(_c_in_kernel, n_rot=n_rot, n_q=n_q, n_lin=n_lin, dk=dk,
                          chunk_split_row=chunk_split_row, chunk_lens=chunk_lens),
        grid=(T // tm, N // tn),
        in_specs=[pl.BlockSpec((tm, D), lambda i, j: (i, 0)), pl.BlockSpec((1, D), lambda i, j: (0, 0)),
                  pl.BlockSpec((D, tn), lambda i, j: (0, j)),
                  pl.BlockSpec((tm, dk // 2), lambda i, j: (i, 0)),
                  pl.BlockSpec((tm, dk // 2), lambda i, j: (i, 0))],
        out_specs=pl.BlockSpec((tm, tn), lambda i, j: (i, j)),
        out_shape=jax.ShapeDtypeStruct((T, N), BF16),
        scratch_shapes=[pltpu.VMEM((tm, D), BF16)],
        compiler_params=_cp(("parallel", "arbitrary")),
        name="c_in",
    )(x, g, w, cos, sin)


def _ret_kernel(*refs, c_len, has_state, hpb, dk, dv):
    if has_state:
        q_ref, k_ref, v_ref, g_ref, gn_ref, s0_ref, o_ref, sf_ref, st_sc = refs
    else:
        q_ref, k_ref, v_ref, g_ref, gn_ref, o_ref, sf_ref, st_sc = refs
    hb = pl.program_id(1)
    c = pl.program_id(2)

    @pl.when(c == 0)
    def _():
        if has_state:
            st_sc[...] = s0_ref[...]
        else:
            st_sc[...] = jnp.zeros_like(st_sc)

    row = lax.broadcasted_iota(jnp.int32, (c_len, c_len), 0)
    col = lax.broadcasted_iota(jnp.int32, (c_len, c_len), 1)
    causal = row >= col
    for hh in range(hpb):
        q = q_ref[:, hh * dk:(hh + 1) * dk]
        k = k_ref[:, hh * dk:(hh + 1) * dk]
        v = v_ref[:, hh * dv:(hh + 1) * dv]
        inner = _dot(jnp.where(causal, _dot_nt(q, k), 0.0).astype(BF16), v)
        st = st_sc[hh]
        o = inner + _dot(q, st.astype(BF16))
        g_c = jnp.exp(_ret_log_decay(hb * hpb + hh) * float(c_len))
        st_sc[hh] = g_c * (st + _dot_tn(k, v))
        mu = jnp.mean(o, axis=-1, keepdims=True)
        d = o - mu
        var = jnp.mean(d * d, axis=-1, keepdims=True)
        on = d * lax.rsqrt(var + GN_EPS) * gn_ref[:, hh * dv:(hh + 1) * dv]
        o_ref[:, hh * dv:(hh + 1) * dv] = (on * g_ref[:, hh * dv:(hh + 1) * dv].astype(F32)).astype(BF16)

    @pl.when(c == pl.num_programs(2) - 1)
    def _():
        sf_ref[...] = st_sc[...]


def _retention(proj, gn, state0, o_prev, nb, s, c_len, row0, dk, dv, hpb):
    T = proj.shape[0]
    H = RET_HEADS
    nc = s // c_len
    rb0 = row0 // c_len
    nhb = H // hpb
    has_state = state0 is not None
    rowblk = lambda b, h, c: rb0 + b * nc + c
    wk, wv = hpb * dk, hpb * dv
    kb = H * dk // wk
    vb = 2 * H * dk // wv
    in_specs = [pl.BlockSpec((c_len, wk), lambda b, h, c: (rowblk(b, h, c), h)),
                pl.BlockSpec((c_len, wk), lambda b, h, c: (rowblk(b, h, c), kb + h)),
                pl.BlockSpec((c_len, wv), lambda b, h, c: (rowblk(b, h, c), vb + h)),
                pl.BlockSpec((c_len, wv), lambda b, h, c: (rowblk(b, h, c), vb + nhb + h)),
                pl.BlockSpec((1, wv), lambda b, h, c: (0, h))]
    args = [proj, proj, proj, proj, gn]
    if has_state:
        in_specs.append(pl.BlockSpec((None, hpb, dk, dv), lambda b, h, c: (b, h, 0, 0)))
        args.append(state0)
    aliases = {}
    if o_prev is not None:
        in_specs.append(pl.BlockSpec(memory_space=pl.ANY))
        args.append(o_prev)
        aliases = {len(args) - 1: 0}

    def kern(*refs):
        if o_prev is not None:
            n_in = len(args)
            refs = refs[:n_in - 1] + refs[n_in:]
        _ret_kernel(*refs, c_len=c_len, has_state=has_state, hpb=hpb, dk=dk, dv=dv)

    return pl.pallas_call(
        kern,
        grid=(nb, nhb, nc),
        in_specs=in_specs,
        out_specs=[pl.BlockSpec((c_len, wv), lambda b, h, c: (rowblk(b, h, c), h)),
                   pl.BlockSpec((None, hpb, dk, dv), lambda b, h, c: (b, h, 0, 0))],
        out_shape=[jax.ShapeDtypeStruct((T, H * dv), BF16),
                   jax.ShapeDtypeStruct((nb, H, dk, dv), F32)],
        scratch_shapes=[pltpu.VMEM((hpb, dk, dv), F32)],
        input_output_aliases=aliases,
        compiler_params=_cp(("parallel", "parallel", "arbitrary")),
        name="retention",
    )(*args)


def _c_out_kernel(x_ref, o_ref, w_ref, y_ref):
    y_ref[...] = x_ref[...] + _dot(o_ref[...], w_ref[...])


def _c_out(x, o, w, tm, tn):
    T, D = x.shape
    K = o.shape[1]
    return pl.pallas_call(
        _c_out_kernel,
        grid=(T // tm, D // tn),
        in_specs=[pl.BlockSpec((tm, tn), lambda i, j: (i, j)), pl.BlockSpec((tm, K), lambda i, j: (i, 0)),
                  pl.BlockSpec((K, tn), lambda i, j: (0, j))],
        out_specs=pl.BlockSpec((tm, tn), lambda i, j: (i, j)),
        out_shape=jax.ShapeDtypeStruct((T, D), F32),
        compiler_params=_cp(("parallel", "arbitrary")),
        name="c_out",
    )(x, o, w)


def _rope_tables(pos, half):
    inv = ROPE_THETA ** (-jnp.arange(half, dtype=F32) / half)
    ang = pos.astype(F32)[:, None] * inv[None, :]
    return jnp.cos(ang), jnp.sin(ang)


def _swap_halves(w):
    half = w.shape[-1] // 2
    return jnp.concatenate([w[..., half:], w[..., :half]], axis=-1)


def _prep_w_in_ab(w):
    o1 = Q_LORA + KV_LORA
    o2 = o1 + MLA_ROPE
    kr = w[:, o1:o2]
    ks = _swap_halves(kr)
    return jnp.concatenate([w[:, :o1], kr, ks, ks, kr, w[:, o2:]], axis=1).astype(BF16)


def _prep_w_q_b(w):
    w = w.reshape(Q_LORA, MLA_HEADS, MLA_NOPE + MLA_ROPE)
    rope = w[..., MLA_NOPE:]
    swap = _swap_halves(rope)
    out = jnp.concatenate([w[..., :MLA_NOPE], rope, swap, swap, rope], axis=-1)
    return out.reshape(Q_LORA, -1).astype(BF16)


def _prep_w_kv_b(w):
    w = w.reshape(KV_LORA, MLA_HEADS, MLA_NOPE + MLA_V)
    return jnp.concatenate([w[..., :MLA_NOPE].reshape(KV_LORA, -1),
                            w[..., MLA_NOPE:].reshape(KV_LORA, -1)], axis=1).astype(BF16)


def kernel(x_prompt, x_sample, cache_mla_ckv, cache_mla_krope, state_s5_re, state_s5_im, state_ret,
           norm_mix, norm_mlp, norm_final, w_in_ab, q_a_norm, kv_a_norm, w_q_b, w_kv_b,
           s5_lam_re, s5_lam_im, s5_log_dt, s5_b_re, s5_b_im, s5_c_re, s5_c_im, s5_d, w_glu, b_glu,
           w_out_ab, w_in_c, ret_gn, w_out_c, w_up, w_down):
    BP, SP, D = x_prompt.shape
    BS, SS, _ = x_sample.shape
    past = cache_mla_ckv.shape[2]
    depth = norm_mix.shape[0]
    G, N = s5_lam_re.shape[1:]
    P, L = S5_GROUP, S5_L
    dk, dv = state_ret.shape[3:]
    TP, TS = BP * SP, BS * SS
    T = TP + TS
    TM = 512

    x = jnp.concatenate([x_prompt.reshape(TP, D), x_sample.reshape(TS, D)], axis=0)
    pos = jnp.concatenate([jnp.tile(jnp.arange(SP), BP), jnp.tile(past + jnp.arange(SS), BS)])

    c32, s32 = _rope_tables(pos, MLA_ROPE // 2)
    z64 = jnp.zeros((T, LANE - MLA_ROPE), F32)
    cs1 = jnp.concatenate([c32, c32, z64], axis=1)
    cs2 = jnp.concatenate([-s32, s32, z64], axis=1)
    cos_r, sin_r = _rope_tables(pos, dk // 2)

    skp = -(-(past + SS) // LANE) * LANE
    ckv_p, kr_p, s5r_p, s5i_p, ret_p = [], [], [], [], []
    ckv_s, kr_s, s5r_s, s5i_s, ret_s = [], [], [], [], []

    for layer in range(depth):
        i = layer // 2
        g_mix = norm_mix[layer].reshape(1, D)
        if layer % 2 == 0:
            q, ckv, kr, u = _ab_in(x, g_mix, _prep_w_in_ab(w_in_ab[i]), q_a_norm[i].reshape(1, -1),
                                   kv_a_norm[i].reshape(1, -1), _prep_w_q_b(w_q_b[i]), cs1, cs2, 256)
            ckv_p.append(ckv[:TP].reshape(BP, SP, KV_LORA))
            kr_p.append(kr[:TP, :MLA_ROPE].reshape(BP, SP, MLA_ROPE))
            ckv_s.append(ckv[TP:].reshape(BS, SS, KV_LORA))
            kr_s.append(kr[TP:, :MLA_ROPE].reshape(BS, SS, MLA_ROPE))
            pad = skp - past - SS
            ckv_hist = jnp.concatenate([cache_mla_ckv[i], ckv[TP:].reshape(BS, SS, KV_LORA),
                                        jnp.zeros((BS, pad, KV_LORA), F32)], axis=1)
            kr_past = jnp.pad(cache_mla_krope[i], ((0, 0), (0, 0), (0, LANE - MLA_ROPE)))
            kr_hist = jnp.concatenate([kr_past, kr[TP:].reshape(BS, SS, LANE),
                                       jnp.zeros((BS, pad, LANE), F32)], axis=1)
            w_kv = _prep_w_kv_b(w_kv_b[i])
            k_p, v_p = _kv_proj(ckv, kr, w_kv, 512)
            k_s, v_s = _kv_proj(ckv_hist.reshape(BS * skp, KV_LORA), kr_hist.reshape(BS * skp, LANE),
                                w_kv, skp)
            attn = jnp.concatenate([
                _attn_long(q, k_p, v_p, BP, SP, ATT_TQ, ATT_TK),
                _attn_short(q, k_s, v_s, BS, SS, skp, TP, 0, past, past + SS)], axis=0)
            params = _s5_params(s5_lam_re[i], s5_lam_im[i], s5_log_dt[i], s5_b_re[i], s5_b_im[i],
                                s5_c_re[i], s5_c_im[i])
            ncp, ncs = SP // L, SS // L
            ub = u.astype(BF16)
            u_p = ub[:TP].reshape(BP, ncp, L, G, P).transpose(3, 1, 0, 2, 4).reshape(G, ncp * BP, L * P)
            u_s = ub[TP:].reshape(BS, ncs, L, G, P).transpose(3, 1, 0, 2, 4).reshape(G, ncs * BS, L * P)
            y_p, hr_p, hi_p = _s5_main(u_p, params, None, BP, ncp)
            h0 = (jnp.swapaxes(state_s5_re[i], 0, 1), jnp.swapaxes(state_s5_im[i], 0, 1))
            y_s, hr_s, hi_s = _s5_main(u_s, params, h0, BS, ncs)
            y_p = y_p.reshape(G, ncp, BP, L, P).transpose(2, 1, 3, 0, 4).reshape(TP, G * P)
            y_s = y_s.reshape(G, ncs, BS, L, P).transpose(2, 1, 3, 0, 4).reshape(TS, G * P)
            y = jnp.concatenate([y_p, y_s], axis=0)
            s5r_p.append(jnp.swapaxes(hr_p, 0, 1))
            s5i_p.append(jnp.swapaxes(hi_p, 0, 1))
            s5r_s.append(jnp.swapaxes(hr_s, 0, 1))
            s5i_s.append(jnp.swapaxes(hi_s, 0, 1))
            x = _ab_out(x, attn, y, u, s5_d[i].reshape(1, G * P), w_glu[i].astype(BF16),
                        b_glu[i].reshape(1, -1), w_out_ab[i].astype(BF16), 256)
        else:
            proj = _c_in(x, g_mix, w_in_c[i].astype(BF16), cos_r, sin_r, dk, dv, TM, 1024, TP, (RET_C, SS))
            gn = ret_gn[i].reshape(1, -1)
            o, st_p = _retention(proj, gn, None, None, BP, SP, RET_C, 0, dk, dv, RET_HPB)
            o, st_s = _retention(proj, gn, state_ret[i], o, BS, SS, SS, TP, dk, dv, RET_HPB)
            ret_p.append(st_p)
            ret_s.append(st_s)
            x = _c_out(x, o, w_out_c[i].astype(BF16), TM, 512)
        x = _mlp(x, norm_mlp[layer].reshape(1, D), w_up[layer].astype(BF16), w_down[layer].astype(BF16),
                 norm_final.reshape(1, D), layer == depth - 1, TM, 512)

    return (x[:TP].reshape(BP, SP, D), x[TP:].reshape(BS, SS, D),
            jnp.stack(ckv_p), jnp.stack(kr_p), jnp.stack(s5r_p), jnp.stack(s5i_p), jnp.stack(ret_p),
            jnp.stack(ckv_s), jnp.stack(kr_s), jnp.stack(s5r_s), jnp.stack(s5i_s), jnp.stack(ret_s))
```

```python
import functools
import math

import jax
import jax.numpy as jnp
import numpy as np
from jax import lax
from jax.experimental import pallas as pl
from jax.experimental.pallas import tpu as pltpu

F32 = jnp.float32
BF16 = jnp.bfloat16

CHUNK = 64
MLA_V = 128
MLA_NOPE = 128
MLA_ROPE = 64
MLA_HEADS = 8
Q_LORA = 512
KV_LORA = 256
MLA_SCALE = (MLA_NOPE + MLA_ROPE) ** -0.5
S5_GROUP = 16
S5_STATE = 64
RET_HEADS = 8
ROPE_THETA = 10000.0
EPS = 1e-6
GN_EPS = 1e-5
LOG2E = 1.4426950408889634

LANE = 128
QK_PAD = 256
S5_L = 64
RET_C = 256
RET_HPB = 4
ATT_TQ = 512
MLP_TF = 1024
ATT_TK = 512
NEG_BIG = -1e30
VMEM_LIMIT = 56 * 2 ** 20


def _cp(sem):
    return pltpu.CompilerParams(dimension_semantics=sem, vmem_limit_bytes=VMEM_LIMIT)


def _rms(x, g):
    return x * lax.rsqrt(jnp.mean(x * x, axis=-1, keepdims=True) + EPS) * g


def _dot(a, b):
    return jnp.dot(a, b, preferred_element_type=F32)


def _dot_nt(a, b):
    return lax.dot_general(a, b, (((1,), (1,)), ((), ())), preferred_element_type=F32)


def _dot_tn(a, b):
    return lax.dot_general(a, b, (((0,), (0,)), ((), ())), preferred_element_type=F32)


def _parts_specs(rows, tm, width, row_axis=0):
    specs, starts, start = [], [], 0
    for r in rows:
        nblk = r // tm

        def index_map(*g, start=start, nblk=nblk):
            return (jnp.clip(g[row_axis] - start, 0, nblk - 1), 0)

        specs.append(pl.BlockSpec((tm, width), index_map))
        starts.append(start)
        start += nblk
    return specs, tuple(starts)


def _parts_load(refs, starts, i):
    x = refs[0][...]
    for ref, s in zip(refs[1:], starts[1:]):
        x = jnp.where(i >= s, ref[...], x)
    return x


def _parts_store(refs, starts, i, val):
    ends = starts[1:] + (None,)
    for ref, s, e in zip(refs, starts, ends):
        cond = i >= s if e is None else (i >= s) & (i < e)

        @pl.when(cond)
        def _(ref=ref):
            ref[...] = val


def _ab_in_kernel(*refs, x_starts):
    n = len(x_starts)
    x_refs = refs[:n]
    (g_ref, win_ref, qn_ref, kvn_ref, wq_ref, cs1_ref, cs2_ref,
     q_out, ckv_out, kr_out, u_out) = refs[n:]
    x = _parts_load(x_refs, x_starts, pl.program_id(0))
    xn = _rms(x, g_ref[...]).astype(BF16)
    proj = _dot(xn, win_ref[...])
    cqn = _rms(proj[:, :Q_LORA], qn_ref[...]).astype(BF16)
    o = Q_LORA
    ckv_out[...] = _rms(proj[:, o:o + KV_LORA], kvn_ref[...])
    o += KV_LORA
    cs1 = cs1_ref[...]
    cs2 = cs2_ref[...]
    kr_out[...] = proj[:, o:o + LANE] * cs1 + proj[:, o + LANE:o + 2 * LANE] * cs2
    o += 2 * LANE
    u_out[...] = proj[:, o:]
    q = _dot(cqn, wq_ref[...])
    qs = MLA_SCALE * LOG2E
    for h in range(MLA_HEADS):
        b = h * 3 * LANE
        q_out[:, h * QK_PAD:h * QK_PAD + LANE] = (q[:, b:b + LANE] * qs).astype(BF16)
        rope = q[:, b + LANE:b + 2 * LANE] * cs1 + q[:, b + 2 * LANE:b + 3 * LANE] * cs2
        q_out[:, h * QK_PAD + LANE:(h + 1) * QK_PAD] = (rope * qs).astype(BF16)


def _ab_in(xs, g, win, qn, kvn, wq, cs1, cs2, tm):
    D = xs[0].shape[1]
    T = sum(x.shape[0] for x in xs)
    row = lambda i: (i, 0)
    fix = lambda i: (0, 0)
    s5w = win.shape[1] - Q_LORA - KV_LORA - 2 * LANE
    x_specs, x_starts = _parts_specs([x.shape[0] for x in xs], tm, D)
    return pl.pallas_call(
        functools.partial(_ab_in_kernel, x_starts=x_starts),
        grid=(T // tm,),
        in_specs=x_specs + [pl.BlockSpec((1, D), fix),
                  pl.BlockSpec(win.shape, fix), pl.BlockSpec((1, Q_LORA), fix),
                  pl.BlockSpec((1, KV_LORA), fix), pl.BlockSpec(wq.shape, fix),
                  pl.BlockSpec((tm, LANE), row), pl.BlockSpec((tm, LANE), row)],
        out_specs=[pl.BlockSpec((tm, MLA_HEADS * QK_PAD), row), pl.BlockSpec((tm, KV_LORA), row),
                   pl.BlockSpec((tm, LANE), row), pl.BlockSpec((tm, s5w), row)],
        out_shape=[jax.ShapeDtypeStruct((T, MLA_HEADS * QK_PAD), BF16),
                   jax.ShapeDtypeStruct((T, KV_LORA), F32),
                   jax.ShapeDtypeStruct((T, LANE), F32),
                   jax.ShapeDtypeStruct((T, s5w), F32)],
        compiler_params=_cp(("parallel",)),
        name="ab_in",
    )(*xs, g, win, qn, kvn, wq, cs1, cs2)


def _kv_kernel(ckv_ref, kr_ref, w_ref, k_out, v_out):
    kv = _dot(ckv_ref[...].astype(BF16), w_ref[...])
    kr = kr_ref[...].astype(BF16)
    for h in range(MLA_HEADS):
        k_out[:, h * QK_PAD:h * QK_PAD + LANE] = kv[:, h * LANE:(h + 1) * LANE].astype(BF16)
        k_out[:, h * QK_PAD + LANE:(h + 1) * QK_PAD] = kr
    v_out[...] = kv[:, MLA_HEADS * MLA_NOPE:].astype(BF16)


def _kv_proj(ckv, kr, w, tm):
    R = ckv.shape[0]
    row = lambda i: (i, 0)
    return pl.pallas_call(
        _kv_kernel,
        grid=(R // tm,),
        in_specs=[pl.BlockSpec((tm, KV_LORA), row), pl.BlockSpec((tm, LANE), row),
                  pl.BlockSpec(w.shape, lambda i: (0, 0))],
        out_specs=[pl.BlockSpec((tm, MLA_HEADS * QK_PAD), row),
                   pl.BlockSpec((tm, MLA_HEADS * MLA_V), row)],
        out_shape=[jax.ShapeDtypeStruct((R, MLA_HEADS * QK_PAD), BF16),
                   jax.ShapeDtypeStruct((R, MLA_HEADS * MLA_V), BF16)],
        compiler_params=_cp(("parallel",)),
        name="kv_proj",
    )(ckv, kr, w)


def _attn_long_kernel(q_ref, k_ref, v_ref, o_ref, s_sc, m_sc, l_sc, acc_sc, *, tq, tk):
    qi = pl.program_id(2)
    n_full = (qi * tq) // tk

    def scores_to(slot, kt):
        s_sc[slot] = _dot_nt(q_ref[...], k_ref[pl.ds(pl.multiple_of(kt * tk, tk), tk), :])

    def consume(slot, kt, masked):
        s = s_sc[slot]
        if masked:
            qpos = qi * tq + lax.broadcasted_iota(jnp.int32, (tq, tk), 0)
            kpos = kt * tk + lax.broadcasted_iota(jnp.int32, (tq, tk), 1)
            s = jnp.where(kpos // CHUNK <= qpos // CHUNK, s, NEG_BIG)
        m = m_sc[...]
        m_new = jnp.maximum(m, jnp.max(s, axis=1, keepdims=True))
        alpha = jnp.exp2(m - m_new)
        p = jnp.exp2(s - jnp.concatenate([m_new] * (tk // LANE), axis=1))
        m_sc[...] = m_new
        l_sc[...] = alpha * l_sc[...] + jnp.sum(p, axis=1, keepdims=True)
        v = v_ref[pl.ds(pl.multiple_of(kt * tk, tk), tk), :]
        acc_sc[...] = alpha * acc_sc[...] + _dot(p.astype(BF16), v)

    m_sc[...] = jnp.full_like(m_sc, NEG_BIG)
    l_sc[...] = jnp.zeros_like(l_sc)
    acc_sc[...] = jnp.zeros_like(acc_sc)
    scores_to(0, 0)

    def pair(j, carry):
        scores_to(1, 2 * j + 1)
        consume(0, 2 * j, False)
        scores_to(0, 2 * j + 2)
        consume(1, 2 * j + 1, False)
        return carry

    lax.fori_loop(0, n_full // 2, pair, 0)

    @pl.when(n_full % 2 == 1)
    def _():
        scores_to(1, n_full)
        consume(0, n_full - 1, False)
        consume(1, n_full, True)

    @pl.when(n_full % 2 == 0)
    def _():
        consume(0, n_full, True)

    o_ref[...] = (acc_sc[...] / l_sc[...]).astype(BF16)


def _attn_long(q, k, v, nb, s, tq, tk):
    assert tk in (tq, 2 * tq) and s % tk == 0
    nq = s // tq
    return pl.pallas_call(
        functools.partial(_attn_long_kernel, tq=tq, tk=tk),
        grid=(nb, MLA_HEADS, nq),
        in_specs=[pl.BlockSpec((tq, QK_PAD), lambda b, h, i: (b * nq + i, h)),
                  pl.BlockSpec((s, QK_PAD), lambda b, h, i: (b, h)),
                  pl.BlockSpec((s, MLA_V), lambda b, h, i: (b, h))],
        out_specs=pl.BlockSpec((tq, MLA_V), lambda b, h, i: (b * nq + i, h)),
        out_shape=jax.ShapeDtypeStruct((nb * s, MLA_HEADS * MLA_V), BF16),
        scratch_shapes=[pltpu.VMEM((2, tq, tk), F32), pltpu.VMEM((tq, LANE), F32),
                        pltpu.VMEM((tq, LANE), F32), pltpu.VMEM((tq, MLA_V), F32)],
        compiler_params=_cp(("parallel", "parallel", "arbitrary")),
        name="attn_long",
    )(q, k, v)


def _attn_short_kernel(q_ref, k_ref, v_ref, o_ref, *, q_pos0, n_keys):
    sq = q_ref.shape[0]
    skp = k_ref.shape[0]
    s = _dot_nt(q_ref[...], k_ref[...])
    qpos = q_pos0 + lax.broadcasted_iota(jnp.int32, (sq, skp), 0)
    kpos = lax.broadcasted_iota(jnp.int32, (sq, skp), 1)
    ok = (kpos // CHUNK <= qpos // CHUNK) & (kpos < n_keys)
    s = jnp.where(ok, s, NEG_BIG)
    m = jnp.max(s, axis=1, keepdims=True)
    p = jnp.exp2(s - m)
    l = jnp.sum(p, axis=1, keepdims=True)
    o_ref[...] = (_dot(p.astype(BF16), v_ref[...]) / l).astype(BF16)


def _attn_short(q, k, v, nb, sq, skp, q_row0, k_row0, q_pos0, n_keys):
    qb0 = q_row0 // sq
    kb0 = k_row0 // skp
    return pl.pallas_call(
        functools.partial(_attn_short_kernel, q_pos0=q_pos0, n_keys=n_keys),
        grid=(nb, MLA_HEADS),
        in_specs=[pl.BlockSpec((sq, QK_PAD), lambda b, h: (qb0 + b, h)),
                  pl.BlockSpec((skp, QK_PAD), lambda b, h: (kb0 + b, h)),
                  pl.BlockSpec((skp, MLA_V), lambda b, h: (kb0 + b, h))],
        out_specs=pl.BlockSpec((sq, MLA_V), lambda b, h: (b, h)),
        out_shape=jax.ShapeDtypeStruct((nb * sq, MLA_HEADS * MLA_V), BF16),
        compiler_params=_cp(("parallel", "parallel")),
        name="attn_short",
    )(q, k, v)


def _s5_param_kernel(lr_ref, li_ref, ldt_ref, btr_ref, bti_ref, cr_ref, ci_ref,
                     m_out, wr_out, wi_out, vr_out, vi_out, alr_out, ali_out,
                     car_sc, cai_sc):
    L, P = S5_L, S5_GROUP
    lr = jnp.minimum(lr_ref[...], -1e-4)
    li = li_ref[...]
    dt = jnp.exp(ldt_ref[...])
    mag = jnp.exp(lr * dt)
    a_re = mag * jnp.cos(li * dt)
    a_im = mag * jnp.sin(li * dt)
    den = lr * lr + li * li
    f_re = ((a_re - 1.0) * lr + a_im * li) / den
    f_im = (a_im * lr - (a_re - 1.0) * li) / den
    btr = btr_ref[...]
    bti = bti_ref[...]
    bb_re = f_re * btr - f_im * bti
    bb_im = f_re * bti + f_im * btr
    cr = cr_ref[...]
    ci = ci_ref[...]

    n_pow = -(-(L + 1) // 8) * 8
    t = lax.broadcasted_iota(jnp.int32, (n_pow, 1), 0).astype(F32)
    pmag = jnp.exp((lr * dt) * t)
    pang = (li * dt) * t
    pw_re = pmag * jnp.cos(pang)
    pw_im = pmag * jnp.sin(pang)

    def apow(t):
        return pw_re[t:t + 1, :], pw_im[t:t + 1, :]

    for i in range(L):
        rows = slice(i * P, (i + 1) * P)
        pr, pi = apow(L - 1 - i)
        wr_out[rows, :] = (pr * bb_re - pi * bb_im).astype(BF16)
        wi_out[rows, :] = (pr * bb_im + pi * bb_re).astype(BF16)
        pr, pi = apow(i + 1)
        vr_out[rows, :] = (cr * pr - ci * pi).astype(BF16)
        vi_out[rows, :] = (-(cr * pi + ci * pr)).astype(BF16)
        pr, pi = apow(i)
        car_sc[rows, :] = cr * pr - ci * pi
        cai_sc[rows, :] = cr * pi + ci * pr
    pr, pi = apow(L)
    alr_out[...] = pr
    ali_out[...] = pi
    hp = lax.Precision.HIGHEST
    nt = (((1,), (1,)), ((), ()))
    krow = (lax.dot_general(bb_re, car_sc[...], nt, precision=hp, preferred_element_type=F32)
            - lax.dot_general(bb_im, cai_sc[...], nt, precision=hp, preferred_element_type=F32))
    lane = lax.broadcasted_iota(jnp.int32, (P, L * P), 1)
    for i in range(L):
        shifted = krow if i == 0 else pltpu.roll(krow, i * P, axis=1)
        m_out[i * P:(i + 1) * P, :] = jnp.where(lane >= i * P, shifted, 0.0).astype(BF16)


def _s5_params(lam_re, lam_im, log_dt, b_re, b_im, c_re, c_im):
    G, N = lam_re.shape
    P, L = S5_GROUP, S5_L
    LP = L * P
    v3 = lambda a: a.reshape(G, 1, N)
    ldt = jnp.broadcast_to(log_dt[:, None, None], (G, 1, N))
    btr = jnp.swapaxes(b_re, 1, 2)
    bti = jnp.swapaxes(b_im, 1, 2)
    vec = pl.BlockSpec((None, 1, N), lambda g: (g, 0, 0))
    mat = pl.BlockSpec((None, P, N), lambda g: (g, 0, 0))
    big = pl.BlockSpec((None, LP, N), lambda g: (g, 0, 0))
    return pl.pallas_call(
        _s5_param_kernel,
        grid=(G,),
        in_specs=[vec, vec, vec, mat, mat, mat, mat],
        out_specs=[pl.BlockSpec((None, LP, LP), lambda g: (g, 0, 0)), big, big, big, big, vec, vec],
        out_shape=[jax.ShapeDtypeStruct((G, LP, LP), BF16)]
        + [jax.ShapeDtypeStruct((G, LP, N), BF16)] * 4
        + [jax.ShapeDtypeStruct((G, 1, N), F32)] * 2,
        scratch_shapes=[pltpu.VMEM((LP, N), F32), pltpu.VMEM((LP, N), F32)],
        compiler_params=_cp(("parallel",)),
        name="s5_params",
    )(v3(lam_re), v3(lam_im), ldt, btr, bti, c_re, c_im)


def _s5_main_kernel(*refs, nseq, nchunk, has_state):
    if has_state:
        (u_ref, m_ref, wr_ref, wi_ref, vr_ref, vi_ref, alr_ref, ali_ref, h0r_ref, h0i_ref,
         y_out, hfr_out, hfi_out, xr_sc, xi_sc, hr_sc, hi_sc) = refs
    else:
        (u_ref, m_ref, wr_ref, wi_ref, vr_ref, vi_ref, alr_ref, ali_ref,
         y_out, hfr_out, hfi_out, xr_sc, xi_sc, hr_sc, hi_sc) = refs
    ub = u_ref[...]
    xr_sc[...] = _dot(ub, wr_ref[...])
    xi_sc[...] = _dot(ub, wi_ref[...])
    ar = alr_ref[...]
    ai = ali_ref[...]
    n = ar.shape[-1]
    if has_state:
        init = (h0r_ref[...], h0i_ref[...])
    else:
        init = (jnp.zeros((nseq, n), F32), jnp.zeros((nseq, n), F32))

    def body(c, carry):
        sr, si = carry
        rows = pl.ds(pl.multiple_of(c * nseq, nseq), nseq)
        hr_sc[rows, :] = sr
        hi_sc[rows, :] = si
        return (ar * sr - ai * si + xr_sc[rows, :], ar * si + ai * sr + xi_sc[rows, :])

    sr, si = lax.fori_loop(0, nchunk, body, init)
    hfr_out[...] = sr
    hfi_out[...] = si
    y_out[...] = (_dot(ub, m_ref[...])
                  + _dot_nt(hr_sc[...].astype(BF16), vr_ref[...])
                  + _dot_nt(hi_sc[...].astype(BF16), vi_ref[...])).astype(y_out.dtype)


def _s5_main(u, params, h0, nseq, nchunk):
    m, wr, wi, vr, vi, alr, ali = params
    G, R, LP = u.shape
    N = alr.shape[-1]
    has_state = h0 is not None
    g3 = lambda g: (g, 0, 0)
    big = pl.BlockSpec((None, LP, N), g3)
    vec = pl.BlockSpec((None, 1, N), g3)
    st = pl.BlockSpec((None, nseq, N), g3)
    in_specs = [pl.BlockSpec((None, R, LP), g3), pl.BlockSpec((None, LP, LP), g3),
                big, big, big, big, vec, vec]
    args = [u, m, wr, wi, vr, vi, alr, ali]
    if has_state:
        in_specs += [st, st]
        args += list(h0)
    return pl.pallas_call(
        functools.partial(_s5_main_kernel, nseq=nseq, nchunk=nchunk, has_state=has_state),
        grid=(G,),
        in_specs=in_specs,
        out_specs=[pl.BlockSpec((None, R, LP), g3), st, st],
        out_shape=[jax.ShapeDtypeStruct((G, R, LP), BF16),
                   jax.ShapeDtypeStruct((G, nseq, N), F32),
                   jax.ShapeDtypeStruct((G, nseq, N), F32)],
        scratch_shapes=[pltpu.VMEM((R, N), F32)] * 4,
        compiler_params=_cp(("parallel",)),
        name="s5_main",
    )(*args)


def _ab_out_kernel(*refs, x_starts):
    n = len(x_starts)
    x_refs = refs[:n]
    attn_ref, y_ref, u_ref, d_ref, wglu_ref, bglu_ref, wo_ref, o_ref = refs[n:]
    x = _parts_load(x_refs, x_starts, pl.program_id(0))
    y = y_ref[...].astype(F32) + d_ref[...] * u_ref[...]
    z = y * (0.5 * (1.0 + jnp.tanh(math.sqrt(2.0 / math.pi) * (y + 0.044715 * (y * y * y)))))
    gate = 1.0 / (1.0 + jnp.exp(-(_dot(z.astype(BF16), wglu_ref[...]) + bglu_ref[...])))
    ssm = (z * gate).astype(BF16)
    na = attn_ref.shape[1]
    o_ref[...] = x + _dot(attn_ref[...], wo_ref[:na, :]) + _dot(ssm, wo_ref[na:, :])


def _ab_out(xs, attn, y, u, d, wglu, bglu, wo, tm):
    D = xs[0].shape[1]
    T = sum(x.shape[0] for x in xs)
    row = lambda i: (i, 0)
    fix = lambda i: (0, 0)
    x_specs, x_starts = _parts_specs([x.shape[0] for x in xs], tm, D)
    return pl.pallas_call(
        functools.partial(_ab_out_kernel, x_starts=x_starts),
        grid=(T // tm,),
        in_specs=x_specs + [pl.BlockSpec((tm, attn.shape[1]), row),
                  pl.BlockSpec((tm, y.shape[1]), row), pl.BlockSpec((tm, u.shape[1]), row),
                  pl.BlockSpec(d.shape, fix), pl.BlockSpec(wglu.shape, fix),
                  pl.BlockSpec(bglu.shape, fix), pl.BlockSpec(wo.shape, fix)],
        out_specs=pl.BlockSpec((tm, D), row),
        out_shape=jax.ShapeDtypeStruct((T, D), F32),
        compiler_params=_cp(("parallel",)),
        name="ab_out",
    )(*xs, attn, y, u, d, wglu, bglu, wo)


def _mlp_kernel(x_ref, g_ref, wup_ref, wdn_ref, gf_ref, *rest, final_norm, o_starts):
    o_refs = rest[:len(o_starts)]
    xn_sc, acc_sc = rest[len(o_starts):]
    f = pl.program_id(1)

    @pl.when(f == 0)
    def _():
        xn_sc[...] = _rms(x_ref[...], g_ref[...]).astype(BF16)
        acc_sc[...] = jnp.zeros_like(acc_sc)

    a = jnp.maximum(_dot(xn_sc[...], wup_ref[...]), 0.0)
    acc_sc[...] += _dot((a * a).astype(BF16), wdn_ref[...])

    @pl.when(f == pl.num_programs(1) - 1)
    def _():
        r = x_ref[...] + acc_sc[...]
        if final_norm:
            r = _rms(r, gf_ref[...])
        _parts_store(o_refs, o_starts, pl.program_id(0), r)


def _mlp(x, g, wup, wdn, gf, final_norm, out_rows, tm, tf):
    T, D = x.shape
    FF = wup.shape[1]
    assert sum(out_rows) == T
    o_specs, o_starts = _parts_specs(out_rows, tm, D)
    return pl.pallas_call(
        functools.partial(_mlp_kernel, final_norm=final_norm, o_starts=o_starts),
        grid=(T // tm, FF // tf),
        in_specs=[pl.BlockSpec((tm, D), lambda i, f: (i, 0)), pl.BlockSpec((1, D), lambda i, f: (0, 0)),
                  pl.BlockSpec((D, tf), lambda i, f: (0, f)), pl.BlockSpec((tf, D), lambda i, f: (f, 0)),
                  pl.BlockSpec((1, D), lambda i, f: (0, 0))],
        out_specs=o_specs,
        out_shape=[jax.ShapeDtypeStruct((r, D), F32) for r in out_rows],
        scratch_shapes=[pltpu.VMEM((tm, D), BF16), pltpu.VMEM((tm, D), F32)],
        compiler_params=_cp(("parallel", "arbitrary")),
        name="mlp",
    )(x, g, wup, wdn, gf)


def _ret_log_decay(head):
    hf = jnp.full((1, 1), head, jnp.int32).astype(F32)
    return jnp.log(1.0 - jnp.exp2(-5.0 - hf))


def _c_in_kernel(x_ref, g_ref, w_ref, cos_ref, sin_ref, o_ref, xn_sc, *, n_rot, n_q, n_lin, dk,
                 chunk_split_row, chunk_lens):
    i = pl.program_id(0)
    j = pl.program_id(1)

    @pl.when(j == 0)
    def _():
        xn_sc[...] = _rms(x_ref[...], g_ref[...]).astype(BF16)

    tm = x_ref.shape[0]
    tn = w_ref.shape[1]
    half = dk // 2

    def chunks():
        for s in range(0, tn, dk):
            yield s, _dot(xn_sc[...], w_ref[:, s:s + dk])

    @pl.when(j < n_rot)
    def _():
        cos = cos_ref[...]
        sin = sin_ref[...]
        c_len = jnp.where(i * tm >= chunk_split_row, chunk_lens[1], chunk_lens[0])
        pos = (lax.broadcasted_iota(jnp.int32, (tm, 1), 0) & (c_len - 1)).astype(F32) + 1.0
        sign = jnp.where(j < n_q, 1.0, -1.0)
        scale = jnp.where(j < n_q, 1.0, dk ** -0.5)
        for s, acc in chunks():
            lg = _ret_log_decay(((j * tn + s) // dk) % RET_HEADS)
            dec = jnp.exp((sign * lg) * pos) * scale
            x1 = acc[:, :half]
            x2 = acc[:, half:]
            o_ref[:, s:s + half] = ((x1 * cos - x2 * sin) * dec).astype(BF16)
            o_ref[:, s + half:s + dk] = ((x1 * sin + x2 * cos) * dec).astype(BF16)

    @pl.when((j >= n_rot) & (j < n_lin))
    def _():
        for s, acc in chunks():
            o_ref[:, s:s + dk] = acc.astype(BF16)

    @pl.when(j >= n_lin)
    def _():
        for s, acc in chunks():
            o_ref[:, s:s + dk] = (acc / (1.0 + jnp.exp(-acc))).astype(BF16)


def _c_in(x, g, w, cos, sin, dk, dv, tm, tn, chunk_split_row, chunk_lens):
    T, D = x.shape
    N = w.shape[1]
    qk = RET_HEADS * dk
    n_q = qk // tn
    n_rot = 2 * qk // tn
    n_lin = (2 * qk + RET_HEADS * dv) // tn
    assert chunk_split_row % tm == 0
    assert all(tm % c == 0 and c & (c - 1) == 0 for c in chunk_lens)
    return pl.pallas_call(
        functools.partial(_c_in_kernel, n_rot=n_rot, n_q=n_q, n_lin=n_lin, dk=dk,
                          chunk_split_row=chunk_split_row, chunk_lens=chunk_lens),
        grid=(T // tm, N // tn),
        in_specs=[pl.BlockSpec((tm, D), lambda i, j: (i, 0)), pl.BlockSpec((1, D), lambda i, j: (0, 0)),
                  pl.BlockSpec((D, tn), lambda i, j: (0, j)),
                  pl.BlockSpec((tm, dk // 2), lambda i, j: (i, 0)),
                  pl.BlockSpec((tm, dk // 2), lambda i, j: (i, 0))],
        out_specs=pl.BlockSpec((tm, tn), lambda i, j: (i, j)),
        out_shape=jax.ShapeDtypeStruct((T, N), BF16),
        scratch_shapes=[pltpu.VMEM((tm, D), BF16)],
        compiler_params=_cp(("parallel", "arbitrary")),
        name="c_in",
    )(x, g, w, cos, sin)


def _ret_kernel(*refs, c_len, has_state, hpb, dk, dv):
    if has_state:
        q_ref, k_ref, v_ref, g_ref, gn_ref, s0_ref, o_ref, sf_ref, st_sc = refs
    else:
        q_ref, k_ref, v_ref, g_ref, gn_ref, o_ref, sf_ref, st_sc = refs
    hb = pl.program_id(1)
    c = pl.program_id(2)

    @pl.when(c == 0)
    def _():
        if has_state:
            st_sc[...] = s0_ref[...]
        else:
            st_sc[...] = jnp.zeros_like(st_sc)

    row = lax.broadcasted_iota(jnp.int32, (c_len, c_len), 0)
    col = lax.broadcasted_iota(jnp.int32, (c_len, c_len), 1)
    causal = row >= col
    for hh in range(hpb):
        q = q_ref[:, hh * dk:(hh + 1) * dk]
        k = k_ref[:, hh * dk:(hh + 1) * dk]
        v = v_ref[:, hh * dv:(hh + 1) * dv]
        inner = _dot(jnp.where(causal, _dot_nt(q, k), 0.0).astype(BF16), v)
        st = st_sc[hh]
        o = inner + _dot(q, st.astype(BF16))
        g_c = jnp.exp(_ret_log_decay(hb * hpb + hh) * float(c_len))
        st_sc[hh] = g_c * (st + _dot_tn(k, v))
        mu = jnp.mean(o, axis=-1, keepdims=True)
        d = o - mu
        var = jnp.mean(d * d, axis=-1, keepdims=True)
        on = d * lax.rsqrt(var + GN_EPS) * gn_ref[:, hh * dv:(hh + 1) * dv]
        o_ref[:, hh * dv:(hh + 1) * dv] = (on * g_ref[:, hh * dv:(hh + 1) * dv].astype(F32)).astype(BF16)

    @pl.when(c == pl.num_programs(2) - 1)
    def _():
        sf_ref[...] = st_sc[...]


def _retention(proj, gn, state0, o_prev, nb, s, c_len, row0, dk, dv, hpb):
    T = proj.shape[0]
    H = RET_HEADS
    nc = s // c_len
    rb0 = row0 // c_len
    nhb = H // hpb
    has_state = state0 is not None
    rowblk = lambda b, h, c: rb0 + b * nc + c
    wk, wv = hpb * dk, hpb * dv
    kb = H * dk // wk
    vb = 2 * H * dk // wv
    in_specs = [pl.BlockSpec((c_len, wk), lambda b, h, c: (rowblk(b, h, c), h)),
                pl.BlockSpec((c_len, wk), lambda b, h, c: (rowblk(b, h, c), kb + h)),
                pl.BlockSpec((c_len, wv), lambda b, h, c: (rowblk(b, h, c), vb + h)),
                pl.BlockSpec((c_len, wv), lambda b, h, c: (rowblk(b, h, c), vb + nhb + h)),
                pl.BlockSpec((1, wv), lambda b, h, c: (0, h))]
    args = [proj, proj, proj, proj, gn]
    if has_state:
        in_specs.append(pl.BlockSpec((None, hpb, dk, dv), lambda b, h, c: (b, h, 0, 0)))
        args.append(state0)
    aliases = {}
    if o_prev is not None:
        in_specs.append(pl.BlockSpec(memory_space=pl.ANY))
        args.append(o_prev)
        aliases = {len(args) - 1: 0}

    def kern(*refs):
        if o_prev is not None:
            n_in = len(args)
            refs = refs[:n_in - 1] + refs[n_in:]
        _ret_kernel(*refs, c_len=c_len, has_state=has_state, hpb=hpb, dk=dk, dv=dv)

    return pl.pallas_call(
        kern,
        grid=(nb, nhb, nc),
        in_specs=in_specs,
        out_specs=[pl.BlockSpec((c_len, wv), lambda b, h, c: (rowblk(b, h, c), h)),
                   pl.BlockSpec((None, hpb, dk, dv), lambda b, h, c: (b, h, 0, 0))],
        out_shape=[jax.ShapeDtypeStruct((T, H * dv), BF16),
                   jax.ShapeDtypeStruct((nb, H, dk, dv), F32)],
        scratch_shapes=[pltpu.VMEM((hpb, dk, dv), F32)],
        input_output_aliases=aliases,
        compiler_params=_cp(("parallel", "parallel", "arbitrary")),
        name="retention",
    )(*args)


def _c_out_kernel(x_ref, o_ref, w_ref, y_ref):
    y_ref[...] = x_ref[...] + _dot(o_ref[...], w_ref[...])


def _c_out(x, o, w, tm, tn):
    T, D = x.shape
    K = o.shape[1]
    return pl.pallas_call(
        _c_out_kernel,
        grid=(T // tm, D // tn),
        in_specs=[pl.BlockSpec((tm, tn), lambda i, j: (i, j)), pl.BlockSpec((tm, K), lambda i, j: (i, 0)),
                  pl.BlockSpec((K, tn), lambda i, j: (0, j))],
        out_specs=pl.BlockSpec((tm, tn), lambda i, j: (i, j)),
        out_shape=jax.ShapeDtypeStruct((T, D), F32),
        compiler_params=_cp(("parallel", "arbitrary")),
        name="c_out",
    )(x, o, w)


def _rope_tables(pos, half):
    inv = ROPE_THETA ** (-jnp.arange(half, dtype=F32) / half)
    ang = pos.astype(F32)[:, None] * inv[None, :]
    return jnp.cos(ang), jnp.sin(ang)


def _swap_halves(w):
    half = w.shape[-1] // 2
    return jnp.concatenate([w[..., half:], w[..., :half]], axis=-1)


def _prep_w_in_ab(w):
    o1 = Q_LORA + KV_LORA
    o2 = o1 + MLA_ROPE
    kr = w[:, o1:o2]
    ks = _swap_halves(kr)
    return jnp.concatenate([w[:, :o1], kr, ks, ks, kr, w[:, o2:]], axis=1).astype(BF16)


def _prep_w_q_b(w):
    w = w.reshape(Q_LORA, MLA_HEADS, MLA_NOPE + MLA_ROPE)
    rope = w[..., MLA_NOPE:]
    swap = _swap_halves(rope)
    out = jnp.concatenate([w[..., :MLA_NOPE], rope, swap, swap, rope], axis=-1)
    return out.reshape(Q_LORA, -1).astype(BF16)


def _prep_w_kv_b(w):
    w = w.reshape(KV_LORA, MLA_HEADS, MLA_NOPE + MLA_V)
    return jnp.concatenate([w[..., :MLA_NOPE].reshape(KV_LORA, -1),
                            w[..., MLA_NOPE:].reshape(KV_LORA, -1)], axis=1).astype(BF16)


def kernel(x_prompt, x_sample, cache_mla_ckv, cache_mla_krope, state_s5_re, state_s5_im, state_ret,
           norm_mix, norm_mlp, norm_final, w_in_ab, q_a_norm, kv_a_norm, w_q_b, w_kv_b,
           s5_lam_re, s5_lam_im, s5_log_dt, s5_b_re, s5_b_im, s5_c_re, s5_c_im, s5_d, w_glu, b_glu,
           w_out_ab, w_in_c, ret_gn, w_out_c, w_up, w_down):
    BP, SP, D = x_prompt.shape
    BS, SS, _ = x_sample.shape
    past = cache_mla_ckv.shape[2]
    depth = norm_mix.shape[0]
    G, N = s5_lam_re.shape[1:]
    P, L = S5_GROUP, S5_L
    dk, dv = state_ret.shape[3:]
    TP, TS = BP * SP, BS * SS
    T = TP + TS
    TM = 512

    x = (x_prompt.reshape(TP, D), x_sample.reshape(TS, D))
    pos = jnp.concatenate([jnp.tile(jnp.arange(SP), BP), jnp.tile(past + jnp.arange(SS), BS)])

    c32, s32 = _rope_tables(pos, MLA_ROPE // 2)
    z64 = jnp.zeros((T, LANE - MLA_ROPE), F32)
    cs1 = jnp.concatenate([c32, c32, z64], axis=1)
    cs2 = jnp.concatenate([-s32, s32, z64], axis=1)
    cos_r, sin_r = _rope_tables(pos, dk // 2)

    skp = -(-(past + SS) // LANE) * LANE
    ckv_p, kr_p, s5r_p, s5i_p, ret_p = [], [], [], [], []
    ckv_s, kr_s, s5r_s, s5i_s, ret_s = [], [], [], [], []

    for layer in range(depth):
        i = layer // 2
        g_mix = norm_mix[layer].reshape(1, D)
        if layer % 2 == 0:
            xs = x if isinstance(x, tuple) else (x,)
            q, ckv, kr, u = _ab_in(xs, g_mix, _prep_w_in_ab(w_in_ab[i]), q_a_norm[i].reshape(1, -1),
                                   kv_a_norm[i].reshape(1, -1), _prep_w_q_b(w_q_b[i]), cs1, cs2, 256)
            ckv_p.append(ckv[:TP].reshape(BP, SP, KV_LORA))
            kr_p.append(kr[:TP, :MLA_ROPE].reshape(BP, SP, MLA_ROPE))
            ckv_s.append(ckv[TP:].reshape(BS, SS, KV_LORA))
            kr_s.append(kr[TP:, :MLA_ROPE].reshape(BS, SS, MLA_ROPE))
            pad = skp - past - SS
            ckv_hist = jnp.concatenate([cache_mla_ckv[i], ckv[TP:].reshape(BS, SS, KV_LORA),
                                        jnp.zeros((BS, pad, KV_LORA), F32)], axis=1)
            kr_past = jnp.pad(cache_mla_krope[i], ((0, 0), (0, 0), (0, LANE - MLA_ROPE)))
            kr_hist = jnp.concatenate([kr_past, kr[TP:].reshape(BS, SS, LANE),
                                       jnp.zeros((BS, pad, LANE), F32)], axis=1)
            w_kv = _prep_w_kv_b(w_kv_b[i])
            k_p, v_p = _kv_proj(ckv, kr, w_kv, 512)
            k_s, v_s = _kv_proj(ckv_hist.reshape(BS * skp, KV_LORA), kr_hist.reshape(BS * skp, LANE),
                                w_kv, skp)
            attn = jnp.concatenate([
                _attn_long(q, k_p, v_p, BP, SP, ATT_TQ, ATT_TK),
                _attn_short(q, k_s, v_s, BS, SS, skp, TP, 0, past, past + SS)], axis=0)
            params = _s5_params(s5_lam_re[i], s5_lam_im[i], s5_log_dt[i], s5_b_re[i], s5_b_im[i],
                                s5_c_re[i], s5_c_im[i])
            ncp, ncs = SP // L, SS // L
            ub = u.astype(BF16)
            u_p = ub[:TP].reshape(BP, ncp, L, G, P).transpose(3, 1, 0, 2, 4).reshape(G, ncp * BP, L * P)
            u_s = ub[TP:].reshape(BS, ncs, L, G, P).transpose(3, 1, 0, 2, 4).reshape(G, ncs * BS, L * P)
            y_p, hr_p, hi_p = _s5_main(u_p, params, None, BP, ncp)
            h0 = (jnp.swapaxes(state_s5_re[i], 0, 1), jnp.swapaxes(state_s5_im[i], 0, 1))
            y_s, hr_s, hi_s = _s5_main(u_s, params, h0, BS, ncs)
            y_p = y_p.reshape(G, ncp, BP, L, P).transpose(2, 1, 3, 0, 4).reshape(TP, G * P)
            y_s = y_s.reshape(G, ncs, BS, L, P).transpose(2, 1, 3, 0, 4).reshape(TS, G * P)
            y = jnp.concatenate([y_p, y_s], axis=0)
            s5r_p.append(jnp.swapaxes(hr_p, 0, 1))
            s5i_p.append(jnp.swapaxes(hi_p, 0, 1))
            s5r_s.append(jnp.swapaxes(hr_s, 0, 1))
            s5i_s.append(jnp.swapaxes(hi_s, 0, 1))
            x = _ab_out(xs, attn, y, u, s5_d[i].reshape(1, G * P), w_glu[i].astype(BF16),
                        b_glu[i].reshape(1, -1), w_out_ab[i].astype(BF16), 256)
        else:
            proj = _c_in(x, g_mix, w_in_c[i].astype(BF16), cos_r, sin_r, dk, dv, TM, 1024, TP, (RET_C, SS))
            gn = ret_gn[i].reshape(1, -1)
            o, st_p = _retention(proj, gn, None, None, BP, SP, RET_C, 0, dk, dv, RET_HPB)
            o, st_s = _retention(proj, gn, state_ret[i], o, BS, SS, SS, TP, dk, dv, RET_HPB)
            ret_p.append(st_p)
            ret_s.append(st_s)
            x = _c_out(x, o, w_out_c[i].astype(BF16), TM, 1024)
        last = layer == depth - 1
        outs = _mlp(x, norm_mlp[layer].reshape(1, D), w_up[layer].astype(BF16), w_down[layer].astype(BF16),
                    norm_final.reshape(1, D), last, (TP, TS) if last else (T,), TM, MLP_TF)
        x = outs[0]

    return (outs[0].reshape(BP, SP, D), outs[1].reshape(BS, SS, D),
            jnp.stack(ckv_p), jnp.stack(kr_p), jnp.stack(s5r_p), jnp.stack(s5i_p), jnp.stack(ret_p),
            jnp.stack(ckv_s), jnp.stack(kr_s), jnp.stack(s5r_s), jnp.stack(s5i_s), jnp.stack(ret_s))
```

```python
import functools
import math

import jax
import jax.numpy as jnp
import numpy as np
from jax import lax
from jax.experimental import pallas as pl
from jax.experimental.pallas import tpu as pltpu

F32 = jnp.float32
BF16 = jnp.bfloat16

CHUNK = 64
MLA_V = 128
MLA_NOPE = 128
MLA_ROPE = 64
MLA_HEADS = 8
Q_LORA = 512
KV_LORA = 256
MLA_SCALE = (MLA_NOPE + MLA_ROPE) ** -0.5
S5_GROUP = 16
S5_STATE = 64
RET_HEADS = 8
ROPE_THETA = 10000.0
EPS = 1e-6
GN_EPS = 1e-5
LOG2E = 1.4426950408889634

LANE = 128
QK_PAD = 256
V_PAD = 256
S5_L = 64
RET_C = 256
RET_HPB = 4
MLP_TF = 1024
ATT_TK = 512
NEG_BIG = -1e30
VMEM_LIMIT = 56 * 2 ** 20


def _cp(sem):
    return pltpu.CompilerParams(dimension_semantics=sem, vmem_limit_bytes=VMEM_LIMIT)


def _rms(x, g):
    return x * lax.rsqrt(jnp.mean(x * x, axis=-1, keepdims=True) + EPS) * g


def _dot(a, b):
    return jnp.dot(a, b, preferred_element_type=F32)


def _dot_nt(a, b):
    return lax.dot_general(a, b, (((1,), (1,)), ((), ())), preferred_element_type=F32)


def _dot_tn(a, b):
    return lax.dot_general(a, b, (((0,), (0,)), ((), ())), preferred_element_type=F32)


def _parts_specs(rows, tm, width, row_axis=0):
    specs, starts, start = [], [], 0
    for r in rows:
        nblk = r // tm

        def index_map(*g, start=start, nblk=nblk):
            return (jnp.clip(g[row_axis] - start, 0, nblk - 1), 0)

        specs.append(pl.BlockSpec((tm, width), index_map))
        starts.append(start)
        start += nblk
    return specs, tuple(starts)


def _parts_load(refs, starts, i):
    x = refs[0][...]
    for ref, s in zip(refs[1:], starts[1:]):
        x = jnp.where(i >= s, ref[...], x)
    return x


def _parts_store(refs, starts, i, val):
    ends = starts[1:] + (None,)
    for ref, s, e in zip(refs, starts, ends):
        cond = i >= s if e is None else (i >= s) & (i < e)

        @pl.when(cond)
        def _(ref=ref):
            ref[...] = val


def _ab_in_kernel(*refs, x_starts):
    n = len(x_starts)
    x_refs = refs[:n]
    (g_ref, win_ref, qn_ref, kvn_ref, wq_ref, cs1_ref, cs2_ref,
     q_out, ckv_out, kr_out, u_out) = refs[n:]
    x = _parts_load(x_refs, x_starts, pl.program_id(0))
    xn = _rms(x, g_ref[...]).astype(BF16)
    proj = _dot(xn, win_ref[...])
    cqn = _rms(proj[:, :Q_LORA], qn_ref[...]).astype(BF16)
    o = Q_LORA
    ckv_out[...] = _rms(proj[:, o:o + KV_LORA], kvn_ref[...])
    o += KV_LORA
    cs1 = cs1_ref[...]
    cs2 = cs2_ref[...]
    kr_out[...] = proj[:, o:o + LANE] * cs1 + proj[:, o + LANE:o + 2 * LANE] * cs2
    o += 2 * LANE
    u_out[...] = proj[:, o:]
    q = _dot(cqn, wq_ref[...])
    qs = MLA_SCALE * LOG2E
    for h in range(MLA_HEADS):
        b = h * 3 * LANE
        q_out[:, h * QK_PAD:h * QK_PAD + LANE] = (q[:, b:b + LANE] * qs).astype(BF16)
        rope = q[:, b + LANE:b + 2 * LANE] * cs1 + q[:, b + 2 * LANE:b + 3 * LANE] * cs2
        q_out[:, h * QK_PAD + LANE:(h + 1) * QK_PAD] = (rope * qs).astype(BF16)


def _ab_in(xs, g, win, qn, kvn, wq, cs1, cs2, tm):
    D = xs[0].shape[1]
    T = sum(x.shape[0] for x in xs)
    row = lambda i: (i, 0)
    fix = lambda i: (0, 0)
    s5w = win.shape[1] - Q_LORA - KV_LORA - 2 * LANE
    x_specs, x_starts = _parts_specs([x.shape[0] for x in xs], tm, D)
    return pl.pallas_call(
        functools.partial(_ab_in_kernel, x_starts=x_starts),
        grid=(T // tm,),
        in_specs=x_specs + [pl.BlockSpec((1, D), fix),
                  pl.BlockSpec(win.shape, fix), pl.BlockSpec((1, Q_LORA), fix),
                  pl.BlockSpec((1, KV_LORA), fix), pl.BlockSpec(wq.shape, fix),
                  pl.BlockSpec((tm, LANE), row), pl.BlockSpec((tm, LANE), row)],
        out_specs=[pl.BlockSpec((tm, MLA_HEADS * QK_PAD), row), pl.BlockSpec((tm, KV_LORA), row),
                   pl.BlockSpec((tm, LANE), row), pl.BlockSpec((tm, s5w), row)],
        out_shape=[jax.ShapeDtypeStruct((T, MLA_HEADS * QK_PAD), BF16),
                   jax.ShapeDtypeStruct((T, KV_LORA), F32),
                   jax.ShapeDtypeStruct((T, LANE), F32),
                   jax.ShapeDtypeStruct((T, s5w), F32)],
        compiler_params=_cp(("parallel",)),
        name="ab_in",
    )(*xs, g, win, qn, kvn, wq, cs1, cs2)


def _kv_kernel(ckv_ref, kr_ref, w_ref, k_out, v_out):
    kv = _dot(ckv_ref[...].astype(BF16), w_ref[...])
    kr = kr_ref[...].astype(BF16)
    v0 = MLA_HEADS * MLA_NOPE
    ones = jnp.ones((kv.shape[0], V_PAD - MLA_V), BF16)
    for h in range(MLA_HEADS):
        k_out[:, h * QK_PAD:h * QK_PAD + LANE] = kv[:, h * LANE:(h + 1) * LANE].astype(BF16)
        k_out[:, h * QK_PAD + LANE:(h + 1) * QK_PAD] = kr
        v_out[:, h * V_PAD:h * V_PAD + MLA_V] = kv[:, v0 + h * MLA_V:v0 + (h + 1) * MLA_V].astype(BF16)
        v_out[:, h * V_PAD + MLA_V:(h + 1) * V_PAD] = ones


def _kv_proj(ckv, kr, w, tm):
    R = ckv.shape[0]
    row = lambda i: (i, 0)
    return pl.pallas_call(
        _kv_kernel,
        grid=(R // tm,),
        in_specs=[pl.BlockSpec((tm, KV_LORA), row), pl.BlockSpec((tm, LANE), row),
                  pl.BlockSpec(w.shape, lambda i: (0, 0))],
        out_specs=[pl.BlockSpec((tm, MLA_HEADS * QK_PAD), row),
                   pl.BlockSpec((tm, MLA_HEADS * V_PAD), row)],
        out_shape=[jax.ShapeDtypeStruct((R, MLA_HEADS * QK_PAD), BF16),
                   jax.ShapeDtypeStruct((R, MLA_HEADS * V_PAD), BF16)],
        compiler_params=_cp(("parallel",)),
        name="kv_proj",
    )(ckv, kr, w)


def _attn_long_kernel(q_ref, k_ref, v_ref, o_ref, s_sc, m_sc, acc_sc, *, tq, tk):
    qi = pl.program_id(2)

    def scores_to(slot, kt):
        s_sc[slot] = _dot_nt(q_ref[...], k_ref[pl.ds(pl.multiple_of(kt * tk, tk), tk), :])

    def consume(slot, kt, masked):
        s = s_sc[slot]
        if masked:
            qpos = qi * tq + lax.broadcasted_iota(jnp.int32, (tq, tk), 0)
            kpos = kt * tk + lax.broadcasted_iota(jnp.int32, (tq, tk), 1)
            s = jnp.where(kpos // CHUNK <= qpos // CHUNK, s, NEG_BIG)
        m = m_sc[...]
        m_new = jnp.maximum(m, jnp.max(s, axis=1, keepdims=True))
        alpha = jnp.exp2(m - m_new)
        p = jnp.exp2(s - jnp.concatenate([m_new] * (tk // LANE), axis=1))
        m_sc[...] = m_new
        v = v_ref[pl.ds(pl.multiple_of(kt * tk, tk), tk), :]
        acc_sc[...] = (jnp.concatenate([alpha] * (V_PAD // LANE), axis=1) * acc_sc[...]
                       + _dot(p.astype(BF16), v))

    m_sc[...] = jnp.full_like(m_sc, NEG_BIG)
    acc_sc[...] = jnp.zeros_like(acc_sc)
    scores_to(0, 0)

    def pair(j, carry):
        scores_to(1, 2 * j + 1)
        consume(0, 2 * j, False)
        scores_to(0, 2 * j + 2)
        consume(1, 2 * j + 1, False)
        return carry

    lax.fori_loop(0, qi, pair, 0)
    scores_to(1, 2 * qi + 1)
    consume(0, 2 * qi, True)
    consume(1, 2 * qi + 1, True)
    acc = acc_sc[...]
    o_ref[...] = (acc[:, :MLA_V] / acc[:, MLA_V:]).astype(BF16)


def _attn_long(q, k, v, nb, s, tk):
    tq = 2 * tk
    assert s % tq == 0 and V_PAD == 2 * MLA_V
    nq = s // tq
    return pl.pallas_call(
        functools.partial(_attn_long_kernel, tq=tq, tk=tk),
        grid=(nb, MLA_HEADS, nq),
        in_specs=[pl.BlockSpec((tq, QK_PAD), lambda b, h, i: (b * nq + i, h)),
                  pl.BlockSpec((s, QK_PAD), lambda b, h, i: (b, h)),
                  pl.BlockSpec((s, V_PAD), lambda b, h, i: (b, h))],
        out_specs=pl.BlockSpec((tq, MLA_V), lambda b, h, i: (b * nq + i, h)),
        out_shape=jax.ShapeDtypeStruct((nb * s, MLA_HEADS * MLA_V), BF16),
        scratch_shapes=[pltpu.VMEM((2, tq, tk), F32), pltpu.VMEM((tq, LANE), F32),
                        pltpu.VMEM((tq, V_PAD), F32)],
        compiler_params=_cp(("parallel", "parallel", "arbitrary")),
        name="attn_long",
    )(q, k, v)


def _attn_short_kernel(q_ref, k_ref, v_ref, o_ref, *, q_pos0, n_keys):
    sq = q_ref.shape[0]
    skp = k_ref.shape[0]
    s = _dot_nt(q_ref[...], k_ref[...])
    qpos = q_pos0 + lax.broadcasted_iota(jnp.int32, (sq, skp), 0)
    kpos = lax.broadcasted_iota(jnp.int32, (sq, skp), 1)
    ok = (kpos // CHUNK <= qpos // CHUNK) & (kpos < n_keys)
    s = jnp.where(ok, s, NEG_BIG)
    m = jnp.max(s, axis=1, keepdims=True)
    p = jnp.exp2(s - m)
    acc = _dot(p.astype(BF16), v_ref[...])
    o_ref[...] = (acc[:, :MLA_V] / acc[:, MLA_V:]).astype(BF16)


def _attn_short(q, k, v, nb, sq, skp, q_row0, k_row0, q_pos0, n_keys):
    qb0 = q_row0 // sq
    kb0 = k_row0 // skp
    return pl.pallas_call(
        functools.partial(_attn_short_kernel, q_pos0=q_pos0, n_keys=n_keys),
        grid=(nb, MLA_HEADS),
        in_specs=[pl.BlockSpec((sq, QK_PAD), lambda b, h: (qb0 + b, h)),
                  pl.BlockSpec((skp, QK_PAD), lambda b, h: (kb0 + b, h)),
                  pl.BlockSpec((skp, V_PAD), lambda b, h: (kb0 + b, h))],
        out_specs=pl.BlockSpec((sq, MLA_V), lambda b, h: (b, h)),
        out_shape=jax.ShapeDtypeStruct((nb * sq, MLA_HEADS * MLA_V), BF16),
        compiler_params=_cp(("parallel", "parallel")),
        name="attn_short",
    )(q, k, v)


def _s5_param_kernel(lr_ref, li_ref, ldt_ref, btr_ref, bti_ref, cr_ref, ci_ref,
                     m_out, wr_out, wi_out, vr_out, vi_out, alr_out, ali_out,
                     car_sc, cai_sc):
    L, P = S5_L, S5_GROUP
    lr = jnp.minimum(lr_ref[...], -1e-4)
    li = li_ref[...]
    dt = jnp.exp(ldt_ref[...])
    mag = jnp.exp(lr * dt)
    a_re = mag * jnp.cos(li * dt)
    a_im = mag * jnp.sin(li * dt)
    den = lr * lr + li * li
    f_re = ((a_re - 1.0) * lr + a_im * li) / den
    f_im = (a_im * lr - (a_re - 1.0) * li) / den
    btr = btr_ref[...]
    bti = bti_ref[...]
    bb_re = f_re * btr - f_im * bti
    bb_im = f_re * bti + f_im * btr
    cr = cr_ref[...]
    ci = ci_ref[...]

    n_pow = -(-(L + 1) // 8) * 8
    t = lax.broadcasted_iota(jnp.int32, (n_pow, 1), 0).astype(F32)
    pmag = jnp.exp((lr * dt) * t)
    pang = (li * dt) * t
    pw_re = pmag * jnp.cos(pang)
    pw_im = pmag * jnp.sin(pang)

    def apow(t):
        return pw_re[t:t + 1, :], pw_im[t:t + 1, :]

    for i in range(L):
        rows = slice(i * P, (i + 1) * P)
        pr, pi = apow(L - 1 - i)
        wr_out[rows, :] = (pr * bb_re - pi * bb_im).astype(BF16)
        wi_out[rows, :] = (pr * bb_im + pi * bb_re).astype(BF16)
        pr, pi = apow(i + 1)
        vr_out[rows, :] = (cr * pr - ci * pi).astype(BF16)
        vi_out[rows, :] = (-(cr * pi + ci * pr)).astype(BF16)
        pr, pi = apow(i)
        car_sc[rows, :] = cr * pr - ci * pi
        cai_sc[rows, :] = cr * pi + ci * pr
    pr, pi = apow(L)
    alr_out[...] = pr
    ali_out[...] = pi
    hp = lax.Precision.HIGHEST
    nt = (((1,), (1,)), ((), ()))
    krow = (lax.dot_general(bb_re, car_sc[...], nt, precision=hp, preferred_element_type=F32)
            - lax.dot_general(bb_im, cai_sc[...], nt, precision=hp, preferred_element_type=F32))
    lane = lax.broadcasted_iota(jnp.int32, (P, L * P), 1)
    for i in range(L):
        shifted = krow if i == 0 else pltpu.roll(krow, i * P, axis=1)
        m_out[i * P:(i + 1) * P, :] = jnp.where(lane >= i * P, shifted, 0.0).astype(BF16)


def _s5_params(lam_re, lam_im, log_dt, b_re, b_im, c_re, c_im):
    G, N = lam_re.shape
    P, L = S5_GROUP, S5_L
    LP = L * P
    v3 = lambda a: a.reshape(G, 1, N)
    ldt = jnp.broadcast_to(log_dt[:, None, None], (G, 1, N))
    btr = jnp.swapaxes(b_re, 1, 2)
    bti = jnp.swapaxes(b_im, 1, 2)
    vec = pl.BlockSpec((None, 1, N), lambda g: (g, 0, 0))
    mat = pl.BlockSpec((None, P, N), lambda g: (g, 0, 0))
    big = pl.BlockSpec((None, LP, N), lambda g: (g, 0, 0))
    return pl.pallas_call(
        _s5_param_kernel,
        grid=(G,),
        in_specs=[vec, vec, vec, mat, mat, mat, mat],
        out_specs=[pl.BlockSpec((None, LP, LP), lambda g: (g, 0, 0)), big, big, big, big, vec, vec],
        out_shape=[jax.ShapeDtypeStruct((G, LP, LP), BF16)]
        + [jax.ShapeDtypeStruct((G, LP, N), BF16)] * 4
        + [jax.ShapeDtypeStruct((G, 1, N), F32)] * 2,
        scratch_shapes=[pltpu.VMEM((LP, N), F32), pltpu.VMEM((LP, N), F32)],
        compiler_params=_cp(("parallel",)),
        name="s5_params",
    )(v3(lam_re), v3(lam_im), ldt, btr, bti, c_re, c_im)


def _s5_main_kernel(*refs, nseq, nchunk, has_state):
    if has_state:
        (u_ref, m_ref, wr_ref, wi_ref, vr_ref, vi_ref, alr_ref, ali_ref, h0r_ref, h0i_ref,
         y_out, hfr_out, hfi_out, xr_sc, xi_sc, hr_sc, hi_sc) = refs
    else:
        (u_ref, m_ref, wr_ref, wi_ref, vr_ref, vi_ref, alr_ref, ali_ref,
         y_out, hfr_out, hfi_out, xr_sc, xi_sc, hr_sc, hi_sc) = refs
    ub = u_ref[...]
    xr_sc[...] = _dot(ub, wr_ref[...])
    xi_sc[...] = _dot(ub, wi_ref[...])
    ar = alr_ref[...]
    ai = ali_ref[...]
    n = ar.shape[-1]
    if has_state:
        init = (h0r_ref[...], h0i_ref[...])
    else:
        init = (jnp.zeros((nseq, n), F32), jnp.zeros((nseq, n), F32))

    def body(c, carry):
        sr, si = carry
        rows = pl.ds(pl.multiple_of(c * nseq, nseq), nseq)
        hr_sc[rows, :] = sr
        hi_sc[rows, :] = si
        return (ar * sr - ai * si + xr_sc[rows, :], ar * si + ai * sr + xi_sc[rows, :])

    sr, si = lax.fori_loop(0, nchunk, body, init)
    hfr_out[...] = sr
    hfi_out[...] = si
    y_out[...] = (_dot(ub, m_ref[...])
                  + _dot_nt(hr_sc[...].astype(BF16), vr_ref[...])
                  + _dot_nt(hi_sc[...].astype(BF16), vi_ref[...])).astype(y_out.dtype)


def _s5_main(u, params, h0, nseq, nchunk):
    m, wr, wi, vr, vi, alr, ali = params
    G, R, LP = u.shape
    N = alr.shape[-1]
    has_state = h0 is not None
    g3 = lambda g: (g, 0, 0)
    big = pl.BlockSpec((None, LP, N), g3)
    vec = pl.BlockSpec((None, 1, N), g3)
    st = pl.BlockSpec((None, nseq, N), g3)
    in_specs = [pl.BlockSpec((None, R, LP), g3), pl.BlockSpec((None, LP, LP), g3),
                big, big, big, big, vec, vec]
    args = [u, m, wr, wi, vr, vi, alr, ali]
    if has_state:
        in_specs += [st, st]
        args += list(h0)
    return pl.pallas_call(
        functools.partial(_s5_main_kernel, nseq=nseq, nchunk=nchunk, has_state=has_state),
        grid=(G,),
        in_specs=in_specs,
        out_specs=[pl.BlockSpec((None, R, LP), g3), st, st],
        out_shape=[jax.ShapeDtypeStruct((G, R, LP), BF16),
                   jax.ShapeDtypeStruct((G, nseq, N), F32),
                   jax.ShapeDtypeStruct((G, nseq, N), F32)],
        scratch_shapes=[pltpu.VMEM((R, N), F32)] * 4,
        compiler_params=_cp(("parallel",)),
        name="s5_main",
    )(*args)


def _ab_out_kernel(*refs, starts):
    i = pl.program_id(0)
    parts = []
    for st in starts:
        parts.append(_parts_load(refs[:len(st)], st, i))
        refs = refs[len(st):]
    x, attn, y = parts
    u_ref, d_ref, wglu_ref, bglu_ref, wo_ref, o_ref = refs
    y = y.astype(F32) + d_ref[...] * u_ref[...]
    z = y * (0.5 * (1.0 + jnp.tanh(math.sqrt(2.0 / math.pi) * (y + 0.044715 * (y * y * y)))))
    gate = 1.0 / (1.0 + jnp.exp(-(_dot(z.astype(BF16), wglu_ref[...]) + bglu_ref[...])))
    ssm = (z * gate).astype(BF16)
    na = attn.shape[1]
    o_ref[...] = x + _dot(attn, wo_ref[:na, :]) + _dot(ssm, wo_ref[na:, :])


def _ab_out(xs, attns, ys, u, d, wglu, bglu, wo, li, tm):
    D = xs[0].shape[1]
    T = sum(x.shape[0] for x in xs)
    row = lambda i: (i, 0)
    fix = lambda i: (0, 0)
    lay = lambda i: (li, 0, 0)
    specs, starts = [], []
    for parts in (xs, attns, ys):
        sp, st = _parts_specs([p.shape[0] for p in parts], tm, parts[0].shape[1])
        specs += sp
        starts.append(st)
    return pl.pallas_call(
        functools.partial(_ab_out_kernel, starts=tuple(starts)),
        grid=(T // tm,),
        in_specs=specs + [pl.BlockSpec((tm, u.shape[1]), row), pl.BlockSpec(d.shape, fix),
                          pl.BlockSpec((None,) + wglu.shape[1:], lay), pl.BlockSpec(bglu.shape, fix),
                          pl.BlockSpec((None,) + wo.shape[1:], lay)],
        out_specs=pl.BlockSpec((tm, D), row),
        out_shape=jax.ShapeDtypeStruct((T, D), F32),
        compiler_params=_cp(("parallel",)),
        name="ab_out",
    )(*xs, *attns, *ys, u, d, wglu, bglu, wo)


def _mlp_kernel(x_ref, g_ref, wup_ref, wdn_ref, gf_ref, *rest, final_norm, o_starts):
    o_refs = rest[:len(o_starts)]
    xn_sc, acc_sc = rest[len(o_starts):]
    f = pl.program_id(1)

    @pl.when(f == 0)
    def _():
        xn_sc[...] = _rms(x_ref[...], g_ref[...]).astype(BF16)
        acc_sc[...] = jnp.zeros_like(acc_sc)

    a = jnp.maximum(_dot(xn_sc[...], wup_ref[...]), 0.0)
    acc_sc[...] += _dot((a * a).astype(BF16), wdn_ref[...])

    @pl.when(f == pl.num_programs(1) - 1)
    def _():
        r = x_ref[...] + acc_sc[...]
        if final_norm:
            r = _rms(r, gf_ref[...])
        _parts_store(o_refs, o_starts, pl.program_id(0), r)


def _mlp(x, g, wup, wdn, li, gf, final_norm, out_rows, tm, tf):
    T, D = x.shape
    FF = wup.shape[2]
    assert sum(out_rows) == T
    o_specs, o_starts = _parts_specs(out_rows, tm, D)
    return pl.pallas_call(
        functools.partial(_mlp_kernel, final_norm=final_norm, o_starts=o_starts),
        grid=(T // tm, FF // tf),
        in_specs=[pl.BlockSpec((tm, D), lambda i, f: (i, 0)), pl.BlockSpec((1, D), lambda i, f: (0, 0)),
                  pl.BlockSpec((None, D, tf), lambda i, f: (li, 0, f)),
                  pl.BlockSpec((None, tf, D), lambda i, f: (li, f, 0)),
                  pl.BlockSpec((1, D), lambda i, f: (0, 0))],
        out_specs=o_specs,
        out_shape=[jax.ShapeDtypeStruct((r, D), F32) for r in out_rows],
        scratch_shapes=[pltpu.VMEM((tm, D), BF16), pltpu.VMEM((tm, D), F32)],
        compiler_params=_cp(("arbitrary" if len(out_rows) > 1 else "parallel", "arbitrary")),
        name="mlp",
    )(x, g, wup, wdn, gf)


def _ret_log_decay(head):
    hf = jnp.full((1, 1), head, jnp.int32).astype(F32)
    return jnp.log(1.0 - jnp.exp2(-5.0 - hf))


def _c_in_kernel(x_ref, g_ref, w_ref, cos_ref, sin_ref, o_ref, xn_sc, *, n_rot, n_q, n_lin, dk,
                 chunk_split_row, chunk_lens):
    i = pl.program_id(0)
    j = pl.program_id(1)

    @pl.when(j == 0)
    def _():
        xn_sc[...] = _rms(x_ref[...], g_ref[...]).astype(BF16)

    tm = x_ref.shape[0]
    tn = w_ref.shape[1]
    half = dk // 2

    def chunks():
        for s in range(0, tn, dk):
            yield s, _dot(xn_sc[...], w_ref[:, s:s + dk])

    @pl.when(j < n_rot)
    def _():
        cos = cos_ref[...]
        sin = sin_ref[...]
        c_len = jnp.where(i * tm >= chunk_split_row, chunk_lens[1], chunk_lens[0])
        pos = (lax.broadcasted_iota(jnp.int32, (tm, 1), 0) & (c_len - 1)).astype(F32) + 1.0
        sign = jnp.where(j < n_q, 1.0, -1.0)
        scale = jnp.where(j < n_q, 1.0, dk ** -0.5)
        for s, acc in chunks():
            lg = _ret_log_decay(((j * tn + s) // dk) % RET_HEADS)
            dec = jnp.exp((sign * lg) * pos) * scale
            x1 = acc[:, :half]
            x2 = acc[:, half:]
            o_ref[:, s:s + half] = ((x1 * cos - x2 * sin) * dec).astype(BF16)
            o_ref[:, s + half:s + dk] = ((x1 * sin + x2 * cos) * dec).astype(BF16)

    @pl.when((j >= n_rot) & (j < n_lin))
    def _():
        for s, acc in chunks():
            o_ref[:, s:s + dk] = acc.astype(BF16)

    @pl.when(j >= n_lin)
    def _():
        for s, acc in chunks():
            o_ref[:, s:s + dk] = (acc / (1.0 + jnp.exp(-acc))).astype(BF16)


def _c_in(x, g, w, li, cos, sin, dk, dv, tm, tn, chunk_split_row, chunk_lens):
    T, D = x.shape
    N = w.shape[2]
    qk = RET_HEADS * dk
    n_q = qk // tn
    n_rot = 2 * qk // tn
    n_lin = (2 * qk + RET_HEADS * dv) // tn
    assert chunk_split_row % tm == 0
    assert all(tm % c == 0 and c & (c - 1) == 0 for c in chunk_lens)
    return pl.pallas_call(
        functools.partial(_c_in_kernel, n_rot=n_rot, n_q=n_q, n_lin=n_lin, dk=dk,
                          chunk_split_row=chunk_split_row, chunk_lens=chunk_lens),
        grid=(T // tm, N // tn),
        in_specs=[pl.BlockSpec((tm, D), lambda i, j: (i, 0)), pl.BlockSpec((1, D), lambda i, j: (0, 0)),
                  pl.BlockSpec((None, D, tn), lambda i, j: (li, 0, j)),
                  pl.BlockSpec((tm, dk // 2), lambda i, j: (i, 0)),
                  pl.BlockSpec((tm, dk // 2), lambda i, j: (i, 0))],
        out_specs=pl.BlockSpec((tm, tn), lambda i, j: (i, j)),
        out_shape=jax.ShapeDtypeStruct((T, N), BF16),
        scratch_shapes=[pltpu.VMEM((tm, D), BF16)],
        compiler_params=_cp(("parallel", "arbitrary")),
        name="c_in",
    )(x, g, w, cos, sin)


def _ret_kernel(*refs, c_len, has_state, hpb, dk, dv):
    if has_state:
        q_ref, k_ref, v_ref, g_ref, gn_ref, s0_ref, o_ref, sf_ref, st_sc = refs
    else:
        q_ref, k_ref, v_ref, g_ref, gn_ref, o_ref, sf_ref, st_sc = refs
    hb = pl.program_id(1)
    c = pl.program_id(2)

    @pl.when(c == 0)
    def _():
        if has_state:
            st_sc[...] = s0_ref[...]
        else:
            st_sc[...] = jnp.zeros_like(st_sc)

    row = lax.broadcasted_iota(jnp.int32, (c_len, c_len), 0)
    col = lax.broadcasted_iota(jnp.int32, (c_len, c_len), 1)
    causal = row >= col
    for hh in range(hpb):
        q = q_ref[:, hh * dk:(hh + 1) * dk]
        k = k_ref[:, hh * dk:(hh + 1) * dk]
        v = v_ref[:, hh * dv:(hh + 1) * dv]
        inner = _dot(jnp.where(causal, _dot_nt(q, k), 0.0).astype(BF16), v)
        st = st_sc[hh]
        o = inner + _dot(q, st.astype(BF16))
        g_c = jnp.exp(_ret_log_decay(hb * hpb + hh) * float(c_len))
        st_sc[hh] = g_c * (st + _dot_tn(k, v))
        mu = jnp.mean(o, axis=-1, keepdims=True)
        d = o - mu
        var = jnp.mean(d * d, axis=-1, keepdims=True)
        on = d * lax.rsqrt(var + GN_EPS) * gn_ref[:, hh * dv:(hh + 1) * dv]
        o_ref[:, hh * dv:(hh + 1) * dv] = (on * g_ref[:, hh * dv:(hh + 1) * dv].astype(F32)).astype(BF16)

    @pl.when(c == pl.num_programs(2) - 1)
    def _():
        sf_ref[...] = st_sc[...]


def _retention(proj, gn, state0, o_prev, nb, s, c_len, row0, dk, dv, hpb):
    T = proj.shape[0]
    H = RET_HEADS
    nc = s // c_len
    rb0 = row0 // c_len
    nhb = H // hpb
    has_state = state0 is not None
    rowblk = lambda b, h, c: rb0 + b * nc + c
    wk, wv = hpb * dk, hpb * dv
    kb = H * dk // wk
    vb = 2 * H * dk // wv
    in_specs = [pl.BlockSpec((c_len, wk), lambda b, h, c: (rowblk(b, h, c), h)),
                pl.BlockSpec((c_len, wk), lambda b, h, c: (rowblk(b, h, c), kb + h)),
                pl.BlockSpec((c_len, wv), lambda b, h, c: (rowblk(b, h, c), vb + h)),
                pl.BlockSpec((c_len, wv), lambda b, h, c: (rowblk(b, h, c), vb + nhb + h)),
                pl.BlockSpec((1, wv), lambda b, h, c: (0, h))]
    args = [proj, proj, proj, proj, gn]
    if has_state:
        in_specs.append(pl.BlockSpec((None, hpb, dk, dv), lambda b, h, c: (b, h, 0, 0)))
        args.append(state0)
    aliases = {}
    if o_prev is not None:
        in_specs.append(pl.BlockSpec(memory_space=pl.ANY))
        args.append(o_prev)
        aliases = {len(args) - 1: 0}

    def kern(*refs):
        if o_prev is not None:
            n_in = len(args)
            refs = refs[:n_in - 1] + refs[n_in:]
        _ret_kernel(*refs, c_len=c_len, has_state=has_state, hpb=hpb, dk=dk, dv=dv)

    return pl.pallas_call(
        kern,
        grid=(nb, nhb, nc),
        in_specs=in_specs,
        out_specs=[pl.BlockSpec((c_len, wv), lambda b, h, c: (rowblk(b, h, c), h)),
                   pl.BlockSpec((None, hpb, dk, dv), lambda b, h, c: (b, h, 0, 0))],
        out_shape=[jax.ShapeDtypeStruct((T, H * dv), BF16),
                   jax.ShapeDtypeStruct((nb, H, dk, dv), F32)],
        scratch_shapes=[pltpu.VMEM((hpb, dk, dv), F32)],
        input_output_aliases=aliases,
        compiler_params=_cp(("parallel", "parallel", "arbitrary")),
        name="retention",
    )(*args)


def _c_out_kernel(x_ref, o_ref, w_ref, y_ref):
    y_ref[...] = x_ref[...] + _dot(o_ref[...], w_ref[...])


def _c_out(x, o, w, li, tm, tn):
    T, D = x.shape
    K = o.shape[1]
    return pl.pallas_call(
        _c_out_kernel,
        grid=(T // tm, D // tn),
        in_specs=[pl.BlockSpec((tm, tn), lambda i, j: (i, j)), pl.BlockSpec((tm, K), lambda i, j: (i, 0)),
                  pl.BlockSpec((None, K, tn), lambda i, j: (li, 0, j))],
        out_specs=pl.BlockSpec((tm, tn), lambda i, j: (i, j)),
        out_shape=jax.ShapeDtypeStruct((T, D), F32),
        compiler_params=_cp(("parallel", "arbitrary")),
        name="c_out",
    )(x, o, w)


def _rope_tables(pos, half):
    inv = ROPE_THETA ** (-jnp.arange(half, dtype=F32) / half)
    ang = pos.astype(F32)[:, None] * inv[None, :]
    return jnp.cos(ang), jnp.sin(ang)


def _swap_halves(w):
    half = w.shape[-1] // 2
    return jnp.concatenate([w[..., half:], w[..., :half]], axis=-1)


def _prep_w_in_ab(w):
    o1 = Q_LORA + KV_LORA
    o2 = o1 + MLA_ROPE
    kr = w[:, o1:o2]
    ks = _swap_halves(kr)
    return jnp.concatenate([w[:, :o1], kr, ks, ks, kr, w[:, o2:]], axis=1).astype(BF16)


def _prep_w_q_b(w):
    w = w.reshape(Q_LORA, MLA_HEADS, MLA_NOPE + MLA_ROPE)
    rope = w[..., MLA_NOPE:]
    swap = _swap_halves(rope)
    out = jnp.concatenate([w[..., :MLA_NOPE], rope, swap, swap, rope], axis=-1)
    return out.reshape(Q_LORA, -1).astype(BF16)


def _prep_w_kv_b(w):
    w = w.reshape(KV_LORA, MLA_HEADS, MLA_NOPE + MLA_V)
    return jnp.concatenate([w[..., :MLA_NOPE].reshape(KV_LORA, -1),
                            w[..., MLA_NOPE:].reshape(KV_LORA, -1)], axis=1).astype(BF16)


def kernel(x_prompt, x_sample, cache_mla_ckv, cache_mla_krope, state_s5_re, state_s5_im, state_ret,
           norm_mix, norm_mlp, norm_final, w_in_ab, q_a_norm, kv_a_norm, w_q_b, w_kv_b,
           s5_lam_re, s5_lam_im, s5_log_dt, s5_b_re, s5_b_im, s5_c_re, s5_c_im, s5_d, w_glu, b_glu,
           w_out_ab, w_in_c, ret_gn, w_out_c, w_up, w_down):
    BP, SP, D = x_prompt.shape
    BS, SS, _ = x_sample.shape
    past = cache_mla_ckv.shape[2]
    depth = norm_mix.shape[0]
    G, N = s5_lam_re.shape[1:]
    P, L = S5_GROUP, S5_L
    dk, dv = state_ret.shape[3:]
    TP, TS = BP * SP, BS * SS
    T = TP + TS
    TM = 512

    x = (x_prompt.reshape(TP, D), x_sample.reshape(TS, D))
    pos = jnp.concatenate([jnp.tile(jnp.arange(SP), BP), jnp.tile(past + jnp.arange(SS), BS)])

    c32, s32 = _rope_tables(pos, MLA_ROPE // 2)
    z64 = jnp.zeros((T, LANE - MLA_ROPE), F32)
    cs1 = jnp.concatenate([c32, c32, z64], axis=1)
    cs2 = jnp.concatenate([-s32, s32, z64], axis=1)
    cos_r, sin_r = _rope_tables(pos, dk // 2)

    skp = -(-(past + SS) // LANE) * LANE
    w_glu_b, w_out_ab_b, w_in_c_b, w_out_c_b, w_up_b, w_down_b = (
        w.astype(BF16) for w in (w_glu, w_out_ab, w_in_c, w_out_c, w_up, w_down))
    ckv_p, kr_p, s5r_p, s5i_p, ret_p = [], [], [], [], []
    ckv_s, kr_s, s5r_s, s5i_s, ret_s = [], [], [], [], []

    for layer in range(depth):
        i = layer // 2
        g_mix = norm_mix[layer].reshape(1, D)
        if layer % 2 == 0:
            xs = x if isinstance(x, tuple) else (x,)
            q, ckv, kr, u = _ab_in(xs, g_mix, _prep_w_in_ab(w_in_ab[i]), q_a_norm[i].reshape(1, -1),
                                   kv_a_norm[i].reshape(1, -1), _prep_w_q_b(w_q_b[i]), cs1, cs2, 256)
            ckv_p.append(ckv[:TP].reshape(BP, SP, KV_LORA))
            kr_p.append(kr[:TP, :MLA_ROPE].reshape(BP, SP, MLA_ROPE))
            ckv_s.append(ckv[TP:].reshape(BS, SS, KV_LORA))
            kr_s.append(kr[TP:, :MLA_ROPE].reshape(BS, SS, MLA_ROPE))
            pad = skp - past - SS
            ckv_hist = jnp.concatenate([cache_mla_ckv[i], ckv[TP:].reshape(BS, SS, KV_LORA),
                                        jnp.zeros((BS, pad, KV_LORA), F32)], axis=1)
            kr_past = jnp.pad(cache_mla_krope[i], ((0, 0), (0, 0), (0, LANE - MLA_ROPE)))
            kr_hist = jnp.concatenate([kr_past, kr[TP:].reshape(BS, SS, LANE),
                                       jnp.zeros((BS, pad, LANE), F32)], axis=1)
            w_kv = _prep_w_kv_b(w_kv_b[i])
            k_p, v_p = _kv_proj(ckv, kr, w_kv, 512)
            k_s, v_s = _kv_proj(ckv_hist.reshape(BS * skp, KV_LORA), kr_hist.reshape(BS * skp, LANE),
                                w_kv, skp)
            attn = (_attn_long(q, k_p, v_p, BP, SP, ATT_TK),
                    _attn_short(q, k_s, v_s, BS, SS, skp, TP, 0, past, past + SS))
            params = _s5_params(s5_lam_re[i], s5_lam_im[i], s5_log_dt[i], s5_b_re[i], s5_b_im[i],
                                s5_c_re[i], s5_c_im[i])
            ncp, ncs = SP // L, SS // L
            ub = u.astype(BF16)
            u_p = ub[:TP].reshape(BP, ncp, L, G, P).transpose(3, 1, 0, 2, 4).reshape(G, ncp * BP, L * P)
            u_s = ub[TP:].reshape(BS, ncs, L, G, P).transpose(3, 1, 0, 2, 4).reshape(G, ncs * BS, L * P)
            y_p, hr_p, hi_p = _s5_main(u_p, params, None, BP, ncp)
            h0 = (jnp.swapaxes(state_s5_re[i], 0, 1), jnp.swapaxes(state_s5_im[i], 0, 1))
            y_s, hr_s, hi_s = _s5_main(u_s, params, h0, BS, ncs)
            y_p = y_p.reshape(G, ncp, BP, L, P).transpose(2, 1, 3, 0, 4).reshape(TP, G * P)
            y_s = y_s.reshape(G, ncs, BS, L, P).transpose(2, 1, 3, 0, 4).reshape(TS, G * P)
            s5r_p.append(jnp.swapaxes(hr_p, 0, 1))
            s5i_p.append(jnp.swapaxes(hi_p, 0, 1))
            s5r_s.append(jnp.swapaxes(hr_s, 0, 1))
            s5i_s.append(jnp.swapaxes(hi_s, 0, 1))
            x = _ab_out(xs, attn, (y_p, y_s), u, s5_d[i].reshape(1, G * P), w_glu_b,
                        b_glu[i].reshape(1, -1), w_out_ab_b, i, 256)
        else:
            proj = _c_in(x, g_mix, w_in_c_b, i, cos_r, sin_r, dk, dv, TM, 1024, TP, (RET_C, SS))
            gn = ret_gn[i].reshape(1, -1)
            o, st_p = _retention(proj, gn, None, None, BP, SP, RET_C, 0, dk, dv, RET_HPB)
            o, st_s = _retention(proj, gn, state_ret[i], o, BS, SS, SS, TP, dk, dv, RET_HPB)
            ret_p.append(st_p)
            ret_s.append(st_s)
            x = _c_out(x, o, w_out_c_b, i, TM, 1024)
        last = layer == depth - 1
        outs = _mlp(x, norm_mlp[layer].reshape(1, D), w_up_b, w_down_b, layer,
                    norm_final.reshape(1, D), last, (TP, TS) if last else (T,), TM, MLP_TF)
        x = outs[0]

    return (outs[0].reshape(BP, SP, D), outs[1].reshape(BS, SS, D),
            jnp.stack(ckv_p), jnp.stack(kr_p), jnp.stack(s5r_p), jnp.stack(s5i_p), jnp.stack(ret_p),
            jnp.stack(ckv_s), jnp.stack(kr_s), jnp.stack(s5r_s), jnp.stack(s5i_s), jnp.stack(ret_s))
```

```python
import functools
import math

import jax
import jax.numpy as jnp
import numpy as np
from jax import lax
from jax.experimental import pallas as pl
from jax.experimental.pallas import tpu as pltpu

F32 = jnp.float32
BF16 = jnp.bfloat16

CHUNK = 64
MLA_V = 128
MLA_NOPE = 128
MLA_ROPE = 64
MLA_HEADS = 8
Q_LORA = 512
KV_LORA = 256
MLA_SCALE = (MLA_NOPE + MLA_ROPE) ** -0.5
S5_GROUP = 16
S5_STATE = 64
RET_HEADS = 8
ROPE_THETA = 10000.0
EPS = 1e-6
GN_EPS = 1e-5
LOG2E = 1.4426950408889634

LANE = 128
QK_PAD = 256
V_PAD = 256
S5_L = 64
RET_C = 256
RET_HPB = 4
MLP_TF = 1024
ATT_TK = 512
NEG_BIG = -1e30
VMEM_LIMIT = 56 * 2 ** 20


def _cp(sem):
    return pltpu.CompilerParams(dimension_semantics=sem, vmem_limit_bytes=VMEM_LIMIT)


def _rms(x, g):
    return x * lax.rsqrt(jnp.mean(x * x, axis=-1, keepdims=True) + EPS) * g


def _dot(a, b):
    return jnp.dot(a, b, preferred_element_type=F32)


def _dot_nt(a, b):
    return lax.dot_general(a, b, (((1,), (1,)), ((), ())), preferred_element_type=F32)


def _dot_tn(a, b):
    return lax.dot_general(a, b, (((0,), (0,)), ((), ())), preferred_element_type=F32)


def _parts_specs(rows, tm, width, row_axis=0):
    specs, starts, start = [], [], 0
    for r in rows:
        nblk = r // tm

        def index_map(*g, start=start, nblk=nblk):
            return (jnp.clip(g[row_axis] - start, 0, nblk - 1), 0)

        specs.append(pl.BlockSpec((tm, width), index_map))
        starts.append(start)
        start += nblk
    return specs, tuple(starts)


def _parts_load(refs, starts, i):
    x = refs[0][...]
    for ref, s in zip(refs[1:], starts[1:]):
        x = jnp.where(i >= s, ref[...], x)
    return x


def _parts_store(refs, starts, i, val):
    ends = starts[1:] + (None,)
    for ref, s, e in zip(refs, starts, ends):
        cond = i >= s if e is None else (i >= s) & (i < e)

        @pl.when(cond)
        def _(ref=ref):
            ref[...] = val


def _ab_in_kernel(*refs, x_starts):
    n = len(x_starts)
    x_refs = refs[:n]
    (g_ref, win_ref, qn_ref, kvn_ref, wq_ref, cs1_ref, cs2_ref,
     q_out, ckv_out, kr_out, u_out) = refs[n:]
    x = _parts_load(x_refs, x_starts, pl.program_id(0))
    xn = _rms(x, g_ref[...]).astype(BF16)
    proj = _dot(xn, win_ref[...])
    cqn = _rms(proj[:, :Q_LORA], qn_ref[...]).astype(BF16)
    o = Q_LORA
    ckv_out[...] = _rms(proj[:, o:o + KV_LORA], kvn_ref[...])
    o += KV_LORA
    cs1 = cs1_ref[...]
    cs2 = cs2_ref[...]
    kr_out[...] = proj[:, o:o + LANE] * cs1 + proj[:, o + LANE:o + 2 * LANE] * cs2
    o += 2 * LANE
    u_out[...] = proj[:, o:]
    q = _dot(cqn, wq_ref[...])
    qs = MLA_SCALE * LOG2E
    for h in range(MLA_HEADS):
        b = h * 3 * LANE
        q_out[:, h * QK_PAD:h * QK_PAD + LANE] = (q[:, b:b + LANE] * qs).astype(BF16)
        rope = q[:, b + LANE:b + 2 * LANE] * cs1 + q[:, b + 2 * LANE:b + 3 * LANE] * cs2
        q_out[:, h * QK_PAD + LANE:(h + 1) * QK_PAD] = (rope * qs).astype(BF16)


def _ab_in(xs, g, win, qn, kvn, wq, cs1, cs2, tm):
    D = xs[0].shape[1]
    T = sum(x.shape[0] for x in xs)
    row = lambda i: (i, 0)
    fix = lambda i: (0, 0)
    s5w = win.shape[1] - Q_LORA - KV_LORA - 2 * LANE
    x_specs, x_starts = _parts_specs([x.shape[0] for x in xs], tm, D)
    return pl.pallas_call(
        functools.partial(_ab_in_kernel, x_starts=x_starts),
        grid=(T // tm,),
        in_specs=x_specs + [pl.BlockSpec((1, D), fix),
                  pl.BlockSpec(win.shape, fix), pl.BlockSpec((1, Q_LORA), fix),
                  pl.BlockSpec((1, KV_LORA), fix), pl.BlockSpec(wq.shape, fix),
                  pl.BlockSpec((tm, LANE), row), pl.BlockSpec((tm, LANE), row)],
        out_specs=[pl.BlockSpec((tm, MLA_HEADS * QK_PAD), row), pl.BlockSpec((tm, KV_LORA), row),
                   pl.BlockSpec((tm, LANE), row), pl.BlockSpec((tm, s5w), row)],
        out_shape=[jax.ShapeDtypeStruct((T, MLA_HEADS * QK_PAD), BF16),
                   jax.ShapeDtypeStruct((T, KV_LORA), F32),
                   jax.ShapeDtypeStruct((T, LANE), F32),
                   jax.ShapeDtypeStruct((T, s5w), F32)],
        compiler_params=_cp(("parallel",)),
        name="ab_in",
    )(*xs, g, win, qn, kvn, wq, cs1, cs2)


def _kv_kernel(ckv_ref, kr_ref, w_ref, k_out, v_out):
    kv = _dot(ckv_ref[...].astype(BF16), w_ref[...])
    kr = kr_ref[...].astype(BF16)
    v0 = MLA_HEADS * MLA_NOPE
    ones = jnp.ones((kv.shape[0], V_PAD - MLA_V), BF16)
    for h in range(MLA_HEADS):
        k_out[:, h * QK_PAD:h * QK_PAD + LANE] = kv[:, h * LANE:(h + 1) * LANE].astype(BF16)
        k_out[:, h * QK_PAD + LANE:(h + 1) * QK_PAD] = kr
        v_out[:, h * V_PAD:h * V_PAD + MLA_V] = kv[:, v0 + h * MLA_V:v0 + (h + 1) * MLA_V].astype(BF16)
        v_out[:, h * V_PAD + MLA_V:(h + 1) * V_PAD] = ones


def _kv_proj(ckv, kr, w, tm):
    R = ckv.shape[0]
    row = lambda i: (i, 0)
    return pl.pallas_call(
        _kv_kernel,
        grid=(R // tm,),
        in_specs=[pl.BlockSpec((tm, KV_LORA), row), pl.BlockSpec((tm, LANE), row),
                  pl.BlockSpec(w.shape, lambda i: (0, 0))],
        out_specs=[pl.BlockSpec((tm, MLA_HEADS * QK_PAD), row),
                   pl.BlockSpec((tm, MLA_HEADS * V_PAD), row)],
        out_shape=[jax.ShapeDtypeStruct((R, MLA_HEADS * QK_PAD), BF16),
                   jax.ShapeDtypeStruct((R, MLA_HEADS * V_PAD), BF16)],
        compiler_params=_cp(("parallel",)),
        name="kv_proj",
    )(ckv, kr, w)


def _attn_long_kernel(q_ref, k_ref, v_ref, o_ref, s_sc, m_sc, acc_sc, *, tq, tk):
    qi = pl.program_id(2)

    def scores_to(slot, kt, r0=0):
        s_sc[slot, r0:, :] = _dot_nt(q_ref[r0:, :], k_ref[pl.ds(pl.multiple_of(kt * tk, tk), tk), :])

    def consume(slot, kt, masked, r0=0):
        s = s_sc[slot, r0:, :]
        if masked:
            qpos = qi * tq + r0 + lax.broadcasted_iota(jnp.int32, (tq - r0, tk), 0)
            kpos = kt * tk + lax.broadcasted_iota(jnp.int32, (tq - r0, tk), 1)
            s = jnp.where(kpos // CHUNK <= qpos // CHUNK, s, NEG_BIG)
        m = m_sc[r0:, :]
        m_new = jnp.maximum(m, jnp.max(s, axis=1, keepdims=True))
        alpha = jnp.exp2(m - m_new)
        p = jnp.exp2(s - jnp.concatenate([m_new] * (tk // LANE), axis=1))
        m_sc[r0:, :] = m_new
        v = v_ref[pl.ds(pl.multiple_of(kt * tk, tk), tk), :]
        acc_sc[r0:, :] = (jnp.concatenate([alpha] * (V_PAD // LANE), axis=1) * acc_sc[r0:, :]
                          + _dot(p.astype(BF16), v))

    m_sc[...] = jnp.full_like(m_sc, NEG_BIG)
    acc_sc[...] = jnp.zeros_like(acc_sc)
    scores_to(0, 0)

    def pair(j, carry):
        scores_to(1, 2 * j + 1)
        consume(0, 2 * j, False)
        scores_to(0, 2 * j + 2)
        consume(1, 2 * j + 1, False)
        return carry

    lax.fori_loop(0, qi, pair, 0)
    scores_to(1, 2 * qi + 1, tk)
    consume(0, 2 * qi, True)
    consume(1, 2 * qi + 1, True, tk)
    acc = acc_sc[...]
    o_ref[...] = (acc[:, :MLA_V] / acc[:, MLA_V:]).astype(BF16)


def _attn_long(q, k, v, nb, s, tk):
    tq = 2 * tk
    assert s % tq == 0 and V_PAD == 2 * MLA_V
    nq = s // tq
    return pl.pallas_call(
        functools.partial(_attn_long_kernel, tq=tq, tk=tk),
        grid=(nb, MLA_HEADS, nq),
        in_specs=[pl.BlockSpec((tq, QK_PAD), lambda b, h, i: (b * nq + i, h)),
                  pl.BlockSpec((s, QK_PAD), lambda b, h, i: (b, h)),
                  pl.BlockSpec((s, V_PAD), lambda b, h, i: (b, h))],
        out_specs=pl.BlockSpec((tq, MLA_V), lambda b, h, i: (b * nq + i, h)),
        out_shape=jax.ShapeDtypeStruct((nb * s, MLA_HEADS * MLA_V), BF16),
        scratch_shapes=[pltpu.VMEM((2, tq, tk), F32), pltpu.VMEM((tq, LANE), F32),
                        pltpu.VMEM((tq, V_PAD), F32)],
        compiler_params=_cp(("parallel", "parallel", "arbitrary")),
        name="attn_long",
    )(q, k, v)


def _attn_short_kernel(q_ref, k_ref, v_ref, o_ref, *, q_pos0, n_keys):
    sq = q_ref.shape[0]
    skp = k_ref.shape[0]
    s = _dot_nt(q_ref[...], k_ref[...])
    qpos = q_pos0 + lax.broadcasted_iota(jnp.int32, (sq, skp), 0)
    kpos = lax.broadcasted_iota(jnp.int32, (sq, skp), 1)
    ok = (kpos // CHUNK <= qpos // CHUNK) & (kpos < n_keys)
    s = jnp.where(ok, s, NEG_BIG)
    m = jnp.max(s, axis=1, keepdims=True)
    p = jnp.exp2(s - m)
    acc = _dot(p.astype(BF16), v_ref[...])
    o_ref[...] = (acc[:, :MLA_V] / acc[:, MLA_V:]).astype(BF16)


def _attn_short(q, k, v, nb, sq, skp, q_row0, k_row0, q_pos0, n_keys):
    qb0 = q_row0 // sq
    kb0 = k_row0 // skp
    return pl.pallas_call(
        functools.partial(_attn_short_kernel, q_pos0=q_pos0, n_keys=n_keys),
        grid=(nb, MLA_HEADS),
        in_specs=[pl.BlockSpec((sq, QK_PAD), lambda b, h: (qb0 + b, h)),
                  pl.BlockSpec((skp, QK_PAD), lambda b, h: (kb0 + b, h)),
                  pl.BlockSpec((skp, V_PAD), lambda b, h: (kb0 + b, h))],
        out_specs=pl.BlockSpec((sq, MLA_V), lambda b, h: (b, h)),
        out_shape=jax.ShapeDtypeStruct((nb * sq, MLA_HEADS * MLA_V), BF16),
        compiler_params=_cp(("parallel", "parallel")),
        name="attn_short",
    )(q, k, v)


def _s5_param_kernel(lr_ref, li_ref, ldt_ref, btr_ref, bti_ref, cr_ref, ci_ref,
                     m_out, wr_out, wi_out, vr_out, vi_out, alr_out, ali_out,
                     car_sc, cai_sc):
    L, P = S5_L, S5_GROUP
    lr = jnp.minimum(lr_ref[...], -1e-4)
    li = li_ref[...]
    dt = jnp.exp(ldt_ref[...])
    mag = jnp.exp(lr * dt)
    a_re = mag * jnp.cos(li * dt)
    a_im = mag * jnp.sin(li * dt)
    den = lr * lr + li * li
    f_re = ((a_re - 1.0) * lr + a_im * li) / den
    f_im = (a_im * lr - (a_re - 1.0) * li) / den
    btr = btr_ref[...]
    bti = bti_ref[...]
    bb_re = f_re * btr - f_im * bti
    bb_im = f_re * bti + f_im * btr
    cr = cr_ref[...]
    ci = ci_ref[...]

    n_pow = -(-(L + 1) // 8) * 8
    t = lax.broadcasted_iota(jnp.int32, (n_pow, 1), 0).astype(F32)
    pmag = jnp.exp((lr * dt) * t)
    pang = (li * dt) * t
    pw_re = pmag * jnp.cos(pang)
    pw_im = pmag * jnp.sin(pang)

    def apow(t):
        return pw_re[t:t + 1, :], pw_im[t:t + 1, :]

    for i in range(L):
        rows = slice(i * P, (i + 1) * P)
        pr, pi = apow(L - 1 - i)
        wr_out[rows, :] = (pr * bb_re - pi * bb_im).astype(BF16)
        wi_out[rows, :] = (pr * bb_im + pi * bb_re).astype(BF16)
        pr, pi = apow(i + 1)
        vr_out[rows, :] = (cr * pr - ci * pi).astype(BF16)
        vi_out[rows, :] = (-(cr * pi + ci * pr)).astype(BF16)
        pr, pi = apow(i)
        car_sc[rows, :] = cr * pr - ci * pi
        cai_sc[rows, :] = cr * pi + ci * pr
    pr, pi = apow(L)
    alr_out[...] = pr
    ali_out[...] = pi
    hp = lax.Precision.HIGHEST
    nt = (((1,), (1,)), ((), ()))
    krow = (lax.dot_general(bb_re, car_sc[...], nt, precision=hp, preferred_element_type=F32)
            - lax.dot_general(bb_im, cai_sc[...], nt, precision=hp, preferred_element_type=F32))
    lane = lax.broadcasted_iota(jnp.int32, (P, L * P), 1)
    for i in range(L):
        shifted = krow if i == 0 else pltpu.roll(krow, i * P, axis=1)
        m_out[i * P:(i + 1) * P, :] = jnp.where(lane >= i * P, shifted, 0.0).astype(BF16)


def _s5_params(lam_re, lam_im, log_dt, b_re, b_im, c_re, c_im):
    G, N = lam_re.shape
    P, L = S5_GROUP, S5_L
    LP = L * P
    v3 = lambda a: a.reshape(G, 1, N)
    ldt = jnp.broadcast_to(log_dt[:, None, None], (G, 1, N))
    btr = jnp.swapaxes(b_re, 1, 2)
    bti = jnp.swapaxes(b_im, 1, 2)
    vec = pl.BlockSpec((None, 1, N), lambda g: (g, 0, 0))
    mat = pl.BlockSpec((None, P, N), lambda g: (g, 0, 0))
    big = pl.BlockSpec((None, LP, N), lambda g: (g, 0, 0))
    return pl.pallas_call(
        _s5_param_kernel,
        grid=(G,),
        in_specs=[vec, vec, vec, mat, mat, mat, mat],
        out_specs=[pl.BlockSpec((None, LP, LP), lambda g: (g, 0, 0)), big, big, big, big, vec, vec],
        out_shape=[jax.ShapeDtypeStruct((G, LP, LP), BF16)]
        + [jax.ShapeDtypeStruct((G, LP, N), BF16)] * 4
        + [jax.ShapeDtypeStruct((G, 1, N), F32)] * 2,
        scratch_shapes=[pltpu.VMEM((LP, N), F32), pltpu.VMEM((LP, N), F32)],
        compiler_params=_cp(("parallel",)),
        name="s5_params",
    )(v3(lam_re), v3(lam_im), ldt, btr, bti, c_re, c_im)


def _s5_main_kernel(*refs, nseq, nchunk, has_state):
    if has_state:
        (u_ref, m_ref, wr_ref, wi_ref, vr_ref, vi_ref, alr_ref, ali_ref, h0r_ref, h0i_ref,
         y_out, hfr_out, hfi_out, xr_sc, xi_sc, hr_sc, hi_sc) = refs
    else:
        (u_ref, m_ref, wr_ref, wi_ref, vr_ref, vi_ref, alr_ref, ali_ref,
         y_out, hfr_out, hfi_out, xr_sc, xi_sc, hr_sc, hi_sc) = refs
    ub = u_ref[...]
    xr_sc[...] = _dot(ub, wr_ref[...])
    xi_sc[...] = _dot(ub, wi_ref[...])
    ar = alr_ref[...]
    ai = ali_ref[...]
    n = ar.shape[-1]
    if has_state:
        init = (h0r_ref[...], h0i_ref[...])
    else:
        init = (jnp.zeros((nseq, n), F32), jnp.zeros((nseq, n), F32))

    def body(c, carry):
        sr, si = carry
        rows = pl.ds(pl.multiple_of(c * nseq, nseq), nseq)
        hr_sc[rows, :] = sr
        hi_sc[rows, :] = si
        return (ar * sr - ai * si + xr_sc[rows, :], ar * si + ai * sr + xi_sc[rows, :])

    sr, si = lax.fori_loop(0, nchunk, body, init)
    hfr_out[...] = sr
    hfi_out[...] = si
    y_out[...] = (_dot(ub, m_ref[...])
                  + _dot_nt(hr_sc[...].astype(BF16), vr_ref[...])
                  + _dot_nt(hi_sc[...].astype(BF16), vi_ref[...])).astype(y_out.dtype)


def _s5_main(u, params, h0, nseq, nchunk):
    m, wr, wi, vr, vi, alr, ali = params
    G, R, LP = u.shape
    N = alr.shape[-1]
    has_state = h0 is not None
    g3 = lambda g: (g, 0, 0)
    big = pl.BlockSpec((None, LP, N), g3)
    vec = pl.BlockSpec((None, 1, N), g3)
    st = pl.BlockSpec((None, nseq, N), g3)
    in_specs = [pl.BlockSpec((None, R, LP), g3), pl.BlockSpec((None, LP, LP), g3),
                big, big, big, big, vec, vec]
    args = [u, m, wr, wi, vr, vi, alr, ali]
    if has_state:
        in_specs += [st, st]
        args += list(h0)
    return pl.pallas_call(
        functools.partial(_s5_main_kernel, nseq=nseq, nchunk=nchunk, has_state=has_state),
        grid=(G,),
        in_specs=in_specs,
        out_specs=[pl.BlockSpec((None, R, LP), g3), st, st],
        out_shape=[jax.ShapeDtypeStruct((G, R, LP), BF16),
                   jax.ShapeDtypeStruct((G, nseq, N), F32),
                   jax.ShapeDtypeStruct((G, nseq, N), F32)],
        scratch_shapes=[pltpu.VMEM((R, N), F32)] * 4,
        compiler_params=_cp(("parallel",)),
        name="s5_main",
    )(*args)


def _ab_out_kernel(*refs, starts):
    i = pl.program_id(0)
    parts = []
    for st in starts:
        parts.append(_parts_load(refs[:len(st)], st, i))
        refs = refs[len(st):]
    x, attn, y = parts
    u_ref, d_ref, wglu_ref, bglu_ref, wo_ref, o_ref = refs
    y = y.astype(F32) + d_ref[...] * u_ref[...]
    z = y * (0.5 * (1.0 + jnp.tanh(math.sqrt(2.0 / math.pi) * (y + 0.044715 * (y * y * y)))))
    gate = 1.0 / (1.0 + jnp.exp(-(_dot(z.astype(BF16), wglu_ref[...]) + bglu_ref[...])))
    ssm = (z * gate).astype(BF16)
    na = attn.shape[1]
    o_ref[...] = x + _dot(attn, wo_ref[:na, :]) + _dot(ssm, wo_ref[na:, :])


def _ab_out(xs, attns, ys, u, d, wglu, bglu, wo, li, tm):
    D = xs[0].shape[1]
    T = sum(x.shape[0] for x in xs)
    row = lambda i: (i, 0)
    fix = lambda i: (0, 0)
    lay = lambda i: (li, 0, 0)
    specs, starts = [], []
    for parts in (xs, attns, ys):
        sp, st = _parts_specs([p.shape[0] for p in parts], tm, parts[0].shape[1])
        specs += sp
        starts.append(st)
    return pl.pallas_call(
        functools.partial(_ab_out_kernel, starts=tuple(starts)),
        grid=(T // tm,),
        in_specs=specs + [pl.BlockSpec((tm, u.shape[1]), row), pl.BlockSpec(d.shape, fix),
                          pl.BlockSpec((None,) + wglu.shape[1:], lay), pl.BlockSpec(bglu.shape, fix),
                          pl.BlockSpec((None,) + wo.shape[1:], lay)],
        out_specs=pl.BlockSpec((tm, D), row),
        out_shape=jax.ShapeDtypeStruct((T, D), F32),
        compiler_params=_cp(("parallel",)),
        name="ab_out",
    )(*xs, *attns, *ys, u, d, wglu, bglu, wo)


def _mlp_kernel(x_ref, g_ref, wup_ref, wdn_ref, gf_ref, *rest, final_norm, o_starts):
    o_refs = rest[:len(o_starts)]
    xn_sc, acc_sc = rest[len(o_starts):]
    f = pl.program_id(1)

    @pl.when(f == 0)
    def _():
        xn_sc[...] = _rms(x_ref[...], g_ref[...]).astype(BF16)
        acc_sc[...] = jnp.zeros_like(acc_sc)

    a = jnp.maximum(_dot(xn_sc[...], wup_ref[...]), 0.0)
    acc_sc[...] += _dot((a * a).astype(BF16), wdn_ref[...])

    @pl.when(f == pl.num_programs(1) - 1)
    def _():
        r = x_ref[...] + acc_sc[...]
        if final_norm:
            r = _rms(r, gf_ref[...])
        _parts_store(o_refs, o_starts, pl.program_id(0), r)


def _mlp(x, g, wup, wdn, li, gf, final_norm, out_rows, tm, tf):
    T, D = x.shape
    FF = wup.shape[2]
    assert sum(out_rows) == T
    o_specs, o_starts = _parts_specs(out_rows, tm, D)
    return pl.pallas_call(
        functools.partial(_mlp_kernel, final_norm=final_norm, o_starts=o_starts),
        grid=(T // tm, FF // tf),
        in_specs=[pl.BlockSpec((tm, D), lambda i, f: (i, 0)), pl.BlockSpec((1, D), lambda i, f: (0, 0)),
                  pl.BlockSpec((None, D, tf), lambda i, f: (li, 0, f)),
                  pl.BlockSpec((None, tf, D), lambda i, f: (li, f, 0)),
                  pl.BlockSpec((1, D), lambda i, f: (0, 0))],
        out_specs=o_specs,
        out_shape=[jax.ShapeDtypeStruct((r, D), F32) for r in out_rows],
        scratch_shapes=[pltpu.VMEM((tm, D), BF16), pltpu.VMEM((tm, D), F32)],
        compiler_params=_cp(("arbitrary" if len(out_rows) > 1 else "parallel", "arbitrary")),
        name="mlp",
    )(x, g, wup, wdn, gf)


def _ret_log_decay(head):
    hf = jnp.full((1, 1), head, jnp.int32).astype(F32)
    return jnp.log(1.0 - jnp.exp2(-5.0 - hf))


def _c_in_kernel(x_ref, g_ref, w_ref, cos_ref, sin_ref, o_ref, xn_sc, *, n_rot, n_q, n_lin, dk,
                 chunk_split_row, chunk_lens):
    i = pl.program_id(0)
    j = pl.program_id(1)

    @pl.when(j == 0)
    def _():
        xn_sc[...] = _rms(x_ref[...], g_ref[...]).astype(BF16)

    tm = x_ref.shape[0]
    tn = w_ref.shape[1]
    half = dk // 2

    def chunks():
        for s in range(0, tn, dk):
            yield s, _dot(xn_sc[...], w_ref[:, s:s + dk])

    @pl.when(j < n_rot)
    def _():
        cos = cos_ref[...]
        sin = sin_ref[...]
        c_len = jnp.where(i * tm >= chunk_split_row, chunk_lens[1], chunk_lens[0])
        pos = (lax.broadcasted_iota(jnp.int32, (tm, 1), 0) & (c_len - 1)).astype(F32) + 1.0
        sign = jnp.where(j < n_q, 1.0, -1.0)
        scale = jnp.where(j < n_q, 1.0, dk ** -0.5)
        for s, acc in chunks():
            lg = _ret_log_decay(((j * tn + s) // dk) % RET_HEADS)
            dec = jnp.exp((sign * lg) * pos) * scale
            x1 = acc[:, :half]
            x2 = acc[:, half:]
            o_ref[:, s:s + half] = ((x1 * cos - x2 * sin) * dec).astype(BF16)
            o_ref[:, s + half:s + dk] = ((x1 * sin + x2 * cos) * dec).astype(BF16)

    @pl.when((j >= n_rot) & (j < n_lin))
    def _():
        for s, acc in chunks():
            o_ref[:, s:s + dk] = acc.astype(BF16)

    @pl.when(j >= n_lin)
    def _():
        for s, acc in chunks():
            o_ref[:, s:s + dk] = (acc / (1.0 + jnp.exp(-acc))).astype(BF16)


def _c_in(x, g, w, li, cos, sin, dk, dv, tm, tn, chunk_split_row, chunk_lens):
    T, D = x.shape
    N = w.shape[2]
    qk = RET_HEADS * dk
    n_q = qk // tn
    n_rot = 2 * qk // tn
    n_lin = (2 * qk + RET_HEADS * dv) // tn
    assert chunk_split_row % tm == 0
    assert all(tm % c == 0 and c & (c - 1) == 0 for c in chunk_lens)
    return pl.pallas_call(
        functools.partial(_c_in_kernel, n_rot=n_rot, n_q=n_q, n_lin=n_lin, dk=dk,
                          chunk_split_row=chunk_split_row, chunk_lens=chunk_lens),
        grid=(pl.cdiv(T, tm), N // tn),
        in_specs=[pl.BlockSpec((tm, D), lambda i, j: (i, 0)), pl.BlockSpec((1, D), lambda i, j: (0, 0)),
                  pl.BlockSpec((None, D, tn), lambda i, j: (li, 0, j)),
                  pl.BlockSpec((tm, dk // 2), lambda i, j: (i, 0)),
                  pl.BlockSpec((tm, dk // 2), lambda i, j: (i, 0))],
        out_specs=pl.BlockSpec((tm, tn), lambda i, j: (i, j)),
        out_shape=jax.ShapeDtypeStruct((T, N), BF16),
        scratch_shapes=[pltpu.VMEM((tm, D), BF16)],
        compiler_params=_cp(("parallel", "arbitrary")),
        name="c_in",
    )(x, g, w, cos, sin)


def _ret_kernel(*refs, c_len, has_state, hpb, dk, dv):
    if has_state:
        q_ref, k_ref, v_ref, g_ref, gn_ref, s0_ref, o_ref, sf_ref, st_sc = refs
    else:
        q_ref, k_ref, v_ref, g_ref, gn_ref, o_ref, sf_ref, st_sc = refs
    hb = pl.program_id(1)
    c = pl.program_id(2)

    @pl.when(c == 0)
    def _():
        if has_state:
            st_sc[...] = s0_ref[...]
        else:
            st_sc[...] = jnp.zeros_like(st_sc)

    row = lax.broadcasted_iota(jnp.int32, (c_len, c_len), 0)
    col = lax.broadcasted_iota(jnp.int32, (c_len, c_len), 1)
    causal = row >= col
    for hh in range(hpb):
        q = q_ref[:, hh * dk:(hh + 1) * dk]
        k = k_ref[:, hh * dk:(hh + 1) * dk]
        v = v_ref[:, hh * dv:(hh + 1) * dv]
        inner = _dot(jnp.where(causal, _dot_nt(q, k), 0.0).astype(BF16), v)
        st = st_sc[hh]
        o = inner + _dot(q, st.astype(BF16))
        g_c = jnp.exp(_ret_log_decay(hb * hpb + hh) * float(c_len))
        st_sc[hh] = g_c * (st + _dot_tn(k, v))
        mu = jnp.mean(o, axis=-1, keepdims=True)
        d = o - mu
        var = jnp.mean(d * d, axis=-1, keepdims=True)
        on = d * lax.rsqrt(var + GN_EPS) * gn_ref[:, hh * dv:(hh + 1) * dv]
        o_ref[:, hh * dv:(hh + 1) * dv] = (on * g_ref[:, hh * dv:(hh + 1) * dv].astype(F32)).astype(BF16)

    @pl.when(c == pl.num_programs(2) - 1)
    def _():
        sf_ref[...] = st_sc[...]


def _retention(proj, gn, state0, o_prev, nb, s, c_len, row0, dk, dv, hpb):
    T = proj.shape[0]
    H = RET_HEADS
    nc = s // c_len
    rb0 = row0 // c_len
    nhb = H // hpb
    has_state = state0 is not None
    rowblk = lambda b, h, c: rb0 + b * nc + c
    wk, wv = hpb * dk, hpb * dv
    kb = H * dk // wk
    vb = 2 * H * dk // wv
    in_specs = [pl.BlockSpec((c_len, wk), lambda b, h, c: (rowblk(b, h, c), h)),
                pl.BlockSpec((c_len, wk), lambda b, h, c: (rowblk(b, h, c), kb + h)),
                pl.BlockSpec((c_len, wv), lambda b, h, c: (rowblk(b, h, c), vb + h)),
                pl.BlockSpec((c_len, wv), lambda b, h, c: (rowblk(b, h, c), vb + nhb + h)),
                pl.BlockSpec((1, wv), lambda b, h, c: (0, h))]
    args = [proj, proj, proj, proj, gn]
    if has_state:
        in_specs.append(pl.BlockSpec((None, hpb, dk, dv), lambda b, h, c: (b, h, 0, 0)))
        args.append(state0)
    aliases = {}
    if o_prev is not None:
        in_specs.append(pl.BlockSpec(memory_space=pl.ANY))
        args.append(o_prev)
        aliases = {len(args) - 1: 0}

    def kern(*refs):
        if o_prev is not None:
            n_in = len(args)
            refs = refs[:n_in - 1] + refs[n_in:]
        _ret_kernel(*refs, c_len=c_len, has_state=has_state, hpb=hpb, dk=dk, dv=dv)

    return pl.pallas_call(
        kern,
        grid=(nb, nhb, nc),
        in_specs=in_specs,
        out_specs=[pl.BlockSpec((c_len, wv), lambda b, h, c: (rowblk(b, h, c), h)),
                   pl.BlockSpec((None, hpb, dk, dv), lambda b, h, c: (b, h, 0, 0))],
        out_shape=[jax.ShapeDtypeStruct((T, H * dv), BF16),
                   jax.ShapeDtypeStruct((nb, H, dk, dv), F32)],
        scratch_shapes=[pltpu.VMEM((hpb, dk, dv), F32)],
        input_output_aliases=aliases,
        compiler_params=_cp(("parallel", "parallel", "arbitrary")),
        name="retention",
    )(*args)


def _c_out_kernel(x_ref, o_ref, w_ref, y_ref):
    y_ref[...] = x_ref[...] + _dot(o_ref[...], w_ref[...])


def _c_out(x, o, w, li, tm, tn):
    T, D = x.shape
    K = o.shape[1]
    return pl.pallas_call(
        _c_out_kernel,
        grid=(pl.cdiv(T, tm), D // tn),
        in_specs=[pl.BlockSpec((tm, tn), lambda i, j: (i, j)), pl.BlockSpec((tm, K), lambda i, j: (i, 0)),
                  pl.BlockSpec((None, K, tn), lambda i, j: (li, 0, j))],
        out_specs=pl.BlockSpec((tm, tn), lambda i, j: (i, j)),
        out_shape=jax.ShapeDtypeStruct((T, D), F32),
        compiler_params=_cp(("parallel", "arbitrary")),
        name="c_out",
    )(x, o, w)


def _rope_tables(pos, half):
    inv = ROPE_THETA ** (-jnp.arange(half, dtype=F32) / half)
    ang = pos.astype(F32)[:, None] * inv[None, :]
    return jnp.cos(ang), jnp.sin(ang)


def _swap_halves(w):
    half = w.shape[-1] // 2
    return jnp.concatenate([w[..., half:], w[..., :half]], axis=-1)


def _prep_w_in_ab(w):
    o1 = Q_LORA + KV_LORA
    o2 = o1 + MLA_ROPE
    kr = w[:, o1:o2]
    ks = _swap_halves(kr)
    return jnp.concatenate([w[:, :o1], kr, ks, ks, kr, w[:, o2:]], axis=1).astype(BF16)


def _prep_w_q_b(w):
    w = w.reshape(Q_LORA, MLA_HEADS, MLA_NOPE + MLA_ROPE)
    rope = w[..., MLA_NOPE:]
    swap = _swap_halves(rope)
    out = jnp.concatenate([w[..., :MLA_NOPE], rope, swap, swap, rope], axis=-1)
    return out.reshape(Q_LORA, -1).astype(BF16)


def _prep_w_kv_b(w):
    w = w.reshape(KV_LORA, MLA_HEADS, MLA_NOPE + MLA_V)
    return jnp.concatenate([w[..., :MLA_NOPE].reshape(KV_LORA, -1),
                            w[..., MLA_NOPE:].reshape(KV_LORA, -1)], axis=1).astype(BF16)


def kernel(x_prompt, x_sample, cache_mla_ckv, cache_mla_krope, state_s5_re, state_s5_im, state_ret,
           norm_mix, norm_mlp, norm_final, w_in_ab, q_a_norm, kv_a_norm, w_q_b, w_kv_b,
           s5_lam_re, s5_lam_im, s5_log_dt, s5_b_re, s5_b_im, s5_c_re, s5_c_im, s5_d, w_glu, b_glu,
           w_out_ab, w_in_c, ret_gn, w_out_c, w_up, w_down):
    BP, SP, D = x_prompt.shape
    BS, SS, _ = x_sample.shape
    past = cache_mla_ckv.shape[2]
    depth = norm_mix.shape[0]
    G, N = s5_lam_re.shape[1:]
    P, L = S5_GROUP, S5_L
    dk, dv = state_ret.shape[3:]
    TP, TS = BP * SP, BS * SS
    T = TP + TS
    TM = 512

    x = (x_prompt.reshape(TP, D), x_sample.reshape(TS, D))
    pos = jnp.concatenate([jnp.tile(jnp.arange(SP), BP), jnp.tile(past + jnp.arange(SS), BS)])

    c32, s32 = _rope_tables(pos, MLA_ROPE // 2)
    z64 = jnp.zeros((T, LANE - MLA_ROPE), F32)
    cs1 = jnp.concatenate([c32, c32, z64], axis=1)
    cs2 = jnp.concatenate([-s32, s32, z64], axis=1)
    cos_r, sin_r = _rope_tables(pos, dk // 2)

    skp = -(-(past + SS) // LANE) * LANE
    w_glu_b, w_out_ab_b, w_in_c_b, w_out_c_b, w_up_b, w_down_b = (
        w.astype(BF16) for w in (w_glu, w_out_ab, w_in_c, w_out_c, w_up, w_down))
    ckv_p, kr_p, s5r_p, s5i_p, ret_p = [], [], [], [], []
    ckv_s, kr_s, s5r_s, s5i_s, ret_s = [], [], [], [], []

    for layer in range(depth):
        i = layer // 2
        g_mix = norm_mix[layer].reshape(1, D)
        if layer % 2 == 0:
            xs = x if isinstance(x, tuple) else (x,)
            q, ckv, kr, u = _ab_in(xs, g_mix, _prep_w_in_ab(w_in_ab[i]), q_a_norm[i].reshape(1, -1),
                                   kv_a_norm[i].reshape(1, -1), _prep_w_q_b(w_q_b[i]), cs1, cs2, 256)
            ckv_p.append(ckv[:TP].reshape(BP, SP, KV_LORA))
            kr_p.append(kr[:TP, :MLA_ROPE].reshape(BP, SP, MLA_ROPE))
            ckv_s.append(ckv[TP:].reshape(BS, SS, KV_LORA))
            kr_s.append(kr[TP:, :MLA_ROPE].reshape(BS, SS, MLA_ROPE))
            pad = skp - past - SS
            ckv_hist = jnp.concatenate([cache_mla_ckv[i], ckv[TP:].reshape(BS, SS, KV_LORA),
                                        jnp.zeros((BS, pad, KV_LORA), F32)], axis=1)
            kr_past = jnp.pad(cache_mla_krope[i], ((0, 0), (0, 0), (0, LANE - MLA_ROPE)))
            kr_hist = jnp.concatenate([kr_past, kr[TP:].reshape(BS, SS, LANE),
                                       jnp.zeros((BS, pad, LANE), F32)], axis=1)
            w_kv = _prep_w_kv_b(w_kv_b[i])
            k_p, v_p = _kv_proj(ckv, kr, w_kv, 512)
            k_s, v_s = _kv_proj(ckv_hist.reshape(BS * skp, KV_LORA), kr_hist.reshape(BS * skp, LANE),
                                w_kv, skp)
            attn = (_attn_long(q, k_p, v_p, BP, SP, ATT_TK),
                    _attn_short(q, k_s, v_s, BS, SS, skp, TP, 0, past, past + SS))
            params = _s5_params(s5_lam_re[i], s5_lam_im[i], s5_log_dt[i], s5_b_re[i], s5_b_im[i],
                                s5_c_re[i], s5_c_im[i])
            ncp, ncs = SP // L, SS // L
            ub = u.astype(BF16)
            u_p = ub[:TP].reshape(BP, ncp, L, G, P).transpose(3, 1, 0, 2, 4).reshape(G, ncp * BP, L * P)
            u_s = ub[TP:].reshape(BS, ncs, L, G, P).transpose(3, 1, 0, 2, 4).reshape(G, ncs * BS, L * P)
            y_p, hr_p, hi_p = _s5_main(u_p, params, None, BP, ncp)
            h0 = (jnp.swapaxes(state_s5_re[i], 0, 1), jnp.swapaxes(state_s5_im[i], 0, 1))
            y_s, hr_s, hi_s = _s5_main(u_s, params, h0, BS, ncs)
            y_p = y_p.reshape(G, ncp, BP, L, P).transpose(2, 1, 3, 0, 4).reshape(TP, G * P)
            y_s = y_s.reshape(G, ncs, BS, L, P).transpose(2, 1, 3, 0, 4).reshape(TS, G * P)
            s5r_p.append(jnp.swapaxes(hr_p, 0, 1))
            s5i_p.append(jnp.swapaxes(hi_p, 0, 1))
            s5r_s.append(jnp.swapaxes(hr_s, 0, 1))
            s5i_s.append(jnp.swapaxes(hi_s, 0, 1))
            x = _ab_out(xs, attn, (y_p, y_s), u, s5_d[i].reshape(1, G * P), w_glu_b,
                        b_glu[i].reshape(1, -1), w_out_ab_b, i, 256)
        else:
            proj = _c_in(x, g_mix, w_in_c_b, i, cos_r, sin_r, dk, dv, 2 * TM, 1024, TP, (RET_C, SS))
            gn = ret_gn[i].reshape(1, -1)
            o, st_p = _retention(proj, gn, None, None, BP, SP, RET_C, 0, dk, dv, RET_HPB)
            o, st_s = _retention(proj, gn, state_ret[i], o, BS, SS, SS, TP, dk, dv, RET_HPB)
            ret_p.append(st_p)
            ret_s.append(st_s)
            x = _c_out(x, o, w_out_c_b, i, 2 * TM, 512)
        last = layer == depth - 1
        outs = _mlp(x, norm_mlp[layer].reshape(1, D), w_up_b, w_down_b, layer,
                    norm_final.reshape(1, D), last, (TP, TS) if last else (T,), TM, MLP_TF)
        x = outs[0]

    return (outs[0].reshape(BP, SP, D), outs[1].reshape(BS, SS, D),
            jnp.stack(ckv_p), jnp.stack(kr_p), jnp.stack(s5r_p), jnp.stack(s5i_p), jnp.stack(ret_p),
            jnp.stack(ckv_s), jnp.stack(kr_s), jnp.stack(s5r_s), jnp.stack(s5i_s), jnp.stack(ret_s))
```

```python
import functools
import math

import jax
import jax.numpy as jnp
import numpy as np
from jax import lax
from jax.experimental import pallas as pl
from jax.experimental.pallas import tpu as pltpu

F32 = jnp.float32
BF16 = jnp.bfloat16

CHUNK = 64
MLA_V = 128
MLA_NOPE = 128
MLA_ROPE = 64
MLA_HEADS = 8
Q_LORA = 512
KV_LORA = 256
MLA_SCALE = (MLA_NOPE + MLA_ROPE) ** -0.5
S5_GROUP = 16
S5_STATE = 64
RET_HEADS = 8
ROPE_THETA = 10000.0
EPS = 1e-6
GN_EPS = 1e-5
LOG2E = 1.4426950408889634

LANE = 128
QK_PAD = 256
V_PAD = 256
S5_L = 64
RET_C = 256
RET_HPB = 8
MLP_TF = 1024
ATT_TK = 512
NEG_BIG = -1e30
VMEM_LIMIT = 56 * 2 ** 20


def _cp(sem):
    return pltpu.CompilerParams(dimension_semantics=sem, vmem_limit_bytes=VMEM_LIMIT)


def _rms(x, g):
    return x * lax.rsqrt(jnp.mean(x * x, axis=-1, keepdims=True) + EPS) * g


def _dot(a, b):
    return jnp.dot(a, b, preferred_element_type=F32)


def _dot_nt(a, b):
    return lax.dot_general(a, b, (((1,), (1,)), ((), ())), preferred_element_type=F32)


def _dot_tn(a, b):
    return lax.dot_general(a, b, (((0,), (0,)), ((), ())), preferred_element_type=F32)


def _parts_specs(rows, tm, width, row_axis=0):
    assert all(r % tm == 0 for r in rows[:-1])
    specs, starts, start = [], [], 0
    for r in rows:
        nblk = pl.cdiv(r, tm)

        def index_map(*g, start=start, nblk=nblk):
            return (jnp.clip(g[row_axis] - start, 0, nblk - 1), 0)

        specs.append(pl.BlockSpec((tm, width), index_map))
        starts.append(start)
        start += nblk
    return specs, tuple(starts)


def _parts_load(refs, starts, i):
    x = refs[0][...]
    for ref, s in zip(refs[1:], starts[1:]):
        x = jnp.where(i >= s, ref[...], x)
    return x


def _parts_store(refs, starts, i, val):
    ends = starts[1:] + (None,)
    for ref, s, e in zip(refs, starts, ends):
        cond = i >= s if e is None else (i >= s) & (i < e)

        @pl.when(cond)
        def _(ref=ref):
            ref[...] = val


def _ab_in_kernel(*refs, x_starts):
    n = len(x_starts)
    x_refs = refs[:n]
    (g_ref, win_ref, qn_ref, kvn_ref, wq_ref, cs1_ref, cs2_ref,
     q_out, ckv_out, kr_out, u_out) = refs[n:]
    x = _parts_load(x_refs, x_starts, pl.program_id(0))
    xn = _rms(x, g_ref[...]).astype(BF16)
    proj = _dot(xn, win_ref[...])
    cqn = _rms(proj[:, :Q_LORA], qn_ref[...]).astype(BF16)
    o = Q_LORA
    ckv_out[...] = _rms(proj[:, o:o + KV_LORA], kvn_ref[...])
    o += KV_LORA
    cs1 = cs1_ref[...]
    cs2 = cs2_ref[...]
    kr_out[...] = proj[:, o:o + LANE] * cs1 + proj[:, o + LANE:o + 2 * LANE] * cs2
    o += 2 * LANE
    u_out[...] = proj[:, o:]
    q = _dot(cqn, wq_ref[...])
    qs = MLA_SCALE * LOG2E
    for h in range(MLA_HEADS):
        b = h * 3 * LANE
        q_out[:, h * QK_PAD:h * QK_PAD + LANE] = (q[:, b:b + LANE] * qs).astype(BF16)
        rope = q[:, b + LANE:b + 2 * LANE] * cs1 + q[:, b + 2 * LANE:b + 3 * LANE] * cs2
        q_out[:, h * QK_PAD + LANE:(h + 1) * QK_PAD] = (rope * qs).astype(BF16)


def _pos_table_spec(tm, width, long_rows, seq_len, axis=0):
    assert seq_len % tm == 0 and long_rows % tm == 0
    n_long, per_seq = long_rows // tm, seq_len // tm

    def index_map(*g):
        return (jnp.where(g[axis] < n_long, g[axis] % per_seq, per_seq), 0)

    return pl.BlockSpec((tm, width), index_map)


def _ab_in(xs, g, win, qn, kvn, wq, cs1, cs2, tm, long_rows, seq_len):
    D = xs[0].shape[1]
    T = sum(x.shape[0] for x in xs)
    row = lambda i: (i, 0)
    fix = lambda i: (0, 0)
    s5w = win.shape[1] - Q_LORA - KV_LORA - 2 * LANE
    x_specs, x_starts = _parts_specs([x.shape[0] for x in xs], tm, D)
    return pl.pallas_call(
        functools.partial(_ab_in_kernel, x_starts=x_starts),
        grid=(T // tm,),
        in_specs=x_specs + [pl.BlockSpec((1, D), fix),
                  pl.BlockSpec(win.shape, fix), pl.BlockSpec((1, Q_LORA), fix),
                  pl.BlockSpec((1, KV_LORA), fix), pl.BlockSpec(wq.shape, fix),
                  _pos_table_spec(tm, LANE, long_rows, seq_len), _pos_table_spec(tm, LANE, long_rows, seq_len)],
        out_specs=[pl.BlockSpec((tm, MLA_HEADS * QK_PAD), row), pl.BlockSpec((tm, KV_LORA), row),
                   pl.BlockSpec((tm, LANE), row), pl.BlockSpec((tm, s5w), row)],
        out_shape=[jax.ShapeDtypeStruct((T, MLA_HEADS * QK_PAD), BF16),
                   jax.ShapeDtypeStruct((T, KV_LORA), F32),
                   jax.ShapeDtypeStruct((T, LANE), F32),
                   jax.ShapeDtypeStruct((T, s5w), F32)],
        compiler_params=_cp(("parallel",)),
        name="ab_in",
    )(*xs, g, win, qn, kvn, wq, cs1, cs2)


def _kv_kernel(ckv_ref, kr_ref, w_ref, k_out, v_out):
    kv = _dot(ckv_ref[...].astype(BF16), w_ref[...])
    kr = kr_ref[...].astype(BF16)
    v0 = MLA_HEADS * MLA_NOPE
    ones = jnp.ones((kv.shape[0], V_PAD - MLA_V), BF16)
    for h in range(MLA_HEADS):
        k_out[:, h * QK_PAD:h * QK_PAD + LANE] = kv[:, h * LANE:(h + 1) * LANE].astype(BF16)
        k_out[:, h * QK_PAD + LANE:(h + 1) * QK_PAD] = kr
        v_out[:, h * V_PAD:h * V_PAD + MLA_V] = kv[:, v0 + h * MLA_V:v0 + (h + 1) * MLA_V].astype(BF16)
        v_out[:, h * V_PAD + MLA_V:(h + 1) * V_PAD] = ones


def _kv_proj(ckv, kr, w, tm):
    R = ckv.shape[0]
    row = lambda i: (i, 0)
    return pl.pallas_call(
        _kv_kernel,
        grid=(R // tm,),
        in_specs=[pl.BlockSpec((tm, KV_LORA), row), pl.BlockSpec((tm, LANE), row),
                  pl.BlockSpec(w.shape, lambda i: (0, 0))],
        out_specs=[pl.BlockSpec((tm, MLA_HEADS * QK_PAD), row),
                   pl.BlockSpec((tm, MLA_HEADS * V_PAD), row)],
        out_shape=[jax.ShapeDtypeStruct((R, MLA_HEADS * QK_PAD), BF16),
                   jax.ShapeDtypeStruct((R, MLA_HEADS * V_PAD), BF16)],
        compiler_params=_cp(("parallel",)),
        name="kv_proj",
    )(ckv, kr, w)


def _attn_long_kernel(q_ref, k_ref, v_ref, o_ref, s_sc, m_sc, acc_sc, *, tq, tk):
    qi = pl.program_id(2)

    def scores_to(slot, kt, r0=0):
        s_sc[slot, r0:, :] = _dot_nt(q_ref[r0:, :], k_ref[pl.ds(pl.multiple_of(kt * tk, tk), tk), :])

    def consume(slot, kt, masked, r0=0):
        s = s_sc[slot, r0:, :]
        if masked:
            qpos = qi * tq + r0 + lax.broadcasted_iota(jnp.int32, (tq - r0, tk), 0)
            kpos = kt * tk + lax.broadcasted_iota(jnp.int32, (tq - r0, tk), 1)
            s = jnp.where(kpos // CHUNK <= qpos // CHUNK, s, NEG_BIG)
        m = m_sc[r0:, :]
        m_new = jnp.maximum(m, jnp.max(s, axis=1, keepdims=True))
        alpha = jnp.exp2(m - m_new)
        p = jnp.exp2(s - jnp.concatenate([m_new] * (tk // LANE), axis=1))
        m_sc[r0:, :] = m_new
        v = v_ref[pl.ds(pl.multiple_of(kt * tk, tk), tk), :]
        acc_sc[r0:, :] = (jnp.concatenate([alpha] * (V_PAD // LANE), axis=1) * acc_sc[r0:, :]
                          + _dot(p.astype(BF16), v))

    m_sc[...] = jnp.full_like(m_sc, NEG_BIG)
    acc_sc[...] = jnp.zeros_like(acc_sc)
    scores_to(0, 0)

    def pair(j, carry):
        scores_to(1, 2 * j + 1)
        consume(0, 2 * j, False)
        scores_to(0, 2 * j + 2)
        consume(1, 2 * j + 1, False)
        return carry

    lax.fori_loop(0, qi, pair, 0)
    scores_to(1, 2 * qi + 1, tk)
    consume(0, 2 * qi, True)
    consume(1, 2 * qi + 1, True, tk)
    acc = acc_sc[...]
    o_ref[...] = (acc[:, :MLA_V] / acc[:, MLA_V:]).astype(BF16)


def _attn_long(q, k, v, nb, s, tk):
    tq = 2 * tk
    assert s % tq == 0 and V_PAD == 2 * MLA_V
    nq = s // tq
    return pl.pallas_call(
        functools.partial(_attn_long_kernel, tq=tq, tk=tk),
        grid=(nb, MLA_HEADS, nq),
        in_specs=[pl.BlockSpec((tq, QK_PAD), lambda b, h, i: (b * nq + i, h)),
                  pl.BlockSpec((s, QK_PAD), lambda b, h, i: (b, h)),
                  pl.BlockSpec((s, V_PAD), lambda b, h, i: (b, h))],
        out_specs=pl.BlockSpec((tq, MLA_V), lambda b, h, i: (b * nq + i, h)),
        out_shape=jax.ShapeDtypeStruct((nb * s, MLA_HEADS * MLA_V), BF16),
        scratch_shapes=[pltpu.VMEM((2, tq, tk), F32), pltpu.VMEM((tq, LANE), F32),
                        pltpu.VMEM((tq, V_PAD), F32)],
        compiler_params=_cp(("parallel", "parallel", "arbitrary")),
        name="attn_long",
    )(q, k, v)


def _attn_short_kernel(q_ref, k_ref, v_ref, o_ref, *, q_pos0, n_keys):
    sq = q_ref.shape[0]
    skp = k_ref.shape[0]
    s = _dot_nt(q_ref[...], k_ref[...])
    qpos = q_pos0 + lax.broadcasted_iota(jnp.int32, (sq, skp), 0)
    kpos = lax.broadcasted_iota(jnp.int32, (sq, skp), 1)
    ok = (kpos // CHUNK <= qpos // CHUNK) & (kpos < n_keys)
    s = jnp.where(ok, s, NEG_BIG)
    m = jnp.max(s, axis=1, keepdims=True)
    p = jnp.exp2(s - m)
    acc = _dot(p.astype(BF16), v_ref[...])
    o_ref[...] = (acc[:, :MLA_V] / acc[:, MLA_V:]).astype(BF16)


def _attn_short(q, k, v, nb, sq, skp, q_row0, k_row0, q_pos0, n_keys):
    qb0 = q_row0 // sq
    kb0 = k_row0 // skp
    return pl.pallas_call(
        functools.partial(_attn_short_kernel, q_pos0=q_pos0, n_keys=n_keys),
        grid=(nb, MLA_HEADS),
        in_specs=[pl.BlockSpec((sq, QK_PAD), lambda b, h: (qb0 + b, h)),
                  pl.BlockSpec((skp, QK_PAD), lambda b, h: (kb0 + b, h)),
                  pl.BlockSpec((skp, V_PAD), lambda b, h: (kb0 + b, h))],
        out_specs=pl.BlockSpec((sq, MLA_V), lambda b, h: (b, h)),
        out_shape=jax.ShapeDtypeStruct((nb * sq, MLA_HEADS * MLA_V), BF16),
        compiler_params=_cp(("parallel", "parallel")),
        name="attn_short",
    )(q, k, v)


def _s5_param_kernel(lr_ref, li_ref, ldt_ref, btr_ref, bti_ref, cr_ref, ci_ref,
                     m_out, wr_out, wi_out, vr_out, vi_out, alr_out, ali_out,
                     car_sc, cai_sc):
    L, P = S5_L, S5_GROUP
    lr = jnp.minimum(lr_ref[...], -1e-4)
    li = li_ref[...]
    dt = jnp.exp(ldt_ref[...])
    mag = jnp.exp(lr * dt)
    a_re = mag * jnp.cos(li * dt)
    a_im = mag * jnp.sin(li * dt)
    den = lr * lr + li * li
    f_re = ((a_re - 1.0) * lr + a_im * li) / den
    f_im = (a_im * lr - (a_re - 1.0) * li) / den
    btr = btr_ref[...]
    bti = bti_ref[...]
    bb_re = f_re * btr - f_im * bti
    bb_im = f_re * bti + f_im * btr
    cr = cr_ref[...]
    ci = ci_ref[...]

    n_pow = -(-(L + 1) // 8) * 8
    t = lax.broadcasted_iota(jnp.int32, (n_pow, 1), 0).astype(F32)
    pmag = jnp.exp((lr * dt) * t)
    pang = (li * dt) * t
    pw_re = pmag * jnp.cos(pang)
    pw_im = pmag * jnp.sin(pang)

    def apow(t):
        return pw_re[t:t + 1, :], pw_im[t:t + 1, :]

    for i in range(L):
        rows = slice(i * P, (i + 1) * P)
        pr, pi = apow(L - 1 - i)
        wr_out[rows, :] = (pr * bb_re - pi * bb_im).astype(BF16)
        wi_out[rows, :] = (pr * bb_im + pi * bb_re).astype(BF16)
        pr, pi = apow(i + 1)
        vr_out[rows, :] = (cr * pr - ci * pi).astype(BF16)
        vi_out[rows, :] = (-(cr * pi + ci * pr)).astype(BF16)
        pr, pi = apow(i)
        car_sc[rows, :] = cr * pr - ci * pi
        cai_sc[rows, :] = cr * pi + ci * pr
    pr, pi = apow(L)
    alr_out[...] = pr
    ali_out[...] = pi
    hp = lax.Precision.HIGHEST
    nt = (((1,), (1,)), ((), ()))
    krow = (lax.dot_general(bb_re, car_sc[...], nt, precision=hp, preferred_element_type=F32)
            - lax.dot_general(bb_im, cai_sc[...], nt, precision=hp, preferred_element_type=F32))
    lane = lax.broadcasted_iota(jnp.int32, (P, L * P), 1)
    for i in range(L):
        shifted = krow if i == 0 else pltpu.roll(krow, i * P, axis=1)
        m_out[i * P:(i + 1) * P, :] = jnp.where(lane >= i * P, shifted, 0.0).astype(BF16)


def _s5_params(lam_re, lam_im, log_dt, b_re, b_im, c_re, c_im):
    G, N = lam_re.shape
    P, L = S5_GROUP, S5_L
    LP = L * P
    v3 = lambda a: a.reshape(G, 1, N)
    ldt = jnp.broadcast_to(log_dt[:, None, None], (G, 1, N))
    btr = jnp.swapaxes(b_re, 1, 2)
    bti = jnp.swapaxes(b_im, 1, 2)
    vec = pl.BlockSpec((None, 1, N), lambda g: (g, 0, 0))
    mat = pl.BlockSpec((None, P, N), lambda g: (g, 0, 0))
    big = pl.BlockSpec((None, LP, N), lambda g: (g, 0, 0))
    return pl.pallas_call(
        _s5_param_kernel,
        grid=(G,),
        in_specs=[vec, vec, vec, mat, mat, mat, mat],
        out_specs=[pl.BlockSpec((None, LP, LP), lambda g: (g, 0, 0)), big, big, big, big, vec, vec],
        out_shape=[jax.ShapeDtypeStruct((G, LP, LP), BF16)]
        + [jax.ShapeDtypeStruct((G, LP, N), BF16)] * 4
        + [jax.ShapeDtypeStruct((G, 1, N), F32)] * 2,
        scratch_shapes=[pltpu.VMEM((LP, N), F32), pltpu.VMEM((LP, N), F32)],
        compiler_params=_cp(("parallel",)),
        name="s5_params",
    )(v3(lam_re), v3(lam_im), ldt, btr, bti, c_re, c_im)


def _s5_main_kernel(*refs, nseq, nchunk, has_state):
    if has_state:
        (u_ref, m_ref, wr_ref, wi_ref, vr_ref, vi_ref, alr_ref, ali_ref, h0r_ref, h0i_ref,
         y_out, hfr_out, hfi_out, xr_sc, xi_sc, hr_sc, hi_sc) = refs
    else:
        (u_ref, m_ref, wr_ref, wi_ref, vr_ref, vi_ref, alr_ref, ali_ref,
         y_out, hfr_out, hfi_out, xr_sc, xi_sc, hr_sc, hi_sc) = refs
    ub = u_ref[...]
    xr_sc[...] = _dot(ub, wr_ref[...])
    xi_sc[...] = _dot(ub, wi_ref[...])
    ar = alr_ref[...]
    ai = ali_ref[...]
    n = ar.shape[-1]
    if has_state:
        init = (h0r_ref[...], h0i_ref[...])
    else:
        init = (jnp.zeros((nseq, n), F32), jnp.zeros((nseq, n), F32))

    def body(c, carry):
        sr, si = carry
        rows = pl.ds(pl.multiple_of(c * nseq, nseq), nseq)
        hr_sc[rows, :] = sr
        hi_sc[rows, :] = si
        return (ar * sr - ai * si + xr_sc[rows, :], ar * si + ai * sr + xi_sc[rows, :])

    sr, si = lax.fori_loop(0, nchunk, body, init)
    hfr_out[...] = sr
    hfi_out[...] = si
    y_out[...] = (_dot(ub, m_ref[...])
                  + _dot_nt(hr_sc[...].astype(BF16), vr_ref[...])
                  + _dot_nt(hi_sc[...].astype(BF16), vi_ref[...])).astype(y_out.dtype)


def _s5_main(u, params, h0, nseq, nchunk):
    m, wr, wi, vr, vi, alr, ali = params
    G, R, LP = u.shape
    N = alr.shape[-1]
    has_state = h0 is not None
    g3 = lambda g: (g, 0, 0)
    big = pl.BlockSpec((None, LP, N), g3)
    vec = pl.BlockSpec((None, 1, N), g3)
    st = pl.BlockSpec((None, nseq, N), g3)
    in_specs = [pl.BlockSpec((None, R, LP), g3), pl.BlockSpec((None, LP, LP), g3),
                big, big, big, big, vec, vec]
    args = [u, m, wr, wi, vr, vi, alr, ali]
    if has_state:
        in_specs += [st, st]
        args += list(h0)
    return pl.pallas_call(
        functools.partial(_s5_main_kernel, nseq=nseq, nchunk=nchunk, has_state=has_state),
        grid=(G,),
        in_specs=in_specs,
        out_specs=[pl.BlockSpec((None, R, LP), g3), st, st],
        out_shape=[jax.ShapeDtypeStruct((G, R, LP), BF16),
                   jax.ShapeDtypeStruct((G, nseq, N), F32),
                   jax.ShapeDtypeStruct((G, nseq, N), F32)],
        scratch_shapes=[pltpu.VMEM((R, N), F32)] * 4,
        compiler_params=_cp(("parallel",)),
        name="s5_main",
    )(*args)


def _ab_out_kernel(*refs, starts):
    i = pl.program_id(0)
    parts = []
    for st in starts:
        parts.append(_parts_load(refs[:len(st)], st, i))
        refs = refs[len(st):]
    x, attn, y = parts
    u_ref, d_ref, wglu_ref, bglu_ref, wo_ref, o_ref = refs
    y = y.astype(F32) + d_ref[...] * u_ref[...]
    z = y * (0.5 * (1.0 + jnp.tanh(math.sqrt(2.0 / math.pi) * (y + 0.044715 * (y * y * y)))))
    gate = 1.0 / (1.0 + jnp.exp(-(_dot(z.astype(BF16), wglu_ref[...]) + bglu_ref[...])))
    ssm = (z * gate).astype(BF16)
    na = attn.shape[1]
    o_ref[...] = x + _dot(attn, wo_ref[:na, :]) + _dot(ssm, wo_ref[na:, :])


def _ab_out(xs, attns, ys, u, d, wglu, bglu, wo, li, tm):
    D = xs[0].shape[1]
    T = sum(x.shape[0] for x in xs)
    row = lambda i: (i, 0)
    fix = lambda i: (0, 0)
    lay = lambda i: (li, 0, 0)
    specs, starts = [], []
    for parts in (xs, attns, ys):
        sp, st = _parts_specs([p.shape[0] for p in parts], tm, parts[0].shape[1])
        specs += sp
        starts.append(st)
    return pl.pallas_call(
        functools.partial(_ab_out_kernel, starts=tuple(starts)),
        grid=(T // tm,),
        in_specs=specs + [pl.BlockSpec((tm, u.shape[1]), row), pl.BlockSpec(d.shape, fix),
                          pl.BlockSpec((None,) + wglu.shape[1:], lay), pl.BlockSpec(bglu.shape, fix),
                          pl.BlockSpec((None,) + wo.shape[1:], lay)],
        out_specs=pl.BlockSpec((tm, D), row),
        out_shape=jax.ShapeDtypeStruct((T, D), F32),
        compiler_params=_cp(("parallel",)),
        name="ab_out",
    )(*xs, *attns, *ys, u, d, wglu, bglu, wo)


def _mlp_kernel(x_ref, g_ref, wup_ref, wdn_ref, gf_ref, *rest, final_norm, o_starts):
    n_out = len(o_starts)
    o_refs = rest[:n_out]
    xn_sc = rest[n_out]
    acc = rest[n_out + 1] if n_out > 1 else o_refs[0]
    f = pl.program_id(1)

    @pl.when(f == 0)
    def _():
        xn_sc[...] = _rms(x_ref[...], g_ref[...]).astype(BF16)
        acc[...] = jnp.zeros_like(acc)

    a = jnp.maximum(_dot(xn_sc[...], wup_ref[...]), 0.0)
    acc[...] += _dot((a * a).astype(BF16), wdn_ref[...])

    @pl.when(f == pl.num_programs(1) - 1)
    def _():
        r = x_ref[...] + acc[...]
        if final_norm:
            r = _rms(r, gf_ref[...])
        if n_out > 1:
            _parts_store(o_refs, o_starts, pl.program_id(0), r)
        else:
            acc[...] = r


def _mlp(x, g, wup, wdn, li, gf, final_norm, out_rows, tm, tf):
    T, D = x.shape
    FF = wup.shape[2]
    assert sum(out_rows) == T
    o_specs, o_starts = _parts_specs(out_rows, tm, D)
    return pl.pallas_call(
        functools.partial(_mlp_kernel, final_norm=final_norm, o_starts=o_starts),
        grid=(pl.cdiv(T, tm), FF // tf),
        in_specs=[pl.BlockSpec((tm, D), lambda i, f: (i, 0)), pl.BlockSpec((1, D), lambda i, f: (0, 0)),
                  pl.BlockSpec((None, D, tf), lambda i, f: (li, 0, f)),
                  pl.BlockSpec((None, tf, D), lambda i, f: (li, f, 0)),
                  pl.BlockSpec((1, D), lambda i, f: (0, 0))],
        out_specs=o_specs,
        out_shape=[jax.ShapeDtypeStruct((r, D), F32) for r in out_rows],
        scratch_shapes=[pltpu.VMEM((tm, D), BF16)] + [pltpu.VMEM((tm, D), F32)] * (len(out_rows) > 1),
        compiler_params=_cp(("arbitrary" if len(out_rows) > 1 else "parallel", "arbitrary")),
        name="mlp",
    )(x, g, wup, wdn, gf)


def _ret_log_decay(head):
    hf = jnp.full((1, 1), head, jnp.int32).astype(F32)
    return jnp.log(1.0 - jnp.exp2(-5.0 - hf))


def _c_in_kernel(x_ref, g_ref, w_ref, cos_ref, sin_ref, o_ref, xn_sc, *, n_rot, n_q, n_lin, dk,
                 chunk_split_row, chunk_lens):
    i = pl.program_id(0)
    j = pl.program_id(1)

    @pl.when(j == 0)
    def _():
        xn_sc[...] = _rms(x_ref[...], g_ref[...]).astype(BF16)

    tm = x_ref.shape[0]
    tn = w_ref.shape[1]
    half = dk // 2

    def chunks():
        for s in range(0, tn, dk):
            yield s, _dot(xn_sc[...], w_ref[:, s:s + dk])

    @pl.when(j < n_rot)
    def _():
        cos = cos_ref[...]
        sin = sin_ref[...]
        c_len = jnp.where(i * tm >= chunk_split_row, chunk_lens[1], chunk_lens[0])
        pos = (lax.broadcasted_iota(jnp.int32, (tm, 1), 0) & (c_len - 1)).astype(F32) + 1.0
        sign = jnp.where(j < n_q, 1.0, -1.0)
        scale = jnp.where(j < n_q, 1.0, dk ** -0.5)
        for s, acc in chunks():
            lg = _ret_log_decay(((j * tn + s) // dk) % RET_HEADS)
            dec = jnp.exp((sign * lg) * pos) * scale
            x1 = acc[:, :half]
            x2 = acc[:, half:]
            o_ref[:, s:s + half] = ((x1 * cos - x2 * sin) * dec).astype(BF16)
            o_ref[:, s + half:s + dk] = ((x1 * sin + x2 * cos) * dec).astype(BF16)

    @pl.when((j >= n_rot) & (j < n_lin))
    def _():
        for s, acc in chunks():
            o_ref[:, s:s + dk] = acc.astype(BF16)

    @pl.when(j >= n_lin)
    def _():
        for s, acc in chunks():
            o_ref[:, s:s + dk] = (acc / (1.0 + jnp.exp(-acc))).astype(BF16)


def _c_in(x, g, w, li, cos, sin, dk, dv, tm, tn, chunk_split_row, chunk_lens, seq_len):
    T, D = x.shape
    N = w.shape[2]
    qk = RET_HEADS * dk
    n_q = qk // tn
    n_rot = 2 * qk // tn
    n_lin = (2 * qk + RET_HEADS * dv) // tn
    assert chunk_split_row % tm == 0
    assert all(tm % c == 0 and c & (c - 1) == 0 for c in chunk_lens)
    return pl.pallas_call(
        functools.partial(_c_in_kernel, n_rot=n_rot, n_q=n_q, n_lin=n_lin, dk=dk,
                          chunk_split_row=chunk_split_row, chunk_lens=chunk_lens),
        grid=(pl.cdiv(T, tm), N // tn),
        in_specs=[pl.BlockSpec((tm, D), lambda i, j: (i, 0)), pl.BlockSpec((1, D), lambda i, j: (0, 0)),
                  pl.BlockSpec((None, D, tn), lambda i, j: (li, 0, j)),
                  _pos_table_spec(tm, dk // 2, chunk_split_row, seq_len),
                  _pos_table_spec(tm, dk // 2, chunk_split_row, seq_len)],
        out_specs=pl.BlockSpec((tm, tn), lambda i, j: (i, j)),
        out_shape=jax.ShapeDtypeStruct((T, N), BF16),
        scratch_shapes=[pltpu.VMEM((tm, D), BF16)],
        compiler_params=_cp(("parallel", "arbitrary")),
        name="c_in",
    )(x, g, w, cos, sin)


def _ret_kernel(*refs, c_len, has_state, hpb, dk, dv):
    if has_state:
        q_ref, k_ref, v_ref, g_ref, gn_ref, s0_ref, o_ref, sf_ref, st_sc = refs
    else:
        q_ref, k_ref, v_ref, g_ref, gn_ref, o_ref, sf_ref, st_sc = refs
    hb = pl.program_id(1)
    c = pl.program_id(2)

    @pl.when(c == 0)
    def _():
        if has_state:
            st_sc[...] = s0_ref[...]
        else:
            st_sc[...] = jnp.zeros_like(st_sc)

    row = lax.broadcasted_iota(jnp.int32, (c_len, c_len), 0)
    col = lax.broadcasted_iota(jnp.int32, (c_len, c_len), 1)
    causal = row >= col
    for hh in range(hpb):
        q = q_ref[:, hh * dk:(hh + 1) * dk]
        k = k_ref[:, hh * dk:(hh + 1) * dk]
        v = v_ref[:, hh * dv:(hh + 1) * dv]
        inner = _dot(jnp.where(causal, _dot_nt(q, k), 0.0).astype(BF16), v)
        st = st_sc[hh]
        o = inner + _dot(q, st.astype(BF16))
        g_c = jnp.exp(_ret_log_decay(hb * hpb + hh) * float(c_len))
        st_sc[hh] = g_c * (st + _dot_tn(k, v))
        mu = jnp.mean(o, axis=-1, keepdims=True)
        d = o - mu
        var = jnp.mean(d * d, axis=-1, keepdims=True)
        on = d * lax.rsqrt(var + GN_EPS) * gn_ref[:, hh * dv:(hh + 1) * dv]
        o_ref[:, hh * dv:(hh + 1) * dv] = (on * g_ref[:, hh * dv:(hh + 1) * dv].astype(F32)).astype(BF16)

    @pl.when(c == pl.num_programs(2) - 1)
    def _():
        sf_ref[...] = st_sc[...]


def _retention(proj, gn, state0, o_prev, nb, s, c_len, row0, dk, dv, hpb):
    T = proj.shape[0]
    H = RET_HEADS
    nc = s // c_len
    rb0 = row0 // c_len
    nhb = H // hpb
    has_state = state0 is not None
    rowblk = lambda b, h, c: rb0 + b * nc + c
    wk, wv = hpb * dk, hpb * dv
    kb = H * dk // wk
    vb = 2 * H * dk // wv
    in_specs = [pl.BlockSpec((c_len, wk), lambda b, h, c: (rowblk(b, h, c), h)),
                pl.BlockSpec((c_len, wk), lambda b, h, c: (rowblk(b, h, c), kb + h)),
                pl.BlockSpec((c_len, wv), lambda b, h, c: (rowblk(b, h, c), vb + h)),
                pl.BlockSpec((c_len, wv), lambda b, h, c: (rowblk(b, h, c), vb + nhb + h)),
                pl.BlockSpec((1, wv), lambda b, h, c: (0, h))]
    args = [proj, proj, proj, proj, gn]
    if has_state:
        in_specs.append(pl.BlockSpec((None, hpb, dk, dv), lambda b, h, c: (b, h, 0, 0)))
        args.append(state0)
    aliases = {}
    if o_prev is not None:
        in_specs.append(pl.BlockSpec(memory_space=pl.ANY))
        args.append(o_prev)
        aliases = {len(args) - 1: 0}

    def kern(*refs):
        if o_prev is not None:
            n_in = len(args)
            refs = refs[:n_in - 1] + refs[n_in:]
        _ret_kernel(*refs, c_len=c_len, has_state=has_state, hpb=hpb, dk=dk, dv=dv)

    return pl.pallas_call(
        kern,
        grid=(nb, nhb, nc),
        in_specs=in_specs,
        out_specs=[pl.BlockSpec((c_len, wv), lambda b, h, c: (rowblk(b, h, c), h)),
                   pl.BlockSpec((None, hpb, dk, dv), lambda b, h, c: (b, h, 0, 0))],
        out_shape=[jax.ShapeDtypeStruct((T, H * dv), BF16),
                   jax.ShapeDtypeStruct((nb, H, dk, dv), F32)],
        scratch_shapes=[pltpu.VMEM((hpb, dk, dv), F32)],
        input_output_aliases=aliases,
        compiler_params=_cp(("parallel", "parallel", "arbitrary")),
        name="retention",
    )(*args)


def _c_out_kernel(x_ref, o_ref, w_ref, y_ref):
    y_ref[...] = x_ref[...] + _dot(o_ref[...], w_ref[...])


def _c_out(x, o, w, li, tm, tn):
    T, D = x.shape
    K = o.shape[1]
    return pl.pallas_call(
        _c_out_kernel,
        grid=(pl.cdiv(T, tm), D // tn),
        in_specs=[pl.BlockSpec((tm, tn), lambda i, j: (i, j)), pl.BlockSpec((tm, K), lambda i, j: (i, 0)),
                  pl.BlockSpec((None, K, tn), lambda i, j: (li, 0, j))],
        out_specs=pl.BlockSpec((tm, tn), lambda i, j: (i, j)),
        out_shape=jax.ShapeDtypeStruct((T, D), F32),
        compiler_params=_cp(("parallel", "arbitrary")),
        name="c_out",
    )(x, o, w)


def _rope_tables(pos, half):
    inv = ROPE_THETA ** (-jnp.arange(half, dtype=F32) / half)
    ang = pos.astype(F32)[:, None] * inv[None, :]
    return jnp.cos(ang), jnp.sin(ang)


def _swap_halves(w):
    half = w.shape[-1] // 2
    return jnp.concatenate([w[..., half:], w[..., :half]], axis=-1)


def _prep_w_in_ab(w):
    o1 = Q_LORA + KV_LORA
    o2 = o1 + MLA_ROPE
    kr = w[:, o1:o2]
    ks = _swap_halves(kr)
    return jnp.concatenate([w[:, :o1], kr, ks, ks, kr, w[:, o2:]], axis=1).astype(BF16)


def _prep_w_q_b(w):
    w = w.reshape(Q_LORA, MLA_HEADS, MLA_NOPE + MLA_ROPE)
    rope = w[..., MLA_NOPE:]
    swap = _swap_halves(rope)
    out = jnp.concatenate([w[..., :MLA_NOPE], rope, swap, swap, rope], axis=-1)
    return out.reshape(Q_LORA, -1).astype(BF16)


def _prep_w_kv_b(w):
    w = w.reshape(KV_LORA, MLA_HEADS, MLA_NOPE + MLA_V)
    return jnp.concatenate([w[..., :MLA_NOPE].reshape(KV_LORA, -1),
                            w[..., MLA_NOPE:].reshape(KV_LORA, -1)], axis=1).astype(BF16)


def kernel(x_prompt, x_sample, cache_mla_ckv, cache_mla_krope, state_s5_re, state_s5_im, state_ret,
           norm_mix, norm_mlp, norm_final, w_in_ab, q_a_norm, kv_a_norm, w_q_b, w_kv_b,
           s5_lam_re, s5_lam_im, s5_log_dt, s5_b_re, s5_b_im, s5_c_re, s5_c_im, s5_d, w_glu, b_glu,
           w_out_ab, w_in_c, ret_gn, w_out_c, w_up, w_down):
    BP, SP, D = x_prompt.shape
    BS, SS, _ = x_sample.shape
    past = cache_mla_ckv.shape[2]
    depth = norm_mix.shape[0]
    G, N = s5_lam_re.shape[1:]
    P, L = S5_GROUP, S5_L
    dk, dv = state_ret.shape[3:]
    TP, TS = BP * SP, BS * SS
    T = TP + TS
    TM = 512

    x = (x_prompt.reshape(TP, D), x_sample.reshape(TS, D))
    pos = jnp.concatenate([jnp.arange(SP), jnp.tile(past + jnp.arange(SS), 2 * TM // SS)])

    c32, s32 = _rope_tables(pos, MLA_ROPE // 2)
    z64 = jnp.zeros((pos.shape[0], LANE - MLA_ROPE), F32)
    cs1 = jnp.concatenate([c32, c32, z64], axis=1)
    cs2 = jnp.concatenate([-s32, s32, z64], axis=1)
    cos_r, sin_r = _rope_tables(pos, dk // 2)

    skp = -(-(past + SS) // LANE) * LANE
    w_glu_b, w_out_ab_b, w_in_c_b, w_out_c_b, w_up_b, w_down_b = (
        w.astype(BF16) for w in (w_glu, w_out_ab, w_in_c, w_out_c, w_up, w_down))
    ckv_p, kr_p, s5r_p, s5i_p, ret_p = [], [], [], [], []
    ckv_s, kr_s, s5r_s, s5i_s, ret_s = [], [], [], [], []

    for layer in range(depth):
        i = layer // 2
        g_mix = norm_mix[layer].reshape(1, D)
        if layer % 2 == 0:
            xs = x if isinstance(x, tuple) else (x,)
            q, ckv, kr, u = _ab_in(xs, g_mix, _prep_w_in_ab(w_in_ab[i]), q_a_norm[i].reshape(1, -1),
                                   kv_a_norm[i].reshape(1, -1), _prep_w_q_b(w_q_b[i]), cs1, cs2, 256, TP, SP)
            ckv_p.append(ckv[:TP].reshape(BP, SP, KV_LORA))
            kr_p.append(kr[:TP, :MLA_ROPE].reshape(BP, SP, MLA_ROPE))
            ckv_s.append(ckv[TP:].reshape(BS, SS, KV_LORA))
            kr_s.append(kr[TP:, :MLA_ROPE].reshape(BS, SS, MLA_ROPE))
            pad = skp - past - SS
            ckv_hist = jnp.concatenate([cache_mla_ckv[i], ckv[TP:].reshape(BS, SS, KV_LORA),
                                        jnp.zeros((BS, pad, KV_LORA), F32)], axis=1)
            kr_past = jnp.pad(cache_mla_krope[i], ((0, 0), (0, 0), (0, LANE - MLA_ROPE)))
            kr_hist = jnp.concatenate([kr_past, kr[TP:].reshape(BS, SS, LANE),
                                       jnp.zeros((BS, pad, LANE), F32)], axis=1)
            w_kv = _prep_w_kv_b(w_kv_b[i])
            k_p, v_p = _kv_proj(ckv, kr, w_kv, 512)
            k_s, v_s = _kv_proj(ckv_hist.reshape(BS * skp, KV_LORA), kr_hist.reshape(BS * skp, LANE),
                                w_kv, skp)
            attn = (_attn_long(q, k_p, v_p, BP, SP, ATT_TK),
                    _attn_short(q, k_s, v_s, BS, SS, skp, TP, 0, past, past + SS))
            params = _s5_params(s5_lam_re[i], s5_lam_im[i], s5_log_dt[i], s5_b_re[i], s5_b_im[i],
                                s5_c_re[i], s5_c_im[i])
            ncp, ncs = SP // L, SS // L
            ub = u.astype(BF16)
            u_p = ub[:TP].reshape(BP, ncp, L, G, P).transpose(3, 1, 0, 2, 4).reshape(G, ncp * BP, L * P)
            u_s = ub[TP:].reshape(BS, ncs, L, G, P).transpose(3, 1, 0, 2, 4).reshape(G, ncs * BS, L * P)
            y_p, hr_p, hi_p = _s5_main(u_p, params, None, BP, ncp)
            h0 = (jnp.swapaxes(state_s5_re[i], 0, 1), jnp.swapaxes(state_s5_im[i], 0, 1))
            y_s, hr_s, hi_s = _s5_main(u_s, params, h0, BS, ncs)
            y_p = y_p.reshape(G, ncp, BP, L, P).transpose(2, 1, 3, 0, 4).reshape(TP, G * P)
            y_s = y_s.reshape(G, ncs, BS, L, P).transpose(2, 1, 3, 0, 4).reshape(TS, G * P)
            s5r_p.append(jnp.swapaxes(hr_p, 0, 1))
            s5i_p.append(jnp.swapaxes(hi_p, 0, 1))
            s5r_s.append(jnp.swapaxes(hr_s, 0, 1))
            s5i_s.append(jnp.swapaxes(hi_s, 0, 1))
            x = _ab_out(xs, attn, (y_p, y_s), u, s5_d[i].reshape(1, G * P), w_glu_b,
                        b_glu[i].reshape(1, -1), w_out_ab_b, i, 256)
        else:
            proj = _c_in(x, g_mix, w_in_c_b, i, cos_r, sin_r, dk, dv, 2 * TM, 1024, TP, (RET_C, SS), SP)
            gn = ret_gn[i].reshape(1, -1)
            o, st_p = _retention(proj, gn, None, None, BP, SP, RET_C, 0, dk, dv, RET_HPB)
            o, st_s = _retention(proj, gn, state_ret[i], o, BS, SS, SS, TP, dk, dv, RET_HPB)
            ret_p.append(st_p)
            ret_s.append(st_s)
            x = _c_out(x, o, w_out_c_b, i, 2 * TM, 512)
        last = layer == depth - 1
        outs = _mlp(x, norm_mlp[layer].reshape(1, D), w_up_b, w_down_b, layer,
                    norm_final.reshape(1, D), last, (TP, TS) if last else (T,), TM, MLP_TF)
        x = outs[0]

    return (outs[0].reshape(BP, SP, D), outs[1].reshape(BS, SS, D),
            jnp.stack(ckv_p), jnp.stack(kr_p), jnp.stack(s5r_p), jnp.stack(s5i_p), jnp.stack(ret_p),
            jnp.stack(ckv_s), jnp.stack(kr_s), jnp.stack(s5r_s), jnp.stack(s5i_s), jnp.stack(ret_s))
```

```python
import functools
import math

import jax
import jax.numpy as jnp
import numpy as np
from jax import lax
from jax.experimental import pallas as pl
from jax.experimental.pallas import tpu as pltpu

F32 = jnp.float32
BF16 = jnp.bfloat16

CHUNK = 64
MLA_V = 128
MLA_NOPE = 128
MLA_ROPE = 64
MLA_HEADS = 8
Q_LORA = 512
KV_LORA = 256
MLA_SCALE = (MLA_NOPE + MLA_ROPE) ** -0.5
S5_GROUP = 16
S5_STATE = 64
RET_HEADS = 8
ROPE_THETA = 10000.0
EPS = 1e-6
GN_EPS = 1e-5
LOG2E = 1.4426950408889634

LANE = 128
QK_PAD = 256
V_PAD = 256
S5_L = 64
TM = 512
WIDE_TM = 1024
AB_TM = 256
C_IN_TN = 1024
C_OUT_TN = 512
RET_C = 256
RET_HPB = 8
MLP_TF = 1024
ATT_TK = 512
NEG_BIG = -1e30
VMEM_LIMIT = 56 * 2 ** 20


def _cp(sem):
    return pltpu.CompilerParams(dimension_semantics=sem, vmem_limit_bytes=VMEM_LIMIT)


def _rms(x, g):
    return x * lax.rsqrt(jnp.mean(x * x, axis=-1, keepdims=True) + EPS) * g


def _dot(a, b):
    return jnp.dot(a, b, preferred_element_type=F32)


def _dot_nt(a, b):
    return lax.dot_general(a, b, (((1,), (1,)), ((), ())), preferred_element_type=F32)


def _dot_tn(a, b):
    return lax.dot_general(a, b, (((0,), (0,)), ((), ())), preferred_element_type=F32)


def _parts_specs(rows, tm, width, row_axis=0):
    assert all(r % tm == 0 for r in rows[:-1])
    specs, starts, start = [], [], 0
    for r in rows:
        nblk = pl.cdiv(r, tm)

        def index_map(*g, start=start, nblk=nblk):
            return (jnp.clip(g[row_axis] - start, 0, nblk - 1), 0)

        specs.append(pl.BlockSpec((tm, width), index_map))
        starts.append(start)
        start += nblk
    return specs, tuple(starts)


def _parts_load(refs, starts, i):
    x = refs[0][...]
    for ref, s in zip(refs[1:], starts[1:]):
        x = jnp.where(i >= s, ref[...], x)
    return x


def _parts_store(refs, starts, i, val):
    ends = starts[1:] + (None,)
    for ref, s, e in zip(refs, starts, ends):
        cond = i >= s if e is None else (i >= s) & (i < e)

        @pl.when(cond)
        def _(ref=ref):
            ref[...] = val


def _ab_in_kernel(*refs, x_starts, ub_starts):
    n = len(x_starts)
    x_refs = refs[:n]
    (g_ref, win_ref, qn_ref, kvn_ref, wq_ref, cs1_ref, cs2_ref,
     q_out, ckv_out, kr_out, u_out) = refs[n:n + 11]
    ub_outs = refs[n + 11:]
    x = _parts_load(x_refs, x_starts, pl.program_id(0))
    xn = _rms(x, g_ref[...]).astype(BF16)
    proj = _dot(xn, win_ref[...])
    cqn = _rms(proj[:, :Q_LORA], qn_ref[...]).astype(BF16)
    o = Q_LORA
    ckv_out[...] = _rms(proj[:, o:o + KV_LORA], kvn_ref[...])
    o += KV_LORA
    cs1 = cs1_ref[...]
    cs2 = cs2_ref[...]
    kr_out[...] = proj[:, o:o + LANE] * cs1 + proj[:, o + LANE:o + 2 * LANE] * cs2
    o += 2 * LANE
    u_out[...] = proj[:, o:]
    _parts_store(ub_outs, ub_starts, pl.program_id(0), proj[:, o:].astype(BF16))
    q = _dot(cqn, wq_ref[...])
    qs = MLA_SCALE * LOG2E
    for h in range(MLA_HEADS):
        b = h * 3 * LANE
        q_out[:, h * QK_PAD:h * QK_PAD + LANE] = (q[:, b:b + LANE] * qs).astype(BF16)
        rope = q[:, b + LANE:b + 2 * LANE] * cs1 + q[:, b + 2 * LANE:b + 3 * LANE] * cs2
        q_out[:, h * QK_PAD + LANE:(h + 1) * QK_PAD] = (rope * qs).astype(BF16)


def _pos_table_spec(tm, width, long_rows, seq_len, axis=0):
    assert seq_len % tm == 0 and long_rows % tm == 0
    n_long, per_seq = long_rows // tm, seq_len // tm

    def index_map(*g):
        return (jnp.where(g[axis] < n_long, g[axis] % per_seq, per_seq), 0)

    return pl.BlockSpec((tm, width), index_map)


def _ab_in(xs, g, win, qn, kvn, wq, cs1, cs2, tm, long_rows, seq_len):
    D = xs[0].shape[1]
    T = sum(x.shape[0] for x in xs)
    row = lambda i: (i, 0)
    fix = lambda i: (0, 0)
    s5w = win.shape[1] - Q_LORA - KV_LORA - 2 * LANE
    x_specs, x_starts = _parts_specs([x.shape[0] for x in xs], tm, D)
    ub_rows = (long_rows, T - long_rows)
    ub_specs, ub_starts = _parts_specs(ub_rows, tm, s5w)
    return pl.pallas_call(
        functools.partial(_ab_in_kernel, x_starts=x_starts, ub_starts=ub_starts),
        grid=(T // tm,),
        in_specs=x_specs + [pl.BlockSpec((1, D), fix),
                  pl.BlockSpec(win.shape, fix), pl.BlockSpec((1, Q_LORA), fix),
                  pl.BlockSpec((1, KV_LORA), fix), pl.BlockSpec(wq.shape, fix),
                  _pos_table_spec(tm, LANE, long_rows, seq_len), _pos_table_spec(tm, LANE, long_rows, seq_len)],
        out_specs=[pl.BlockSpec((tm, MLA_HEADS * QK_PAD), row), pl.BlockSpec((tm, KV_LORA), row),
                   pl.BlockSpec((tm, LANE), row), pl.BlockSpec((tm, s5w), row)] + ub_specs,
        out_shape=[jax.ShapeDtypeStruct((T, MLA_HEADS * QK_PAD), BF16),
                   jax.ShapeDtypeStruct((T, KV_LORA), F32),
                   jax.ShapeDtypeStruct((T, LANE), F32),
                   jax.ShapeDtypeStruct((T, s5w), F32)]
        + [jax.ShapeDtypeStruct((r, s5w), BF16) for r in ub_rows],
        compiler_params=_cp(("arbitrary",)),
        name="ab_in",
    )(*xs, g, win, qn, kvn, wq, cs1, cs2)


def _kv_kernel(ckv_ref, kr_ref, w_ref, k_out, v_out):
    kv = _dot(ckv_ref[...].astype(BF16), w_ref[...])
    kr = kr_ref[...].astype(BF16)
    v0 = MLA_HEADS * MLA_NOPE
    ones = jnp.ones((kv.shape[0], V_PAD - MLA_V), BF16)
    for h in range(MLA_HEADS):
        k_out[:, h * QK_PAD:h * QK_PAD + LANE] = kv[:, h * LANE:(h + 1) * LANE].astype(BF16)
        k_out[:, h * QK_PAD + LANE:(h + 1) * QK_PAD] = kr
        v_out[:, h * V_PAD:h * V_PAD + MLA_V] = kv[:, v0 + h * MLA_V:v0 + (h + 1) * MLA_V].astype(BF16)
        v_out[:, h * V_PAD + MLA_V:(h + 1) * V_PAD] = ones


def _kv_proj(ckv, kr, w, tm):
    R = ckv.shape[0]
    row = lambda i: (i, 0)
    return pl.pallas_call(
        _kv_kernel,
        grid=(R // tm,),
        in_specs=[pl.BlockSpec((tm, KV_LORA), row), pl.BlockSpec((tm, LANE), row),
                  pl.BlockSpec(w.shape, lambda i: (0, 0))],
        out_specs=[pl.BlockSpec((tm, MLA_HEADS * QK_PAD), row),
                   pl.BlockSpec((tm, MLA_HEADS * V_PAD), row)],
        out_shape=[jax.ShapeDtypeStruct((R, MLA_HEADS * QK_PAD), BF16),
                   jax.ShapeDtypeStruct((R, MLA_HEADS * V_PAD), BF16)],
        compiler_params=_cp(("parallel",)),
        name="kv_proj",
    )(ckv, kr, w)


def _attn_long_kernel(q_ref, k_ref, v_ref, o_ref, s_sc, m_sc, acc_sc, *, tq, tk):
    qi = pl.program_id(2)

    def scores_to(slot, kt, r0=0):
        s_sc[slot, r0:, :] = _dot_nt(q_ref[r0:, :], k_ref[pl.ds(pl.multiple_of(kt * tk, tk), tk), :])

    def consume(slot, kt, masked, r0=0):
        s = s_sc[slot, r0:, :]
        if masked:
            qpos = qi * tq + r0 + lax.broadcasted_iota(jnp.int32, (tq - r0, tk), 0)
            kpos = kt * tk + lax.broadcasted_iota(jnp.int32, (tq - r0, tk), 1)
            s = jnp.where(kpos // CHUNK <= qpos // CHUNK, s, NEG_BIG)
        m = m_sc[r0:, :]
        m_new = jnp.maximum(m, jnp.max(s, axis=1, keepdims=True))
        alpha = jnp.exp2(m - m_new)
        p = jnp.exp2(s - jnp.concatenate([m_new] * (tk // LANE), axis=1))
        m_sc[r0:, :] = m_new
        v = v_ref[pl.ds(pl.multiple_of(kt * tk, tk), tk), :]
        acc_sc[r0:, :] = (jnp.concatenate([alpha] * (V_PAD // LANE), axis=1) * acc_sc[r0:, :]
                          + _dot(p.astype(BF16), v))

    m_sc[...] = jnp.full_like(m_sc, NEG_BIG)
    acc_sc[...] = jnp.zeros_like(acc_sc)
    scores_to(0, 0)

    def pair(j, carry):
        scores_to(1, 2 * j + 1)
        consume(0, 2 * j, False)
        scores_to(0, 2 * j + 2)
        consume(1, 2 * j + 1, False)
        return carry

    lax.fori_loop(0, qi, pair, 0)
    scores_to(1, 2 * qi + 1, tk)
    consume(0, 2 * qi, True)
    consume(1, 2 * qi + 1, True, tk)
    acc = acc_sc[...]
    o_ref[...] = (acc[:, :MLA_V] / acc[:, MLA_V:]).astype(BF16)


def _attn_long(q, k, v, nb, s, tk):
    tq = 2 * tk
    assert s % tq == 0 and V_PAD == 2 * MLA_V
    nq = s // tq
    return pl.pallas_call(
        functools.partial(_attn_long_kernel, tq=tq, tk=tk),
        grid=(nb, MLA_HEADS, nq),
        in_specs=[pl.BlockSpec((tq, QK_PAD), lambda b, h, i: (b * nq + i, h)),
                  pl.BlockSpec((s, QK_PAD), lambda b, h, i: (b, h)),
                  pl.BlockSpec((s, V_PAD), lambda b, h, i: (b, h))],
        out_specs=pl.BlockSpec((tq, MLA_V), lambda b, h, i: (b * nq + i, h)),
        out_shape=jax.ShapeDtypeStruct((nb * s, MLA_HEADS * MLA_V), BF16),
        scratch_shapes=[pltpu.VMEM((2, tq, tk), F32), pltpu.VMEM((tq, LANE), F32),
                        pltpu.VMEM((tq, V_PAD), F32)],
        compiler_params=_cp(("parallel", "parallel", "arbitrary")),
        name="attn_long",
    )(q, k, v)


def _attn_short_kernel(q_ref, k_ref, v_ref, o_ref, *, q_pos0, n_keys):
    sq = q_ref.shape[0]
    skp = k_ref.shape[0]
    s = _dot_nt(q_ref[...], k_ref[...])
    qpos = q_pos0 + lax.broadcasted_iota(jnp.int32, (sq, skp), 0)
    kpos = lax.broadcasted_iota(jnp.int32, (sq, skp), 1)
    ok = (kpos // CHUNK <= qpos // CHUNK) & (kpos < n_keys)
    s = jnp.where(ok, s, NEG_BIG)
    m = jnp.max(s, axis=1, keepdims=True)
    p = jnp.exp2(s - m)
    acc = _dot(p.astype(BF16), v_ref[...])
    o_ref[...] = (acc[:, :MLA_V] / acc[:, MLA_V:]).astype(BF16)


def _attn_short(q, k, v, nb, sq, skp, q_row0, k_row0, q_pos0, n_keys):
    qb0 = q_row0 // sq
    kb0 = k_row0 // skp
    return pl.pallas_call(
        functools.partial(_attn_short_kernel, q_pos0=q_pos0, n_keys=n_keys),
        grid=(nb, MLA_HEADS),
        in_specs=[pl.BlockSpec((sq, QK_PAD), lambda b, h: (qb0 + b, h)),
                  pl.BlockSpec((skp, QK_PAD), lambda b, h: (kb0 + b, h)),
                  pl.BlockSpec((skp, V_PAD), lambda b, h: (kb0 + b, h))],
        out_specs=pl.BlockSpec((sq, MLA_V), lambda b, h: (b, h)),
        out_shape=jax.ShapeDtypeStruct((nb * sq, MLA_HEADS * MLA_V), BF16),
        compiler_params=_cp(("parallel", "parallel")),
        name="attn_short",
    )(q, k, v)


def _s5_param_kernel(lr_ref, li_ref, ldt_ref, btr_ref, bti_ref, cr_ref, ci_ref,
                     m_out, wr_out, wi_out, vr_out, vi_out, alr_out, ali_out,
                     car_sc, cai_sc):
    L, P = S5_L, S5_GROUP
    lr = jnp.minimum(lr_ref[...], -1e-4)
    li = li_ref[...]
    dt = jnp.exp(ldt_ref[...])
    mag = jnp.exp(lr * dt)
    a_re = mag * jnp.cos(li * dt)
    a_im = mag * jnp.sin(li * dt)
    den = lr * lr + li * li
    f_re = ((a_re - 1.0) * lr + a_im * li) / den
    f_im = (a_im * lr - (a_re - 1.0) * li) / den
    btr = btr_ref[...]
    bti = bti_ref[...]
    bb_re = f_re * btr - f_im * bti
    bb_im = f_re * bti + f_im * btr
    cr = cr_ref[...]
    ci = ci_ref[...]

    n_pow = -(-(L + 1) // 8) * 8
    t = lax.broadcasted_iota(jnp.int32, (n_pow, 1), 0).astype(F32)
    pmag = jnp.exp((lr * dt) * t)
    pang = (li * dt) * t
    pw_re = pmag * jnp.cos(pang)
    pw_im = pmag * jnp.sin(pang)

    def apow(t):
        return pw_re[t:t + 1, :], pw_im[t:t + 1, :]

    for i in range(L):
        rows = slice(i * P, (i + 1) * P)
        pr, pi = apow(L - 1 - i)
        wr_out[rows, :] = (pr * bb_re - pi * bb_im).astype(BF16)
        wi_out[rows, :] = (pr * bb_im + pi * bb_re).astype(BF16)
        pr, pi = apow(i + 1)
        vr_out[rows, :] = (cr * pr - ci * pi).astype(BF16)
        vi_out[rows, :] = (-(cr * pi + ci * pr)).astype(BF16)
        pr, pi = apow(i)
        car_sc[rows, :] = cr * pr - ci * pi
        cai_sc[rows, :] = cr * pi + ci * pr
    pr, pi = apow(L)
    alr_out[...] = pr
    ali_out[...] = pi
    hp = lax.Precision.HIGHEST
    nt = (((1,), (1,)), ((), ()))
    krow = (lax.dot_general(bb_re, car_sc[...], nt, precision=hp, preferred_element_type=F32)
            - lax.dot_general(bb_im, cai_sc[...], nt, precision=hp, preferred_element_type=F32))
    lane = lax.broadcasted_iota(jnp.int32, (P, L * P), 1)
    for i in range(L):
        shifted = krow if i == 0 else pltpu.roll(krow, i * P, axis=1)
        m_out[i * P:(i + 1) * P, :] = jnp.where(lane >= i * P, shifted, 0.0).astype(BF16)


def _s5_params(lam_re, lam_im, log_dt, b_re, b_im, c_re, c_im):
    G, N = lam_re.shape
    P, L = S5_GROUP, S5_L
    LP = L * P
    v3 = lambda a: a.reshape(G, 1, N)
    ldt = jnp.broadcast_to(log_dt[:, None, None], (G, 1, N))
    btr = jnp.swapaxes(b_re, 1, 2)
    bti = jnp.swapaxes(b_im, 1, 2)
    vec = pl.BlockSpec((None, 1, N), lambda g: (g, 0, 0))
    mat = pl.BlockSpec((None, P, N), lambda g: (g, 0, 0))
    big = pl.BlockSpec((None, LP, N), lambda g: (g, 0, 0))
    return pl.pallas_call(
        _s5_param_kernel,
        grid=(G,),
        in_specs=[vec, vec, vec, mat, mat, mat, mat],
        out_specs=[pl.BlockSpec((None, LP, LP), lambda g: (g, 0, 0)), big, big, big, big, vec, vec],
        out_shape=[jax.ShapeDtypeStruct((G, LP, LP), BF16)]
        + [jax.ShapeDtypeStruct((G, LP, N), BF16)] * 4
        + [jax.ShapeDtypeStruct((G, 1, N), F32)] * 2,
        scratch_shapes=[pltpu.VMEM((LP, N), F32), pltpu.VMEM((LP, N), F32)],
        compiler_params=_cp(("parallel",)),
        name="s5_params",
    )(v3(lam_re), v3(lam_im), ldt, btr, bti, c_re, c_im)


def _s5_main_kernel(*refs, nseq, nchunk, has_state):
    if has_state:
        (u_ref, m_ref, wr_ref, wi_ref, vr_ref, vi_ref, alr_ref, ali_ref, h0r_ref, h0i_ref,
         y_out, hfr_out, hfi_out, xr_sc, xi_sc, hr_sc, hi_sc) = refs
    else:
        (u_ref, m_ref, wr_ref, wi_ref, vr_ref, vi_ref, alr_ref, ali_ref,
         y_out, hfr_out, hfi_out, xr_sc, xi_sc, hr_sc, hi_sc) = refs
    ub = u_ref[...]
    xr_sc[...] = _dot(ub, wr_ref[...])
    xi_sc[...] = _dot(ub, wi_ref[...])
    ar = alr_ref[...]
    ai = ali_ref[...]
    n = ar.shape[-1]
    if has_state:
        init = (h0r_ref[...], h0i_ref[...])
    else:
        init = (jnp.zeros((nseq, n), F32), jnp.zeros((nseq, n), F32))

    def body(c, carry):
        sr, si = carry
        rows = pl.ds(pl.multiple_of(c * nseq, nseq), nseq)
        hr_sc[rows, :] = sr
        hi_sc[rows, :] = si
        return (ar * sr - ai * si + xr_sc[rows, :], ar * si + ai * sr + xi_sc[rows, :])

    sr, si = lax.fori_loop(0, nchunk, body, init)
    hfr_out[...] = sr
    hfi_out[...] = si
    y_out[...] = (_dot(ub, m_ref[...])
                  + _dot_nt(hr_sc[...].astype(BF16), vr_ref[...])
                  + _dot_nt(hi_sc[...].astype(BF16), vi_ref[...])).astype(y_out.dtype)


def _s5_main(u, params, h0, nseq, nchunk):
    m, wr, wi, vr, vi, alr, ali = params
    G, R, LP = u.shape
    N = alr.shape[-1]
    has_state = h0 is not None
    g3 = lambda g: (g, 0, 0)
    big = pl.BlockSpec((None, LP, N), g3)
    vec = pl.BlockSpec((None, 1, N), g3)
    st = pl.BlockSpec((None, nseq, N), g3)
    in_specs = [pl.BlockSpec((None, R, LP), g3), pl.BlockSpec((None, LP, LP), g3),
                big, big, big, big, vec, vec]
    args = [u, m, wr, wi, vr, vi, alr, ali]
    if has_state:
        in_specs += [st, st]
        args += list(h0)
    return pl.pallas_call(
        functools.partial(_s5_main_kernel, nseq=nseq, nchunk=nchunk, has_state=has_state),
        grid=(G,),
        in_specs=in_specs,
        out_specs=[pl.BlockSpec((None, R, LP), g3), st, st],
        out_shape=[jax.ShapeDtypeStruct((G, R, LP), BF16),
                   jax.ShapeDtypeStruct((G, nseq, N), F32),
                   jax.ShapeDtypeStruct((G, nseq, N), F32)],
        scratch_shapes=[pltpu.VMEM((R, N), F32)] * 4,
        compiler_params=_cp(("parallel",)),
        name="s5_main",
    )(*args)


def _ab_out_kernel(*refs, starts):
    i = pl.program_id(0)
    parts = []
    for st in starts:
        parts.append(_parts_load(refs[:len(st)], st, i))
        refs = refs[len(st):]
    x, attn, y = parts
    u_ref, d_ref, wglu_ref, bglu_ref, wo_ref, o_ref = refs
    y = y.astype(F32) + d_ref[...] * u_ref[...]
    z = y * (0.5 * (1.0 + jnp.tanh(math.sqrt(2.0 / math.pi) * (y + 0.044715 * (y * y * y)))))
    gate = 1.0 / (1.0 + jnp.exp(-(_dot(z.astype(BF16), wglu_ref[...]) + bglu_ref[...])))
    ssm = (z * gate).astype(BF16)
    na = attn.shape[1]
    o_ref[...] = x + _dot(attn, wo_ref[:na, :]) + _dot(ssm, wo_ref[na:, :])


def _ab_out(xs, attns, ys, u, d, wglu, bglu, wo, li, tm):
    D = xs[0].shape[1]
    T = sum(x.shape[0] for x in xs)
    row = lambda i: (i, 0)
    fix = lambda i: (0, 0)
    lay = lambda i: (li, 0, 0)
    specs, starts = [], []
    for parts in (xs, attns, ys):
        sp, st = _parts_specs([p.shape[0] for p in parts], tm, parts[0].shape[1])
        specs += sp
        starts.append(st)
    return pl.pallas_call(
        functools.partial(_ab_out_kernel, starts=tuple(starts)),
        grid=(T // tm,),
        in_specs=specs + [pl.BlockSpec((tm, u.shape[1]), row), pl.BlockSpec(d.shape, fix),
                          pl.BlockSpec((None,) + wglu.shape[1:], lay), pl.BlockSpec(bglu.shape, fix),
                          pl.BlockSpec((None,) + wo.shape[1:], lay)],
        out_specs=pl.BlockSpec((tm, D), row),
        out_shape=jax.ShapeDtypeStruct((T, D), F32),
        compiler_params=_cp(("parallel",)),
        name="ab_out",
    )(*xs, *attns, *ys, u, d, wglu, bglu, wo)


def _mlp_kernel(x_ref, g_ref, wup_ref, wdn_ref, gf_ref, *rest, final_norm, o_starts):
    n_out = len(o_starts)
    o_refs = rest[:n_out]
    xn_sc = rest[n_out]
    acc = rest[n_out + 1] if n_out > 1 else o_refs[0]
    f = pl.program_id(1)

    @pl.when(f == 0)
    def _():
        xn_sc[...] = _rms(x_ref[...], g_ref[...]).astype(BF16)
        acc[...] = jnp.zeros_like(acc)

    a = jnp.maximum(_dot(xn_sc[...], wup_ref[...]), 0.0)
    acc[...] += _dot((a * a).astype(BF16), wdn_ref[...])

    @pl.when(f == pl.num_programs(1) - 1)
    def _():
        r = x_ref[...] + acc[...]
        if final_norm:
            r = _rms(r, gf_ref[...])
        if n_out > 1:
            _parts_store(o_refs, o_starts, pl.program_id(0), r)
        else:
            acc[...] = r


def _mlp(x, g, wup, wdn, li, gf, final_norm, out_rows, tm, tf):
    T, D = x.shape
    FF = wup.shape[2]
    assert sum(out_rows) == T
    o_specs, o_starts = _parts_specs(out_rows, tm, D)
    return pl.pallas_call(
        functools.partial(_mlp_kernel, final_norm=final_norm, o_starts=o_starts),
        grid=(pl.cdiv(T, tm), FF // tf),
        in_specs=[pl.BlockSpec((tm, D), lambda i, f: (i, 0)), pl.BlockSpec((1, D), lambda i, f: (0, 0)),
                  pl.BlockSpec((None, D, tf), lambda i, f: (li, 0, f)),
                  pl.BlockSpec((None, tf, D), lambda i, f: (li, f, 0)),
                  pl.BlockSpec((1, D), lambda i, f: (0, 0))],
        out_specs=o_specs,
        out_shape=[jax.ShapeDtypeStruct((r, D), F32) for r in out_rows],
        scratch_shapes=[pltpu.VMEM((tm, D), BF16)] + [pltpu.VMEM((tm, D), F32)] * (len(out_rows) > 1),
        compiler_params=_cp(("arbitrary" if len(out_rows) > 1 else "parallel", "arbitrary")),
        name="mlp",
    )(x, g, wup, wdn, gf)


def _ret_log_decay(head):
    hf = jnp.full((1, 1), head, jnp.int32).astype(F32)
    return jnp.log(1.0 - jnp.exp2(-5.0 - hf))


def _c_in_kernel(x_ref, g_ref, w_ref, cos_ref, sin_ref, o_ref, xn_sc, *, n_rot, n_q, n_lin, dk,
                 chunk_split_row, chunk_lens):
    i = pl.program_id(0)
    j = pl.program_id(1)

    @pl.when(j == 0)
    def _():
        xn_sc[...] = _rms(x_ref[...], g_ref[...]).astype(BF16)

    tm = x_ref.shape[0]
    tn = w_ref.shape[1]
    half = dk // 2

    def chunks():
        for s in range(0, tn, dk):
            yield s, _dot(xn_sc[...], w_ref[:, s:s + dk])

    @pl.when(j < n_rot)
    def _():
        cos = cos_ref[...]
        sin = sin_ref[...]
        c_len = jnp.where(i * tm >= chunk_split_row, chunk_lens[1], chunk_lens[0])
        pos = (lax.broadcasted_iota(jnp.int32, (tm, 1), 0) & (c_len - 1)).astype(F32) + 1.0
        sign = jnp.where(j < n_q, 1.0, -1.0)
        scale = jnp.where(j < n_q, 1.0, dk ** -0.5)
        for s, acc in chunks():
            lg = _ret_log_decay(((j * tn + s) // dk) % RET_HEADS)
            dec = jnp.exp((sign * lg) * pos) * scale
            x1 = acc[:, :half]
            x2 = acc[:, half:]
            o_ref[:, s:s + half] = ((x1 * cos - x2 * sin) * dec).astype(BF16)
            o_ref[:, s + half:s + dk] = ((x1 * sin + x2 * cos) * dec).astype(BF16)

    @pl.when((j >= n_rot) & (j < n_lin))
    def _():
        for s, acc in chunks():
            o_ref[:, s:s + dk] = acc.astype(BF16)

    @pl.when(j >= n_lin)
    def _():
        for s, acc in chunks():
            o_ref[:, s:s + dk] = (acc / (1.0 + jnp.exp(-acc))).astype(BF16)


def _c_in(x, g, w, li, cos, sin, dk, dv, tm, tn, chunk_split_row, chunk_lens, seq_len):
    T, D = x.shape
    N = w.shape[2]
    qk = RET_HEADS * dk
    n_q = qk // tn
    n_rot = 2 * qk // tn
    n_lin = (2 * qk + RET_HEADS * dv) // tn
    assert chunk_split_row % tm == 0
    assert all(tm % c == 0 and c & (c - 1) == 0 for c in chunk_lens)
    return pl.pallas_call(
        functools.partial(_c_in_kernel, n_rot=n_rot, n_q=n_q, n_lin=n_lin, dk=dk,
                          chunk_split_row=chunk_split_row, chunk_lens=chunk_lens),
        grid=(pl.cdiv(T, tm), N // tn),
        in_specs=[pl.BlockSpec((tm, D), lambda i, j: (i, 0)), pl.BlockSpec((1, D), lambda i, j: (0, 0)),
                  pl.BlockSpec((None, D, tn), lambda i, j: (li, 0, j)),
                  _pos_table_spec(tm, dk // 2, chunk_split_row, seq_len),
                  _pos_table_spec(tm, dk // 2, chunk_split_row, seq_len)],
        out_specs=pl.BlockSpec((tm, tn), lambda i, j: (i, j)),
        out_shape=jax.ShapeDtypeStruct((T, N), BF16),
        scratch_shapes=[pltpu.VMEM((tm, D), BF16)],
        compiler_params=_cp(("parallel", "arbitrary")),
        name="c_in",
    )(x, g, w, cos, sin)


def _ret_kernel(*refs, c_len, has_state, hpb, dk, dv):
    if has_state:
        q_ref, k_ref, v_ref, g_ref, gn_ref, s0_ref, o_ref, sf_ref, st_sc = refs
    else:
        q_ref, k_ref, v_ref, g_ref, gn_ref, o_ref, sf_ref, st_sc = refs
    hb = pl.program_id(1)
    c = pl.program_id(2)

    @pl.when(c == 0)
    def _():
        if has_state:
            st_sc[...] = s0_ref[...]
        else:
            st_sc[...] = jnp.zeros_like(st_sc)

    row = lax.broadcasted_iota(jnp.int32, (c_len, c_len), 0)
    col = lax.broadcasted_iota(jnp.int32, (c_len, c_len), 1)
    causal = row >= col
    for hh in range(hpb):
        q = q_ref[:, hh * dk:(hh + 1) * dk]
        k = k_ref[:, hh * dk:(hh + 1) * dk]
        v = v_ref[:, hh * dv:(hh + 1) * dv]
        inner = _dot(jnp.where(causal, _dot_nt(q, k), 0.0).astype(BF16), v)
        st = st_sc[hh]
        o = inner + _dot(q, st.astype(BF16))
        g_c = jnp.exp(_ret_log_decay(hb * hpb + hh) * float(c_len))
        st_sc[hh] = g_c * (st + _dot_tn(k, v))
        mu = jnp.mean(o, axis=-1, keepdims=True)
        d = o - mu
        var = jnp.mean(d * d, axis=-1, keepdims=True)
        on = d * lax.rsqrt(var + GN_EPS) * gn_ref[:, hh * dv:(hh + 1) * dv]
        o_ref[:, hh * dv:(hh + 1) * dv] = (on * g_ref[:, hh * dv:(hh + 1) * dv].astype(F32)).astype(BF16)

    @pl.when(c == pl.num_programs(2) - 1)
    def _():
        sf_ref[...] = st_sc[...]


def _retention(proj, gn, state0, o_prev, nb, s, c_len, row0, dk, dv, hpb):
    T = proj.shape[0]
    H = RET_HEADS
    nc = s // c_len
    rb0 = row0 // c_len
    nhb = H // hpb
    has_state = state0 is not None
    rowblk = lambda b, h, c: rb0 + b * nc + c
    wk, wv = hpb * dk, hpb * dv
    kb = H * dk // wk
    vb = 2 * H * dk // wv
    in_specs = [pl.BlockSpec((c_len, wk), lambda b, h, c: (rowblk(b, h, c), h)),
                pl.BlockSpec((c_len, wk), lambda b, h, c: (rowblk(b, h, c), kb + h)),
                pl.BlockSpec((c_len, wv), lambda b, h, c: (rowblk(b, h, c), vb + h)),
                pl.BlockSpec((c_len, wv), lambda b, h, c: (rowblk(b, h, c), vb + nhb + h)),
                pl.BlockSpec((1, wv), lambda b, h, c: (0, h))]
    args = [proj, proj, proj, proj, gn]
    if has_state:
        in_specs.append(pl.BlockSpec((None, hpb, dk, dv), lambda b, h, c: (b, h, 0, 0)))
        args.append(state0)
    aliases = {}
    if o_prev is not None:
        in_specs.append(pl.BlockSpec(memory_space=pl.ANY))
        args.append(o_prev)
        aliases = {len(args) - 1: 0}

    def kern(*refs):
        if o_prev is not None:
            n_in = len(args)
            refs = refs[:n_in - 1] + refs[n_in:]
        _ret_kernel(*refs, c_len=c_len, has_state=has_state, hpb=hpb, dk=dk, dv=dv)

    return pl.pallas_call(
        kern,
        grid=(nb, nhb, nc),
        in_specs=in_specs,
        out_specs=[pl.BlockSpec((c_len, wv), lambda b, h, c: (rowblk(b, h, c), h)),
                   pl.BlockSpec((None, hpb, dk, dv), lambda b, h, c: (b, h, 0, 0))],
        out_shape=[jax.ShapeDtypeStruct((T, H * dv), BF16),
                   jax.ShapeDtypeStruct((nb, H, dk, dv), F32)],
        scratch_shapes=[pltpu.VMEM((hpb, dk, dv), F32)],
        input_output_aliases=aliases,
        compiler_params=_cp(("parallel", "parallel", "arbitrary")),
        name="retention",
    )(*args)


def _c_out_kernel(x_ref, o_ref, w_ref, y_ref):
    y_ref[...] = x_ref[...] + _dot(o_ref[...], w_ref[...])


def _c_out(x, o, w, li, tm, tn):
    T, D = x.shape
    K = o.shape[1]
    return pl.pallas_call(
        _c_out_kernel,
        grid=(pl.cdiv(T, tm), D // tn),
        in_specs=[pl.BlockSpec((tm, tn), lambda i, j: (i, j)), pl.BlockSpec((tm, K), lambda i, j: (i, 0)),
                  pl.BlockSpec((None, K, tn), lambda i, j: (li, 0, j))],
        out_specs=pl.BlockSpec((tm, tn), lambda i, j: (i, j)),
        out_shape=jax.ShapeDtypeStruct((T, D), F32),
        compiler_params=_cp(("parallel", "arbitrary")),
        name="c_out",
    )(x, o, w)


def _rope_tables(pos, half):
    inv = ROPE_THETA ** (-jnp.arange(half, dtype=F32) / half)
    ang = pos.astype(F32)[:, None] * inv[None, :]
    return jnp.cos(ang), jnp.sin(ang)


def _swap_halves(w):
    half = w.shape[-1] // 2
    return jnp.concatenate([w[..., half:], w[..., :half]], axis=-1)


def _prep_w_in_ab(w):
    o1 = Q_LORA + KV_LORA
    o2 = o1 + MLA_ROPE
    kr = w[:, o1:o2]
    ks = _swap_halves(kr)
    return jnp.concatenate([w[:, :o1], kr, ks, ks, kr, w[:, o2:]], axis=1).astype(BF16)


def _prep_w_q_b(w):
    w = w.reshape(Q_LORA, MLA_HEADS, MLA_NOPE + MLA_ROPE)
    rope = w[..., MLA_NOPE:]
    swap = _swap_halves(rope)
    out = jnp.concatenate([w[..., :MLA_NOPE], rope, swap, swap, rope], axis=-1)
    return out.reshape(Q_LORA, -1).astype(BF16)


def _prep_w_kv_b(w):
    w = w.reshape(KV_LORA, MLA_HEADS, MLA_NOPE + MLA_V)
    return jnp.concatenate([w[..., :MLA_NOPE].reshape(KV_LORA, -1),
                            w[..., MLA_NOPE:].reshape(KV_LORA, -1)], axis=1).astype(BF16)


def kernel(x_prompt, x_sample, cache_mla_ckv, cache_mla_krope, state_s5_re, state_s5_im, state_ret,
           norm_mix, norm_mlp, norm_final, w_in_ab, q_a_norm, kv_a_norm, w_q_b, w_kv_b,
           s5_lam_re, s5_lam_im, s5_log_dt, s5_b_re, s5_b_im, s5_c_re, s5_c_im, s5_d, w_glu, b_glu,
           w_out_ab, w_in_c, ret_gn, w_out_c, w_up, w_down):
    BP, SP, D = x_prompt.shape
    BS, SS, _ = x_sample.shape
    past = cache_mla_ckv.shape[2]
    depth = norm_mix.shape[0]
    G, N = s5_lam_re.shape[1:]
    P, L = S5_GROUP, S5_L
    dk, dv = state_ret.shape[3:]
    TP, TS = BP * SP, BS * SS
    T = TP + TS

    x = (x_prompt.reshape(TP, D), x_sample.reshape(TS, D))
    pos = jnp.concatenate([jnp.arange(SP), jnp.tile(past + jnp.arange(SS), WIDE_TM // SS)])

    c32, s32 = _rope_tables(pos, MLA_ROPE // 2)
    z64 = jnp.zeros((pos.shape[0], LANE - MLA_ROPE), F32)
    cs1 = jnp.concatenate([c32, c32, z64], axis=1)
    cs2 = jnp.concatenate([-s32, s32, z64], axis=1)
    cos_r, sin_r = _rope_tables(pos, dk // 2)

    skp = -(-(past + SS) // LANE) * LANE
    w_glu_b, w_out_ab_b, w_in_c_b, w_out_c_b, w_up_b, w_down_b = (
        w.astype(BF16) for w in (w_glu, w_out_ab, w_in_c, w_out_c, w_up, w_down))
    ckv_p, kr_p, s5r_p, s5i_p, ret_p = [], [], [], [], []
    ckv_s, kr_s, s5r_s, s5i_s, ret_s = [], [], [], [], []

    for layer in range(depth):
        i = layer // 2
        g_mix = norm_mix[layer].reshape(1, D)
        if layer % 2 == 0:
            xs = x if isinstance(x, tuple) else (x,)
            q, ckv, kr, u, ub_p, ub_s = _ab_in(
                xs, g_mix, _prep_w_in_ab(w_in_ab[i]), q_a_norm[i].reshape(1, -1),
                kv_a_norm[i].reshape(1, -1), _prep_w_q_b(w_q_b[i]), cs1, cs2, AB_TM, TP, SP)
            ckv_p.append(ckv[:TP].reshape(BP, SP, KV_LORA))
            kr_p.append(kr[:TP, :MLA_ROPE].reshape(BP, SP, MLA_ROPE))
            ckv_s.append(ckv[TP:].reshape(BS, SS, KV_LORA))
            kr_s.append(kr[TP:, :MLA_ROPE].reshape(BS, SS, MLA_ROPE))
            pad = skp - past - SS
            ckv_hist = jnp.concatenate([cache_mla_ckv[i], ckv[TP:].reshape(BS, SS, KV_LORA),
                                        jnp.zeros((BS, pad, KV_LORA), F32)], axis=1)
            kr_past = jnp.pad(cache_mla_krope[i], ((0, 0), (0, 0), (0, LANE - MLA_ROPE)))
            kr_hist = jnp.concatenate([kr_past, kr[TP:].reshape(BS, SS, LANE),
                                       jnp.zeros((BS, pad, LANE), F32)], axis=1)
            w_kv = _prep_w_kv_b(w_kv_b[i])
            k_p, v_p = _kv_proj(ckv, kr, w_kv, TM)
            k_s, v_s = _kv_proj(ckv_hist.reshape(BS * skp, KV_LORA), kr_hist.reshape(BS * skp, LANE),
                                w_kv, skp)
            attn = (_attn_long(q, k_p, v_p, BP, SP, ATT_TK),
                    _attn_short(q, k_s, v_s, BS, SS, skp, TP, 0, past, past + SS))
            params = _s5_params(s5_lam_re[i], s5_lam_im[i], s5_log_dt[i], s5_b_re[i], s5_b_im[i],
                                s5_c_re[i], s5_c_im[i])
            ncp, ncs = SP // L, SS // L
            u_p = ub_p.reshape(BP, ncp, L, G, P).transpose(3, 1, 0, 2, 4).reshape(G, ncp * BP, L * P)
            u_s = ub_s.reshape(BS, ncs, L, G, P).transpose(3, 1, 0, 2, 4).reshape(G, ncs * BS, L * P)
            y_p, hr_p, hi_p = _s5_main(u_p, params, None, BP, ncp)
            h0 = (jnp.swapaxes(state_s5_re[i], 0, 1), jnp.swapaxes(state_s5_im[i], 0, 1))
            y_s, hr_s, hi_s = _s5_main(u_s, params, h0, BS, ncs)
            y_p = y_p.reshape(G, ncp, BP, L, P).transpose(2, 1, 3, 0, 4).reshape(TP, G * P)
            y_s = y_s.reshape(G, ncs, BS, L, P).transpose(2, 1, 3, 0, 4).reshape(TS, G * P)
            s5r_p.append(jnp.swapaxes(hr_p, 0, 1))
            s5i_p.append(jnp.swapaxes(hi_p, 0, 1))
            s5r_s.append(jnp.swapaxes(hr_s, 0, 1))
            s5i_s.append(jnp.swapaxes(hi_s, 0, 1))
            x = _ab_out(xs, attn, (y_p, y_s), u, s5_d[i].reshape(1, G * P), w_glu_b,
                        b_glu[i].reshape(1, -1), w_out_ab_b, i, AB_TM)
        else:
            proj = _c_in(x, g_mix, w_in_c_b, i, cos_r, sin_r, dk, dv, WIDE_TM, C_IN_TN, TP, (RET_C, SS), SP)
            gn = ret_gn[i].reshape(1, -1)
            o, st_p = _retention(proj, gn, None, None, BP, SP, RET_C, 0, dk, dv, RET_HPB)
            o, st_s = _retention(proj, gn, state_ret[i], o, BS, SS, SS, TP, dk, dv, RET_HPB)
            ret_p.append(st_p)
            ret_s.append(st_s)
            x = _c_out(x, o, w_out_c_b, i, WIDE_TM, C_OUT_TN)
        last = layer == depth - 1
        outs = _mlp(x, norm_mlp[layer].reshape(1, D), w_up_b, w_down_b, layer,
                    norm_final.reshape(1, D), last, (TP, TS) if last else (T,), TM, MLP_TF)
        x = outs[0]

    return (outs[0].reshape(BP, SP, D), outs[1].reshape(BS, SS, D),
            jnp.stack(ckv_p), jnp.stack(kr_p), jnp.stack(s5r_p), jnp.stack(s5i_p), jnp.stack(ret_p),
            jnp.stack(ckv_s), jnp.stack(kr_s), jnp.stack(s5r_s), jnp.stack(s5i_s), jnp.stack(ret_s))
```

```python
import functools
import math

import jax
import jax.numpy as jnp
import numpy as np
from jax import lax
from jax.experimental import pallas as pl
from jax.experimental.pallas import tpu as pltpu

F32 = jnp.float32
BF16 = jnp.bfloat16

CHUNK = 64
MLA_V = 128
MLA_NOPE = 128
MLA_ROPE = 64
MLA_HEADS = 8
Q_LORA = 512
KV_LORA = 256
MLA_SCALE = (MLA_NOPE + MLA_ROPE) ** -0.5
S5_GROUP = 16
S5_STATE = 64
RET_HEADS = 8
ROPE_THETA = 10000.0
EPS = 1e-6
GN_EPS = 1e-5
LOG2E = 1.4426950408889634

LANE = 128
QK_PAD = 256
V_PAD = 256
S5_L = 64
TM = 512
WIDE_TM = 1024
AB_TM = 256
C_IN_TN = 2048
C_OUT_TN = 512
RET_C = 256
RET_HPB = 8
MLP_TF = 1024
MLP_TM = 640
ATT_TK = 512
NEG_BIG = -1e30
V7X_VMEM_BYTES = 64 * 2 ** 20
VMEM_LIMIT = V7X_VMEM_BYTES * 7 // 8


def _cp(sem):
    return pltpu.CompilerParams(dimension_semantics=sem, vmem_limit_bytes=VMEM_LIMIT)


def _rms(x, g):
    return x * lax.rsqrt(jnp.mean(x * x, axis=-1, keepdims=True) + EPS) * g


def _dot(a, b):
    return jnp.dot(a, b, preferred_element_type=F32)


def _dot_nt(a, b):
    return lax.dot_general(a, b, (((1,), (1,)), ((), ())), preferred_element_type=F32)


def _dot_tn(a, b):
    return lax.dot_general(a, b, (((0,), (0,)), ((), ())), preferred_element_type=F32)


def _parts_specs(rows, tm, width, row_axis=0):
    assert all(r % tm == 0 for r in rows[:-1])
    specs, starts, start = [], [], 0
    for r in rows:
        nblk = pl.cdiv(r, tm)

        def index_map(*g, start=start, nblk=nblk):
            return (jnp.clip(g[row_axis] - start, 0, nblk - 1), 0)

        specs.append(pl.BlockSpec((tm, width), index_map))
        starts.append(start)
        start += nblk
    return specs, tuple(starts)


def _parts_load(refs, starts, i):
    x = refs[0][...]
    for ref, s in zip(refs[1:], starts[1:]):
        x = jnp.where(i >= s, ref[...], x)
    return x


def _parts_store(refs, starts, i, val):
    ends = starts[1:] + (None,)
    for ref, s, e in zip(refs, starts, ends):
        cond = i >= s if e is None else (i >= s) & (i < e)

        @pl.when(cond)
        def _(ref=ref):
            ref[...] = val


def _ab_in_kernel(*refs, x_starts, ub_starts):
    n = len(x_starts)
    x_refs = refs[:n]
    (g_ref, win_ref, qn_ref, kvn_ref, wq_ref, cs1_ref, cs2_ref,
     q_out, ckv_out, kr_out, u_out) = refs[n:n + 11]
    ub_outs = refs[n + 11:]
    x = _parts_load(x_refs, x_starts, pl.program_id(0))
    xn = _rms(x, g_ref[...]).astype(BF16)
    proj = _dot(xn, win_ref[...])
    cqn = _rms(proj[:, :Q_LORA], qn_ref[...]).astype(BF16)
    o = Q_LORA
    ckv_out[...] = _rms(proj[:, o:o + KV_LORA], kvn_ref[...])
    o += KV_LORA
    cs1 = cs1_ref[...]
    cs2 = cs2_ref[...]
    kr_out[...] = proj[:, o:o + LANE] * cs1 + proj[:, o + LANE:o + 2 * LANE] * cs2
    o += 2 * LANE
    u_out[...] = proj[:, o:]
    _parts_store(ub_outs, ub_starts, pl.program_id(0), proj[:, o:].astype(BF16))
    q = _dot(cqn, wq_ref[...])
    qs = MLA_SCALE * LOG2E
    for h in range(MLA_HEADS):
        b = h * 3 * LANE
        q_out[:, h * QK_PAD:h * QK_PAD + LANE] = (q[:, b:b + LANE] * qs).astype(BF16)
        rope = q[:, b + LANE:b + 2 * LANE] * cs1 + q[:, b + 2 * LANE:b + 3 * LANE] * cs2
        q_out[:, h * QK_PAD + LANE:(h + 1) * QK_PAD] = (rope * qs).astype(BF16)


def _pos_table_spec(tm, width, long_rows, seq_len, axis=0):
    assert seq_len % tm == 0 and long_rows % tm == 0
    n_long, per_seq = long_rows // tm, seq_len // tm

    def index_map(*g):
        return (jnp.where(g[axis] < n_long, g[axis] % per_seq, per_seq), 0)

    return pl.BlockSpec((tm, width), index_map)


def _ab_in(xs, g, win, qn, kvn, wq, cs1, cs2, tm, long_rows, seq_len):
    D = xs[0].shape[1]
    T = sum(x.shape[0] for x in xs)
    row = lambda i: (i, 0)
    fix = lambda i: (0, 0)
    s5w = win.shape[1] - Q_LORA - KV_LORA - 2 * LANE
    x_specs, x_starts = _parts_specs([x.shape[0] for x in xs], tm, D)
    ub_rows = (long_rows, T - long_rows)
    ub_specs, ub_starts = _parts_specs(ub_rows, tm, s5w)
    return pl.pallas_call(
        functools.partial(_ab_in_kernel, x_starts=x_starts, ub_starts=ub_starts),
        grid=(T // tm,),
        in_specs=x_specs + [pl.BlockSpec((1, D), fix),
                  pl.BlockSpec(win.shape, fix), pl.BlockSpec((1, Q_LORA), fix),
                  pl.BlockSpec((1, KV_LORA), fix), pl.BlockSpec(wq.shape, fix),
                  _pos_table_spec(tm, LANE, long_rows, seq_len), _pos_table_spec(tm, LANE, long_rows, seq_len)],
        out_specs=[pl.BlockSpec((tm, MLA_HEADS * QK_PAD), row), pl.BlockSpec((tm, KV_LORA), row),
                   pl.BlockSpec((tm, LANE), row), pl.BlockSpec((tm, s5w), row)] + ub_specs,
        out_shape=[jax.ShapeDtypeStruct((T, MLA_HEADS * QK_PAD), BF16),
                   jax.ShapeDtypeStruct((T, KV_LORA), F32),
                   jax.ShapeDtypeStruct((T, LANE), F32),
                   jax.ShapeDtypeStruct((T, s5w), F32)]
        + [jax.ShapeDtypeStruct((r, s5w), BF16) for r in ub_rows],
        compiler_params=_cp(("arbitrary",)),
        name="ab_in",
    )(*xs, g, win, qn, kvn, wq, cs1, cs2)


def _kv_kernel(ckv_ref, kr_ref, w_ref, k_out, v_out):
    kv = _dot(ckv_ref[...].astype(BF16), w_ref[...])
    kr = kr_ref[...].astype(BF16)
    v0 = MLA_HEADS * MLA_NOPE
    ones = jnp.ones((kv.shape[0], V_PAD - MLA_V), BF16)
    for h in range(MLA_HEADS):
        k_out[:, h * QK_PAD:h * QK_PAD + LANE] = kv[:, h * LANE:(h + 1) * LANE].astype(BF16)
        k_out[:, h * QK_PAD + LANE:(h + 1) * QK_PAD] = kr
        v_out[:, h * V_PAD:h * V_PAD + MLA_V] = kv[:, v0 + h * MLA_V:v0 + (h + 1) * MLA_V].astype(BF16)
        v_out[:, h * V_PAD + MLA_V:(h + 1) * V_PAD] = ones


def _kv_proj(ckv, kr, w, tm):
    R = ckv.shape[0]
    row = lambda i: (i, 0)
    return pl.pallas_call(
        _kv_kernel,
        grid=(R // tm,),
        in_specs=[pl.BlockSpec((tm, KV_LORA), row), pl.BlockSpec((tm, LANE), row),
                  pl.BlockSpec(w.shape, lambda i: (0, 0))],
        out_specs=[pl.BlockSpec((tm, MLA_HEADS * QK_PAD), row),
                   pl.BlockSpec((tm, MLA_HEADS * V_PAD), row)],
        out_shape=[jax.ShapeDtypeStruct((R, MLA_HEADS * QK_PAD), BF16),
                   jax.ShapeDtypeStruct((R, MLA_HEADS * V_PAD), BF16)],
        compiler_params=_cp(("parallel",)),
        name="kv_proj",
    )(ckv, kr, w)


def _attn_long_kernel(q_ref, k_ref, v_ref, o_ref, s_sc, m_sc, acc_sc, *, tq, tk):
    qi = pl.program_id(2)

    def scores_to(slot, kt, r0=0):
        s_sc[slot, r0:, :] = _dot_nt(q_ref[r0:, :], k_ref[pl.ds(pl.multiple_of(kt * tk, tk), tk), :])

    def consume(slot, kt, masked, r0=0):
        s = s_sc[slot, r0:, :]
        if masked:
            qpos = qi * tq + r0 + lax.broadcasted_iota(jnp.int32, (tq - r0, tk), 0)
            kpos = kt * tk + lax.broadcasted_iota(jnp.int32, (tq - r0, tk), 1)
            s = jnp.where(kpos // CHUNK <= qpos // CHUNK, s, NEG_BIG)
        m = m_sc[r0:, :]
        m_new = jnp.maximum(m, jnp.max(s, axis=1, keepdims=True))
        alpha = jnp.exp2(m - m_new)
        p = jnp.exp2(s - jnp.concatenate([m_new] * (tk // LANE), axis=1))
        m_sc[r0:, :] = m_new
        v = v_ref[pl.ds(pl.multiple_of(kt * tk, tk), tk), :]
        acc_sc[r0:, :] = (jnp.concatenate([alpha] * (V_PAD // LANE), axis=1) * acc_sc[r0:, :]
                          + _dot(p.astype(BF16), v))

    m_sc[...] = jnp.full_like(m_sc, NEG_BIG)
    acc_sc[...] = jnp.zeros_like(acc_sc)
    scores_to(0, 0)

    def pair(j, carry):
        scores_to(1, 2 * j + 1)
        consume(0, 2 * j, False)
        scores_to(0, 2 * j + 2)
        consume(1, 2 * j + 1, False)
        return carry

    lax.fori_loop(0, qi, pair, 0)
    scores_to(1, 2 * qi + 1, tk)
    consume(0, 2 * qi, True)
    consume(1, 2 * qi + 1, True, tk)
    acc = acc_sc[...]
    o_ref[...] = (acc[:, :MLA_V] / acc[:, MLA_V:]).astype(BF16)


def _attn_long(q, k, v, nb, s, tk):
    tq = 2 * tk
    assert s % tq == 0 and V_PAD == 2 * MLA_V
    nq = s // tq
    return pl.pallas_call(
        functools.partial(_attn_long_kernel, tq=tq, tk=tk),
        grid=(nb, MLA_HEADS, nq),
        in_specs=[pl.BlockSpec((tq, QK_PAD), lambda b, h, i: (b * nq + i, h)),
                  pl.BlockSpec((s, QK_PAD), lambda b, h, i: (b, h)),
                  pl.BlockSpec((s, V_PAD), lambda b, h, i: (b, h))],
        out_specs=pl.BlockSpec((tq, MLA_V), lambda b, h, i: (b * nq + i, h)),
        out_shape=jax.ShapeDtypeStruct((nb * s, MLA_HEADS * MLA_V), BF16),
        scratch_shapes=[pltpu.VMEM((2, tq, tk), F32), pltpu.VMEM((tq, LANE), F32),
                        pltpu.VMEM((tq, V_PAD), F32)],
        compiler_params=_cp(("parallel", "parallel", "arbitrary")),
        name="attn_long",
    )(q, k, v)


def _attn_short_kernel(q_ref, k_ref, v_ref, o_ref, *, q_pos0, n_keys):
    sq = q_ref.shape[0]
    skp = k_ref.shape[0]
    s = _dot_nt(q_ref[...], k_ref[...])
    qpos = q_pos0 + lax.broadcasted_iota(jnp.int32, (sq, skp), 0)
    kpos = lax.broadcasted_iota(jnp.int32, (sq, skp), 1)
    ok = (kpos // CHUNK <= qpos // CHUNK) & (kpos < n_keys)
    s = jnp.where(ok, s, NEG_BIG)
    m = jnp.max(s, axis=1, keepdims=True)
    p = jnp.exp2(s - m)
    acc = _dot(p.astype(BF16), v_ref[...])
    o_ref[...] = (acc[:, :MLA_V] / acc[:, MLA_V:]).astype(BF16)


def _attn_short(q, k, v, nb, sq, skp, q_row0, k_row0, q_pos0, n_keys):
    qb0 = q_row0 // sq
    kb0 = k_row0 // skp
    return pl.pallas_call(
        functools.partial(_attn_short_kernel, q_pos0=q_pos0, n_keys=n_keys),
        grid=(nb, MLA_HEADS),
        in_specs=[pl.BlockSpec((sq, QK_PAD), lambda b, h: (qb0 + b, h)),
                  pl.BlockSpec((skp, QK_PAD), lambda b, h: (kb0 + b, h)),
                  pl.BlockSpec((skp, V_PAD), lambda b, h: (kb0 + b, h))],
        out_specs=pl.BlockSpec((sq, MLA_V), lambda b, h: (b, h)),
        out_shape=jax.ShapeDtypeStruct((nb * sq, MLA_HEADS * MLA_V), BF16),
        compiler_params=_cp(("parallel", "parallel")),
        name="attn_short",
    )(q, k, v)


def _s5_param_kernel(lr_ref, li_ref, ldt_ref, btr_ref, bti_ref, cr_ref, ci_ref,
                     m_out, wr_out, wi_out, vr_out, vi_out, alr_out, ali_out,
                     car_sc, cai_sc):
    L, P = S5_L, S5_GROUP
    lr = jnp.minimum(lr_ref[...], -1e-4)
    li = li_ref[...]
    dt = jnp.exp(ldt_ref[...])
    mag = jnp.exp(lr * dt)
    a_re = mag * jnp.cos(li * dt)
    a_im = mag * jnp.sin(li * dt)
    den = lr * lr + li * li
    f_re = ((a_re - 1.0) * lr + a_im * li) / den
    f_im = (a_im * lr - (a_re - 1.0) * li) / den
    btr = btr_ref[...]
    bti = bti_ref[...]
    bb_re = f_re * btr - f_im * bti
    bb_im = f_re * bti + f_im * btr
    cr = cr_ref[...]
    ci = ci_ref[...]

    n_pow = -(-(L + 1) // 8) * 8
    t = lax.broadcasted_iota(jnp.int32, (n_pow, 1), 0).astype(F32)
    pmag = jnp.exp((lr * dt) * t)
    pang = (li * dt) * t
    pw_re = pmag * jnp.cos(pang)
    pw_im = pmag * jnp.sin(pang)

    def apow(t):
        return pw_re[t:t + 1, :], pw_im[t:t + 1, :]

    for i in range(L):
        rows = slice(i * P, (i + 1) * P)
        pr, pi = apow(L - 1 - i)
        wr_out[rows, :] = (pr * bb_re - pi * bb_im).astype(BF16)
        wi_out[rows, :] = (pr * bb_im + pi * bb_re).astype(BF16)
        pr, pi = apow(i + 1)
        vr_out[rows, :] = (cr * pr - ci * pi).astype(BF16)
        vi_out[rows, :] = (-(cr * pi + ci * pr)).astype(BF16)
        pr, pi = apow(i)
        car_sc[rows, :] = cr * pr - ci * pi
        cai_sc[rows, :] = cr * pi + ci * pr
    pr, pi = apow(L)
    alr_out[...] = pr
    ali_out[...] = pi
    hp = lax.Precision.HIGHEST
    nt = (((1,), (1,)), ((), ()))
    krow = (lax.dot_general(bb_re, car_sc[...], nt, precision=hp, preferred_element_type=F32)
            - lax.dot_general(bb_im, cai_sc[...], nt, precision=hp, preferred_element_type=F32))
    lane = lax.broadcasted_iota(jnp.int32, (P, L * P), 1)
    for i in range(L):
        shifted = krow if i == 0 else pltpu.roll(krow, i * P, axis=1)
        m_out[i * P:(i + 1) * P, :] = jnp.where(lane >= i * P, shifted, 0.0).astype(BF16)


def _s5_params(lam_re, lam_im, log_dt, b_re, b_im, c_re, c_im):
    G, N = lam_re.shape
    P, L = S5_GROUP, S5_L
    LP = L * P
    v3 = lambda a: a.reshape(G, 1, N)
    ldt = jnp.broadcast_to(log_dt[:, None, None], (G, 1, N))
    btr = jnp.swapaxes(b_re, 1, 2)
    bti = jnp.swapaxes(b_im, 1, 2)
    vec = pl.BlockSpec((None, 1, N), lambda g: (g, 0, 0))
    mat = pl.BlockSpec((None, P, N), lambda g: (g, 0, 0))
    big = pl.BlockSpec((None, LP, N), lambda g: (g, 0, 0))
    return pl.pallas_call(
        _s5_param_kernel,
        grid=(G,),
        in_specs=[vec, vec, vec, mat, mat, mat, mat],
        out_specs=[pl.BlockSpec((None, LP, LP), lambda g: (g, 0, 0)), big, big, big, big, vec, vec],
        out_shape=[jax.ShapeDtypeStruct((G, LP, LP), BF16)]
        + [jax.ShapeDtypeStruct((G, LP, N), BF16)] * 4
        + [jax.ShapeDtypeStruct((G, 1, N), F32)] * 2,
        scratch_shapes=[pltpu.VMEM((LP, N), F32), pltpu.VMEM((LP, N), F32)],
        compiler_params=_cp(("parallel",)),
        name="s5_params",
    )(v3(lam_re), v3(lam_im), ldt, btr, bti, c_re, c_im)


def _s5_main_kernel(*refs, nseq, nchunk, has_state):
    if has_state:
        (u_ref, m_ref, wr_ref, wi_ref, vr_ref, vi_ref, alr_ref, ali_ref, h0r_ref, h0i_ref,
         y_out, hfr_out, hfi_out, xr_sc, xi_sc, hr_sc, hi_sc) = refs
    else:
        (u_ref, m_ref, wr_ref, wi_ref, vr_ref, vi_ref, alr_ref, ali_ref,
         y_out, hfr_out, hfi_out, xr_sc, xi_sc, hr_sc, hi_sc) = refs
    ub = u_ref[...]
    xr_sc[...] = _dot(ub, wr_ref[...])
    xi_sc[...] = _dot(ub, wi_ref[...])
    ar = alr_ref[...]
    ai = ali_ref[...]
    n = ar.shape[-1]
    if has_state:
        init = (h0r_ref[...], h0i_ref[...])
    else:
        init = (jnp.zeros((nseq, n), F32), jnp.zeros((nseq, n), F32))

    def body(c, carry):
        sr, si = carry
        rows = pl.ds(pl.multiple_of(c * nseq, nseq), nseq)
        hr_sc[rows, :] = sr
        hi_sc[rows, :] = si
        return (ar * sr - ai * si + xr_sc[rows, :], ar * si + ai * sr + xi_sc[rows, :])

    sr, si = lax.fori_loop(0, nchunk, body, init)
    hfr_out[...] = sr
    hfi_out[...] = si
    y_out[...] = (_dot(ub, m_ref[...])
                  + _dot_nt(hr_sc[...].astype(BF16), vr_ref[...])
                  + _dot_nt(hi_sc[...].astype(BF16), vi_ref[...])).astype(y_out.dtype)


def _s5_main(u, params, h0, nseq, nchunk):
    m, wr, wi, vr, vi, alr, ali = params
    G, R, LP = u.shape
    N = alr.shape[-1]
    has_state = h0 is not None
    g3 = lambda g: (g, 0, 0)
    big = pl.BlockSpec((None, LP, N), g3)
    vec = pl.BlockSpec((None, 1, N), g3)
    st = pl.BlockSpec((None, nseq, N), g3)
    in_specs = [pl.BlockSpec((None, R, LP), g3), pl.BlockSpec((None, LP, LP), g3),
                big, big, big, big, vec, vec]
    args = [u, m, wr, wi, vr, vi, alr, ali]
    if has_state:
        in_specs += [st, st]
        args += list(h0)
    return pl.pallas_call(
        functools.partial(_s5_main_kernel, nseq=nseq, nchunk=nchunk, has_state=has_state),
        grid=(G,),
        in_specs=in_specs,
        out_specs=[pl.BlockSpec((None, R, LP), g3), st, st],
        out_shape=[jax.ShapeDtypeStruct((G, R, LP), BF16),
                   jax.ShapeDtypeStruct((G, nseq, N), F32),
                   jax.ShapeDtypeStruct((G, nseq, N), F32)],
        scratch_shapes=[pltpu.VMEM((R, N), F32)] * 4,
        compiler_params=_cp(("parallel",)),
        name="s5_main",
    )(*args)


def _ab_out_kernel(*refs, starts):
    i = pl.program_id(0)
    parts = []
    for st in starts:
        parts.append(_parts_load(refs[:len(st)], st, i))
        refs = refs[len(st):]
    x, attn, y = parts
    u_ref, d_ref, wglu_ref, bglu_ref, wo_ref, o_ref = refs
    y = y.astype(F32) + d_ref[...] * u_ref[...]
    z = y * (0.5 * (1.0 + jnp.tanh(math.sqrt(2.0 / math.pi) * (y + 0.044715 * (y * y * y)))))
    gate = 1.0 / (1.0 + jnp.exp(-(_dot(z.astype(BF16), wglu_ref[...]) + bglu_ref[...])))
    ssm = (z * gate).astype(BF16)
    na = attn.shape[1]
    o_ref[...] = x + _dot(attn, wo_ref[:na, :]) + _dot(ssm, wo_ref[na:, :])


def _ab_out(xs, attns, ys, u, d, wglu, bglu, wo, li, tm):
    D = xs[0].shape[1]
    T = sum(x.shape[0] for x in xs)
    row = lambda i: (i, 0)
    fix = lambda i: (0, 0)
    lay = lambda i: (li, 0, 0)
    specs, starts = [], []
    for parts in (xs, attns, ys):
        sp, st = _parts_specs([p.shape[0] for p in parts], tm, parts[0].shape[1])
        specs += sp
        starts.append(st)
    return pl.pallas_call(
        functools.partial(_ab_out_kernel, starts=tuple(starts)),
        grid=(T // tm,),
        in_specs=specs + [pl.BlockSpec((tm, u.shape[1]), row), pl.BlockSpec(d.shape, fix),
                          pl.BlockSpec((None,) + wglu.shape[1:], lay), pl.BlockSpec(bglu.shape, fix),
                          pl.BlockSpec((None,) + wo.shape[1:], lay)],
        out_specs=pl.BlockSpec((tm, D), row),
        out_shape=jax.ShapeDtypeStruct((T, D), F32),
        compiler_params=_cp(("parallel",)),
        name="ab_out",
    )(*xs, *attns, *ys, u, d, wglu, bglu, wo)


def _mlp_kernel(x_ref, g_ref, wup_ref, wdn_ref, gf_ref, *rest, final_norm, o_starts):
    n_out = len(o_starts)
    o_refs = rest[:n_out]
    xn_sc = rest[n_out]
    acc = rest[n_out + 1] if n_out > 1 else o_refs[0]
    f = pl.program_id(1)

    @pl.when(f == 0)
    def _():
        xn_sc[...] = _rms(x_ref[...], g_ref[...]).astype(BF16)
        acc[...] = jnp.zeros_like(acc)

    a = jnp.maximum(_dot(xn_sc[...], wup_ref[...]), 0.0)
    acc[...] += _dot((a * a).astype(BF16), wdn_ref[...])

    @pl.when(f == pl.num_programs(1) - 1)
    def _():
        r = x_ref[...] + acc[...]
        if final_norm:
            r = _rms(r, gf_ref[...])
        if n_out > 1:
            _parts_store(o_refs, o_starts, pl.program_id(0), r)
        else:
            acc[...] = r


def _mlp(x, g, wup, wdn, li, gf, final_norm, out_rows, tm, tf):
    T, D = x.shape
    FF = wup.shape[2]
    assert sum(out_rows) == T
    o_specs, o_starts = _parts_specs(out_rows, tm, D)
    return pl.pallas_call(
        functools.partial(_mlp_kernel, final_norm=final_norm, o_starts=o_starts),
        grid=(pl.cdiv(T, tm), FF // tf),
        in_specs=[pl.BlockSpec((tm, D), lambda i, f: (i, 0)), pl.BlockSpec((1, D), lambda i, f: (0, 0)),
                  pl.BlockSpec((None, D, tf), lambda i, f: (li, 0, f)),
                  pl.BlockSpec((None, tf, D), lambda i, f: (li, f, 0)),
                  pl.BlockSpec((1, D), lambda i, f: (0, 0))],
        out_specs=o_specs,
        out_shape=[jax.ShapeDtypeStruct((r, D), F32) for r in out_rows],
        scratch_shapes=[pltpu.VMEM((tm, D), BF16)] + [pltpu.VMEM((tm, D), F32)] * (len(out_rows) > 1),
        compiler_params=_cp(("arbitrary" if len(out_rows) > 1 else "parallel", "arbitrary")),
        name="mlp",
    )(x, g, wup, wdn, gf)


def _ret_log_decay(head):
    hf = jnp.full((1, 1), head, jnp.int32).astype(F32)
    return jnp.log(1.0 - jnp.exp2(-5.0 - hf))


def _c_in_kernel(x_ref, g_ref, w_ref, cos_ref, sin_ref, o_ref, xn_sc, *, n_rot, n_q, n_lin, dk,
                 chunk_split_row, chunk_lens):
    i = pl.program_id(0)
    j = pl.program_id(1)

    @pl.when(j == 0)
    def _():
        xn_sc[...] = _rms(x_ref[...], g_ref[...]).astype(BF16)

    tm = x_ref.shape[0]
    tn = w_ref.shape[1]
    half = dk // 2

    def chunks():
        for s in range(0, tn, dk):
            yield s, _dot(xn_sc[...], w_ref[:, s:s + dk])

    @pl.when(j < n_rot)
    def _():
        cos = cos_ref[...]
        sin = sin_ref[...]
        c_len = jnp.where(i * tm >= chunk_split_row, chunk_lens[1], chunk_lens[0])
        pos = (lax.broadcasted_iota(jnp.int32, (tm, 1), 0) & (c_len - 1)).astype(F32) + 1.0
        sign = jnp.where(j < n_q, 1.0, -1.0)
        scale = jnp.where(j < n_q, 1.0, dk ** -0.5)
        for s, acc in chunks():
            lg = _ret_log_decay(((j * tn + s) // dk) % RET_HEADS)
            dec = jnp.exp((sign * lg) * pos) * scale
            x1 = acc[:, :half]
            x2 = acc[:, half:]
            o_ref[:, s:s + half] = ((x1 * cos - x2 * sin) * dec).astype(BF16)
            o_ref[:, s + half:s + dk] = ((x1 * sin + x2 * cos) * dec).astype(BF16)

    @pl.when((j >= n_rot) & (j < n_lin))
    def _():
        for s, acc in chunks():
            o_ref[:, s:s + dk] = acc.astype(BF16)

    @pl.when(j >= n_lin)
    def _():
        for s, acc in chunks():
            o_ref[:, s:s + dk] = (acc / (1.0 + jnp.exp(-acc))).astype(BF16)


def _c_in(x, g, w, li, cos, sin, dk, dv, tm, tn, chunk_split_row, chunk_lens, seq_len):
    T, D = x.shape
    N = w.shape[2]
    qk = RET_HEADS * dk
    n_q = qk // tn
    n_rot = 2 * qk // tn
    n_lin = (2 * qk + RET_HEADS * dv) // tn
    assert chunk_split_row % tm == 0
    assert all(tm % c == 0 and c & (c - 1) == 0 for c in chunk_lens)
    return pl.pallas_call(
        functools.partial(_c_in_kernel, n_rot=n_rot, n_q=n_q, n_lin=n_lin, dk=dk,
                          chunk_split_row=chunk_split_row, chunk_lens=chunk_lens),
        grid=(pl.cdiv(T, tm), N // tn),
        in_specs=[pl.BlockSpec((tm, D), lambda i, j: (i, 0)), pl.BlockSpec((1, D), lambda i, j: (0, 0)),
                  pl.BlockSpec((None, D, tn), lambda i, j: (li, 0, j)),
                  _pos_table_spec(tm, dk // 2, chunk_split_row, seq_len),
                  _pos_table_spec(tm, dk // 2, chunk_split_row, seq_len)],
        out_specs=pl.BlockSpec((tm, tn), lambda i, j: (i, j)),
        out_shape=jax.ShapeDtypeStruct((T, N), BF16),
        scratch_shapes=[pltpu.VMEM((tm, D), BF16)],
        compiler_params=_cp(("parallel", "arbitrary")),
        name="c_in",
    )(x, g, w, cos, sin)


def _ret_kernel(*refs, c_len, has_state, hpb, dk, dv):
    if has_state:
        q_ref, k_ref, v_ref, g_ref, gn_ref, s0_ref, o_ref, sf_ref, st_sc = refs
    else:
        q_ref, k_ref, v_ref, g_ref, gn_ref, o_ref, sf_ref, st_sc = refs
    hb = pl.program_id(1)
    c = pl.program_id(2)

    @pl.when(c == 0)
    def _():
        if has_state:
            st_sc[...] = s0_ref[...]
        else:
            st_sc[...] = jnp.zeros_like(st_sc)

    row = lax.broadcasted_iota(jnp.int32, (c_len, c_len), 0)
    col = lax.broadcasted_iota(jnp.int32, (c_len, c_len), 1)
    causal = row >= col
    for hh in range(hpb):
        q = q_ref[:, hh * dk:(hh + 1) * dk]
        k = k_ref[:, hh * dk:(hh + 1) * dk]
        v = v_ref[:, hh * dv:(hh + 1) * dv]
        inner = _dot(jnp.where(causal, _dot_nt(q, k), 0.0).astype(BF16), v)
        st = st_sc[hh]
        o = inner + _dot(q, st.astype(BF16))
        g_c = jnp.exp(_ret_log_decay(hb * hpb + hh) * float(c_len))
        st_sc[hh] = g_c * (st + _dot_tn(k, v))
        mu = jnp.mean(o, axis=-1, keepdims=True)
        d = o - mu
        var = jnp.mean(d * d, axis=-1, keepdims=True)
        on = d * lax.rsqrt(var + GN_EPS) * gn_ref[:, hh * dv:(hh + 1) * dv]
        o_ref[:, hh * dv:(hh + 1) * dv] = (on * g_ref[:, hh * dv:(hh + 1) * dv].astype(F32)).astype(BF16)

    @pl.when(c == pl.num_programs(2) - 1)
    def _():
        sf_ref[...] = st_sc[...]


def _retention(proj, gn, state0, o_prev, nb, s, c_len, row0, dk, dv, hpb):
    T = proj.shape[0]
    H = RET_HEADS
    nc = s // c_len
    rb0 = row0 // c_len
    nhb = H // hpb
    has_state = state0 is not None
    rowblk = lambda b, h, c: rb0 + b * nc + c
    wk, wv = hpb * dk, hpb * dv
    kb = H * dk // wk
    vb = 2 * H * dk // wv
    in_specs = [pl.BlockSpec((c_len, wk), lambda b, h, c: (rowblk(b, h, c), h)),
                pl.BlockSpec((c_len, wk), lambda b, h, c: (rowblk(b, h, c), kb + h)),
                pl.BlockSpec((c_len, wv), lambda b, h, c: (rowblk(b, h, c), vb + h)),
                pl.BlockSpec((c_len, wv), lambda b, h, c: (rowblk(b, h, c), vb + nhb + h)),
                pl.BlockSpec((1, wv), lambda b, h, c: (0, h))]
    args = [proj, proj, proj, proj, gn]
    if has_state:
        in_specs.append(pl.BlockSpec((None, hpb, dk, dv), lambda b, h, c: (b, h, 0, 0)))
        args.append(state0)
    aliases = {}
    if o_prev is not None:
        in_specs.append(pl.BlockSpec(memory_space=pl.ANY))
        args.append(o_prev)
        aliases = {len(args) - 1: 0}

    def kern(*refs):
        if o_prev is not None:
            n_in = len(args)
            refs = refs[:n_in - 1] + refs[n_in:]
        _ret_kernel(*refs, c_len=c_len, has_state=has_state, hpb=hpb, dk=dk, dv=dv)

    return pl.pallas_call(
        kern,
        grid=(nb, nhb, nc),
        in_specs=in_specs,
        out_specs=[pl.BlockSpec((c_len, wv), lambda b, h, c: (rowblk(b, h, c), h)),
                   pl.BlockSpec((None, hpb, dk, dv), lambda b, h, c: (b, h, 0, 0))],
        out_shape=[jax.ShapeDtypeStruct((T, H * dv), BF16),
                   jax.ShapeDtypeStruct((nb, H, dk, dv), F32)],
        scratch_shapes=[pltpu.VMEM((hpb, dk, dv), F32)],
        input_output_aliases=aliases,
        compiler_params=_cp(("parallel", "parallel", "arbitrary")),
        name="retention",
    )(*args)


def _c_out_kernel(x_ref, o_ref, w_ref, y_ref):
    y_ref[...] = x_ref[...] + _dot(o_ref[...], w_ref[...])


def _c_out(x, o, w, li, tm, tn):
    T, D = x.shape
    K = o.shape[1]
    return pl.pallas_call(
        _c_out_kernel,
        grid=(pl.cdiv(T, tm), D // tn),
        in_specs=[pl.BlockSpec((tm, tn), lambda i, j: (i, j)), pl.BlockSpec((tm, K), lambda i, j: (i, 0)),
                  pl.BlockSpec((None, K, tn), lambda i, j: (li, 0, j))],
        out_specs=pl.BlockSpec((tm, tn), lambda i, j: (i, j)),
        out_shape=jax.ShapeDtypeStruct((T, D), F32),
        compiler_params=_cp(("parallel", "arbitrary")),
        name="c_out",
    )(x, o, w)


def _rope_tables(pos, half):
    inv = ROPE_THETA ** (-jnp.arange(half, dtype=F32) / half)
    ang = pos.astype(F32)[:, None] * inv[None, :]
    return jnp.cos(ang), jnp.sin(ang)


def _swap_halves(w):
    half = w.shape[-1] // 2
    return jnp.concatenate([w[..., half:], w[..., :half]], axis=-1)


def _prep_w_in_ab(w):
    o1 = Q_LORA + KV_LORA
    o2 = o1 + MLA_ROPE
    kr = w[:, o1:o2]
    ks = _swap_halves(kr)
    return jnp.concatenate([w[:, :o1], kr, ks, ks, kr, w[:, o2:]], axis=1).astype(BF16)


def _prep_w_q_b(w):
    w = w.reshape(Q_LORA, MLA_HEADS, MLA_NOPE + MLA_ROPE)
    rope = w[..., MLA_NOPE:]
    swap = _swap_halves(rope)
    out = jnp.concatenate([w[..., :MLA_NOPE], rope, swap, swap, rope], axis=-1)
    return out.reshape(Q_LORA, -1).astype(BF16)


def _prep_w_kv_b(w):
    w = w.reshape(KV_LORA, MLA_HEADS, MLA_NOPE + MLA_V)
    return jnp.concatenate([w[..., :MLA_NOPE].reshape(KV_LORA, -1),
                            w[..., MLA_NOPE:].reshape(KV_LORA, -1)], axis=1).astype(BF16)


def kernel(x_prompt, x_sample, cache_mla_ckv, cache_mla_krope, state_s5_re, state_s5_im, state_ret,
           norm_mix, norm_mlp, norm_final, w_in_ab, q_a_norm, kv_a_norm, w_q_b, w_kv_b,
           s5_lam_re, s5_lam_im, s5_log_dt, s5_b_re, s5_b_im, s5_c_re, s5_c_im, s5_d, w_glu, b_glu,
           w_out_ab, w_in_c, ret_gn, w_out_c, w_up, w_down):
    BP, SP, D = x_prompt.shape
    BS, SS, _ = x_sample.shape
    past = cache_mla_ckv.shape[2]
    depth = norm_mix.shape[0]
    G, N = s5_lam_re.shape[1:]
    P, L = S5_GROUP, S5_L
    dk, dv = state_ret.shape[3:]
    TP, TS = BP * SP, BS * SS
    T = TP + TS

    x = (x_prompt.reshape(TP, D), x_sample.reshape(TS, D))
    pos = jnp.concatenate([jnp.arange(SP), jnp.tile(past + jnp.arange(SS), WIDE_TM // SS)])

    c32, s32 = _rope_tables(pos, MLA_ROPE // 2)
    z64 = jnp.zeros((pos.shape[0], LANE - MLA_ROPE), F32)
    cs1 = jnp.concatenate([c32, c32, z64], axis=1)
    cs2 = jnp.concatenate([-s32, s32, z64], axis=1)
    cos_r, sin_r = _rope_tables(pos, dk // 2)

    skp = -(-(past + SS) // LANE) * LANE
    w_glu_b, w_out_ab_b, w_in_c_b, w_out_c_b, w_up_b, w_down_b = (
        w.astype(BF16) for w in (w_glu, w_out_ab, w_in_c, w_out_c, w_up, w_down))
    ckv_p, kr_p, s5r_p, s5i_p, ret_p = [], [], [], [], []
    ckv_s, kr_s, s5r_s, s5i_s, ret_s = [], [], [], [], []

    for layer in range(depth):
        i = layer // 2
        g_mix = norm_mix[layer].reshape(1, D)
        if layer % 2 == 0:
            xs = x if isinstance(x, tuple) else (x,)
            q, ckv, kr, u, ub_p, ub_s = _ab_in(
                xs, g_mix, _prep_w_in_ab(w_in_ab[i]), q_a_norm[i].reshape(1, -1),
                kv_a_norm[i].reshape(1, -1), _prep_w_q_b(w_q_b[i]), cs1, cs2, AB_TM, TP, SP)
            ckv_p.append(ckv[:TP].reshape(BP, SP, KV_LORA))
            kr_p.append(kr[:TP, :MLA_ROPE].reshape(BP, SP, MLA_ROPE))
            ckv_s.append(ckv[TP:].reshape(BS, SS, KV_LORA))
            kr_s.append(kr[TP:, :MLA_ROPE].reshape(BS, SS, MLA_ROPE))
            pad = skp - past - SS
            ckv_hist = jnp.concatenate([cache_mla_ckv[i], ckv[TP:].reshape(BS, SS, KV_LORA),
                                        jnp.zeros((BS, pad, KV_LORA), F32)], axis=1)
            kr_past = jnp.pad(cache_mla_krope[i], ((0, 0), (0, 0), (0, LANE - MLA_ROPE)))
            kr_hist = jnp.concatenate([kr_past, kr[TP:].reshape(BS, SS, LANE),
                                       jnp.zeros((BS, pad, LANE), F32)], axis=1)
            w_kv = _prep_w_kv_b(w_kv_b[i])
            k_p, v_p = _kv_proj(ckv, kr, w_kv, TM)
            k_s, v_s = _kv_proj(ckv_hist.reshape(BS * skp, KV_LORA), kr_hist.reshape(BS * skp, LANE),
                                w_kv, skp)
            attn = (_attn_long(q, k_p, v_p, BP, SP, ATT_TK),
                    _attn_short(q, k_s, v_s, BS, SS, skp, TP, 0, past, past + SS))
            params = _s5_params(s5_lam_re[i], s5_lam_im[i], s5_log_dt[i], s5_b_re[i], s5_b_im[i],
                                s5_c_re[i], s5_c_im[i])
            ncp, ncs = SP // L, SS // L
            u_p = ub_p.reshape(BP, ncp, L, G, P).transpose(3, 1, 0, 2, 4).reshape(G, ncp * BP, L * P)
            u_s = ub_s.reshape(BS, ncs, L, G, P).transpose(3, 1, 0, 2, 4).reshape(G, ncs * BS, L * P)
            y_p, hr_p, hi_p = _s5_main(u_p, params, None, BP, ncp)
            h0 = (jnp.swapaxes(state_s5_re[i], 0, 1), jnp.swapaxes(state_s5_im[i], 0, 1))
            y_s, hr_s, hi_s = _s5_main(u_s, params, h0, BS, ncs)
            y_p = y_p.reshape(G, ncp, BP, L, P).transpose(2, 1, 3, 0, 4).reshape(TP, G * P)
            y_s = y_s.reshape(G, ncs, BS, L, P).transpose(2, 1, 3, 0, 4).reshape(TS, G * P)
            s5r_p.append(jnp.swapaxes(hr_p, 0, 1))
            s5i_p.append(jnp.swapaxes(hi_p, 0, 1))
            s5r_s.append(jnp.swapaxes(hr_s, 0, 1))
            s5i_s.append(jnp.swapaxes(hi_s, 0, 1))
            x = _ab_out(xs, attn, (y_p, y_s), u, s5_d[i].reshape(1, G * P), w_glu_b,
                        b_glu[i].reshape(1, -1), w_out_ab_b, i, AB_TM)
        else:
            proj = _c_in(x, g_mix, w_in_c_b, i, cos_r, sin_r, dk, dv, WIDE_TM, C_IN_TN, TP, (RET_C, SS), SP)
            gn = ret_gn[i].reshape(1, -1)
            o, st_p = _retention(proj, gn, None, None, BP, SP, RET_C, 0, dk, dv, RET_HPB)
            o, st_s = _retention(proj, gn, state_ret[i], o, BS, SS, SS, TP, dk, dv, RET_HPB)
            ret_p.append(st_p)
            ret_s.append(st_s)
            x = _c_out(x, o, w_out_c_b, i, WIDE_TM, C_OUT_TN)
        last = layer == depth - 1
        mlp_tm = TM if last or T % MLP_TM else MLP_TM
        outs = _mlp(x, norm_mlp[layer].reshape(1, D), w_up_b, w_down_b, layer,
                    norm_final.reshape(1, D), last, (TP, TS) if last else (T,), mlp_tm, MLP_TF)
        x = outs[0]

    return (outs[0].reshape(BP, SP, D), outs[1].reshape(BS, SS, D),
            jnp.stack(ckv_p), jnp.stack(kr_p), jnp.stack(s5r_p), jnp.stack(s5i_p), jnp.stack(ret_p),
            jnp.stack(ckv_s), jnp.stack(kr_s), jnp.stack(s5r_s), jnp.stack(s5i_s), jnp.stack(ret_s))
```

```python
import functools
import math

import jax
import jax.numpy as jnp
import numpy as np
from jax import lax
from jax.experimental import pallas as pl
from jax.experimental.pallas import tpu as pltpu

F32 = jnp.float32
BF16 = jnp.bfloat16

CHUNK = 64
MLA_V = 128
MLA_NOPE = 128
MLA_ROPE = 64
MLA_HEADS = 8
Q_LORA = 512
KV_LORA = 256
MLA_SCALE = (MLA_NOPE + MLA_ROPE) ** -0.5
S5_GROUP = 16
S5_STATE = 64
RET_HEADS = 8
ROPE_THETA = 10000.0
EPS = 1e-6
GN_EPS = 1e-5
LOG2E = 1.4426950408889634

LANE = 128
QK_PAD = 256
V_PAD = 256
S5_L = 64
TM = 512
WIDE_TM = 1024
AB_TM = 512
RESIDENT = pl.Buffered(1)
C_IN_TN = 2048
C_OUT_TN = 512
RET_C = 256
RET_HPB = 8
MLP_TF = 1024
MLP_TM = 640
ATT_TK = 512
NEG_BIG = -1e30
V7X_VMEM_BYTES = 64 * 2 ** 20
VMEM_LIMIT = V7X_VMEM_BYTES * 7 // 8


def _cp(sem):
    return pltpu.CompilerParams(dimension_semantics=sem, vmem_limit_bytes=VMEM_LIMIT)


def _rms(x, g):
    return x * lax.rsqrt(jnp.mean(x * x, axis=-1, keepdims=True) + EPS) * g


def _dot(a, b):
    return jnp.dot(a, b, preferred_element_type=F32)


def _dot_nt(a, b):
    return lax.dot_general(a, b, (((1,), (1,)), ((), ())), preferred_element_type=F32)


def _dot_tn(a, b):
    return lax.dot_general(a, b, (((0,), (0,)), ((), ())), preferred_element_type=F32)


def _parts_specs(rows, tm, width, row_axis=0):
    assert all(r % tm == 0 for r in rows[:-1])
    specs, starts, start = [], [], 0
    for r in rows:
        nblk = pl.cdiv(r, tm)

        def index_map(*g, start=start, nblk=nblk):
            return (jnp.clip(g[row_axis] - start, 0, nblk - 1), 0)

        specs.append(pl.BlockSpec((tm, width), index_map))
        starts.append(start)
        start += nblk
    return specs, tuple(starts)


def _parts_load(refs, starts, i):
    x = refs[0][...]
    for ref, s in zip(refs[1:], starts[1:]):
        x = jnp.where(i >= s, ref[...], x)
    return x


def _parts_store(refs, starts, i, val):
    ends = starts[1:] + (None,)
    for ref, s, e in zip(refs, starts, ends):
        cond = i >= s if e is None else (i >= s) & (i < e)

        @pl.when(cond)
        def _(ref=ref):
            ref[...] = val


def _ab_in_kernel(*refs, x_starts, ub_starts):
    n = len(x_starts)
    x_refs = refs[:n]
    (g_ref, win_ref, qn_ref, kvn_ref, wq_ref, cs1_ref, cs2_ref,
     q_out, ckv_out, kr_out, u_out) = refs[n:n + 11]
    ub_outs = refs[n + 11:]
    x = _parts_load(x_refs, x_starts, pl.program_id(0))
    xn = _rms(x, g_ref[...]).astype(BF16)
    proj = _dot(xn, win_ref[...])
    cqn = _rms(proj[:, :Q_LORA], qn_ref[...]).astype(BF16)
    o = Q_LORA
    ckv_out[...] = _rms(proj[:, o:o + KV_LORA], kvn_ref[...])
    o += KV_LORA
    cs1 = cs1_ref[...]
    cs2 = cs2_ref[...]
    kr_out[...] = proj[:, o:o + LANE] * cs1 + proj[:, o + LANE:o + 2 * LANE] * cs2
    o += 2 * LANE
    u_out[...] = proj[:, o:]
    _parts_store(ub_outs, ub_starts, pl.program_id(0), proj[:, o:].astype(BF16))
    q = _dot(cqn, wq_ref[...])
    qs = MLA_SCALE * LOG2E
    for h in range(MLA_HEADS):
        b = h * 3 * LANE
        q_out[:, h * QK_PAD:h * QK_PAD + LANE] = (q[:, b:b + LANE] * qs).astype(BF16)
        rope = q[:, b + LANE:b + 2 * LANE] * cs1 + q[:, b + 2 * LANE:b + 3 * LANE] * cs2
        q_out[:, h * QK_PAD + LANE:(h + 1) * QK_PAD] = (rope * qs).astype(BF16)


def _pos_table_spec(tm, width, long_rows, seq_len, axis=0):
    assert seq_len % tm == 0 and long_rows % tm == 0
    n_long, per_seq = long_rows // tm, seq_len // tm

    def index_map(*g):
        return (jnp.where(g[axis] < n_long, g[axis] % per_seq, per_seq), 0)

    return pl.BlockSpec((tm, width), index_map)


def _ab_in(xs, g, win, qn, kvn, wq, cs1, cs2, tm, long_rows, seq_len):
    D = xs[0].shape[1]
    T = sum(x.shape[0] for x in xs)
    row = lambda i: (i, 0)
    fix = lambda i: (0, 0)
    s5w = win.shape[1] - Q_LORA - KV_LORA - 2 * LANE
    x_specs, x_starts = _parts_specs([x.shape[0] for x in xs], tm, D)
    ub_rows = (long_rows, T - long_rows)
    ub_specs, ub_starts = _parts_specs(ub_rows, tm, s5w)
    return pl.pallas_call(
        functools.partial(_ab_in_kernel, x_starts=x_starts, ub_starts=ub_starts),
        grid=(T // tm,),
        in_specs=x_specs + [pl.BlockSpec((1, D), fix),
                  pl.BlockSpec(win.shape, fix, pipeline_mode=RESIDENT), pl.BlockSpec((1, Q_LORA), fix),
                  pl.BlockSpec((1, KV_LORA), fix), pl.BlockSpec(wq.shape, fix, pipeline_mode=RESIDENT),
                  _pos_table_spec(tm, LANE, long_rows, seq_len), _pos_table_spec(tm, LANE, long_rows, seq_len)],
        out_specs=[pl.BlockSpec((tm, MLA_HEADS * QK_PAD), row), pl.BlockSpec((tm, KV_LORA), row),
                   pl.BlockSpec((tm, LANE), row), pl.BlockSpec((tm, s5w), row)] + ub_specs,
        out_shape=[jax.ShapeDtypeStruct((T, MLA_HEADS * QK_PAD), BF16),
                   jax.ShapeDtypeStruct((T, KV_LORA), F32),
                   jax.ShapeDtypeStruct((T, LANE), F32),
                   jax.ShapeDtypeStruct((T, s5w), F32)]
        + [jax.ShapeDtypeStruct((r, s5w), BF16) for r in ub_rows],
        compiler_params=_cp(("arbitrary",)),
        name="ab_in",
    )(*xs, g, win, qn, kvn, wq, cs1, cs2)


def _kv_kernel(ckv_ref, kr_ref, w_ref, k_out, v_out):
    kv = _dot(ckv_ref[...].astype(BF16), w_ref[...])
    kr = kr_ref[...].astype(BF16)
    v0 = MLA_HEADS * MLA_NOPE
    ones = jnp.ones((kv.shape[0], V_PAD - MLA_V), BF16)
    for h in range(MLA_HEADS):
        k_out[:, h * QK_PAD:h * QK_PAD + LANE] = kv[:, h * LANE:(h + 1) * LANE].astype(BF16)
        k_out[:, h * QK_PAD + LANE:(h + 1) * QK_PAD] = kr
        v_out[:, h * V_PAD:h * V_PAD + MLA_V] = kv[:, v0 + h * MLA_V:v0 + (h + 1) * MLA_V].astype(BF16)
        v_out[:, h * V_PAD + MLA_V:(h + 1) * V_PAD] = ones


def _kv_proj(ckv, kr, w, tm):
    R = ckv.shape[0]
    row = lambda i: (i, 0)
    return pl.pallas_call(
        _kv_kernel,
        grid=(R // tm,),
        in_specs=[pl.BlockSpec((tm, KV_LORA), row), pl.BlockSpec((tm, LANE), row),
                  pl.BlockSpec(w.shape, lambda i: (0, 0))],
        out_specs=[pl.BlockSpec((tm, MLA_HEADS * QK_PAD), row),
                   pl.BlockSpec((tm, MLA_HEADS * V_PAD), row)],
        out_shape=[jax.ShapeDtypeStruct((R, MLA_HEADS * QK_PAD), BF16),
                   jax.ShapeDtypeStruct((R, MLA_HEADS * V_PAD), BF16)],
        compiler_params=_cp(("parallel",)),
        name="kv_proj",
    )(ckv, kr, w)


def _attn_long_kernel(q_ref, k_ref, v_ref, o_ref, s_sc, m_sc, acc_sc, *, tq, tk):
    qi = pl.program_id(2)

    def scores_to(slot, kt, r0=0):
        s_sc[slot, r0:, :] = _dot_nt(q_ref[r0:, :], k_ref[pl.ds(pl.multiple_of(kt * tk, tk), tk), :])

    def consume(slot, kt, masked, r0=0):
        s = s_sc[slot, r0:, :]
        if masked:
            qpos = qi * tq + r0 + lax.broadcasted_iota(jnp.int32, (tq - r0, tk), 0)
            kpos = kt * tk + lax.broadcasted_iota(jnp.int32, (tq - r0, tk), 1)
            s = jnp.where(kpos // CHUNK <= qpos // CHUNK, s, NEG_BIG)
        m = m_sc[r0:, :]
        m_new = jnp.maximum(m, jnp.max(s, axis=1, keepdims=True))
        alpha = jnp.exp2(m - m_new)
        p = jnp.exp2(s - jnp.concatenate([m_new] * (tk // LANE), axis=1))
        m_sc[r0:, :] = m_new
        v = v_ref[pl.ds(pl.multiple_of(kt * tk, tk), tk), :]
        acc_sc[r0:, :] = (jnp.concatenate([alpha] * (V_PAD // LANE), axis=1) * acc_sc[r0:, :]
                          + _dot(p.astype(BF16), v))

    m_sc[...] = jnp.full_like(m_sc, NEG_BIG)
    acc_sc[...] = jnp.zeros_like(acc_sc)
    scores_to(0, 0)

    def pair(j, carry):
        scores_to(1, 2 * j + 1)
        consume(0, 2 * j, False)
        scores_to(0, 2 * j + 2)
        consume(1, 2 * j + 1, False)
        return carry

    lax.fori_loop(0, qi, pair, 0)
    scores_to(1, 2 * qi + 1, tk)
    consume(0, 2 * qi, True)
    consume(1, 2 * qi + 1, True, tk)
    acc = acc_sc[...]
    o_ref[...] = (acc[:, :MLA_V] / acc[:, MLA_V:]).astype(BF16)


def _attn_long(q, k, v, nb, s, tk):
    tq = 2 * tk
    assert s % tq == 0 and V_PAD == 2 * MLA_V
    nq = s // tq
    return pl.pallas_call(
        functools.partial(_attn_long_kernel, tq=tq, tk=tk),
        grid=(nb, MLA_HEADS, nq),
        in_specs=[pl.BlockSpec((tq, QK_PAD), lambda b, h, i: (b * nq + i, h)),
                  pl.BlockSpec((s, QK_PAD), lambda b, h, i: (b, h)),
                  pl.BlockSpec((s, V_PAD), lambda b, h, i: (b, h))],
        out_specs=pl.BlockSpec((tq, MLA_V), lambda b, h, i: (b * nq + i, h)),
        out_shape=jax.ShapeDtypeStruct((nb * s, MLA_HEADS * MLA_V), BF16),
        scratch_shapes=[pltpu.VMEM((2, tq, tk), F32), pltpu.VMEM((tq, LANE), F32),
                        pltpu.VMEM((tq, V_PAD), F32)],
        compiler_params=_cp(("parallel", "parallel", "arbitrary")),
        name="attn_long",
    )(q, k, v)


def _attn_short_kernel(q_ref, k_ref, v_ref, o_ref, *, q_pos0, n_keys):
    sq = q_ref.shape[0]
    skp = k_ref.shape[0]
    s = _dot_nt(q_ref[...], k_ref[...])
    qpos = q_pos0 + lax.broadcasted_iota(jnp.int32, (sq, skp), 0)
    kpos = lax.broadcasted_iota(jnp.int32, (sq, skp), 1)
    ok = (kpos // CHUNK <= qpos // CHUNK) & (kpos < n_keys)
    s = jnp.where(ok, s, NEG_BIG)
    m = jnp.max(s, axis=1, keepdims=True)
    p = jnp.exp2(s - m)
    acc = _dot(p.astype(BF16), v_ref[...])
    o_ref[...] = (acc[:, :MLA_V] / acc[:, MLA_V:]).astype(BF16)


def _attn_short(q, k, v, nb, sq, skp, q_row0, k_row0, q_pos0, n_keys):
    qb0 = q_row0 // sq
    kb0 = k_row0 // skp
    return pl.pallas_call(
        functools.partial(_attn_short_kernel, q_pos0=q_pos0, n_keys=n_keys),
        grid=(nb, MLA_HEADS),
        in_specs=[pl.BlockSpec((sq, QK_PAD), lambda b, h: (qb0 + b, h)),
                  pl.BlockSpec((skp, QK_PAD), lambda b, h: (kb0 + b, h)),
                  pl.BlockSpec((skp, V_PAD), lambda b, h: (kb0 + b, h))],
        out_specs=pl.BlockSpec((sq, MLA_V), lambda b, h: (b, h)),
        out_shape=jax.ShapeDtypeStruct((nb * sq, MLA_HEADS * MLA_V), BF16),
        compiler_params=_cp(("parallel", "parallel")),
        name="attn_short",
    )(q, k, v)


def _s5_param_kernel(lr_ref, li_ref, ldt_ref, btr_ref, bti_ref, cr_ref, ci_ref,
                     m_out, wr_out, wi_out, vr_out, vi_out, alr_out, ali_out,
                     car_sc, cai_sc):
    L, P = S5_L, S5_GROUP
    lr = jnp.minimum(lr_ref[...], -1e-4)
    li = li_ref[...]
    dt = jnp.exp(ldt_ref[...])
    mag = jnp.exp(lr * dt)
    a_re = mag * jnp.cos(li * dt)
    a_im = mag * jnp.sin(li * dt)
    den = lr * lr + li * li
    f_re = ((a_re - 1.0) * lr + a_im * li) / den
    f_im = (a_im * lr - (a_re - 1.0) * li) / den
    btr = btr_ref[...]
    bti = bti_ref[...]
    bb_re = f_re * btr - f_im * bti
    bb_im = f_re * bti + f_im * btr
    cr = cr_ref[...]
    ci = ci_ref[...]

    n_pow = -(-(L + 1) // 8) * 8
    t = lax.broadcasted_iota(jnp.int32, (n_pow, 1), 0).astype(F32)
    pmag = jnp.exp((lr * dt) * t)
    pang = (li * dt) * t
    pw_re = pmag * jnp.cos(pang)
    pw_im = pmag * jnp.sin(pang)

    def apow(t):
        return pw_re[t:t + 1, :], pw_im[t:t + 1, :]

    for i in range(L):
        rows = slice(i * P, (i + 1) * P)
        pr, pi = apow(L - 1 - i)
        wr_out[rows, :] = (pr * bb_re - pi * bb_im).astype(BF16)
        wi_out[rows, :] = (pr * bb_im + pi * bb_re).astype(BF16)
        pr, pi = apow(i + 1)
        vr_out[rows, :] = (cr * pr - ci * pi).astype(BF16)
        vi_out[rows, :] = (-(cr * pi + ci * pr)).astype(BF16)
        pr, pi = apow(i)
        car_sc[rows, :] = cr * pr - ci * pi
        cai_sc[rows, :] = cr * pi + ci * pr
    pr, pi = apow(L)
    alr_out[...] = pr
    ali_out[...] = pi
    hp = lax.Precision.HIGHEST
    nt = (((1,), (1,)), ((), ()))
    krow = (lax.dot_general(bb_re, car_sc[...], nt, precision=hp, preferred_element_type=F32)
            - lax.dot_general(bb_im, cai_sc[...], nt, precision=hp, preferred_element_type=F32))
    lane = lax.broadcasted_iota(jnp.int32, (P, L * P), 1)
    for i in range(L):
        shifted = krow if i == 0 else pltpu.roll(krow, i * P, axis=1)
        m_out[i * P:(i + 1) * P, :] = jnp.where(lane >= i * P, shifted, 0.0).astype(BF16)


def _s5_params(lam_re, lam_im, log_dt, b_re, b_im, c_re, c_im):
    G, N = lam_re.shape
    P, L = S5_GROUP, S5_L
    LP = L * P
    v3 = lambda a: a.reshape(G, 1, N)
    ldt = jnp.broadcast_to(log_dt[:, None, None], (G, 1, N))
    btr = jnp.swapaxes(b_re, 1, 2)
    bti = jnp.swapaxes(b_im, 1, 2)
    vec = pl.BlockSpec((None, 1, N), lambda g: (g, 0, 0))
    mat = pl.BlockSpec((None, P, N), lambda g: (g, 0, 0))
    big = pl.BlockSpec((None, LP, N), lambda g: (g, 0, 0))
    return pl.pallas_call(
        _s5_param_kernel,
        grid=(G,),
        in_specs=[vec, vec, vec, mat, mat, mat, mat],
        out_specs=[pl.BlockSpec((None, LP, LP), lambda g: (g, 0, 0)), big, big, big, big, vec, vec],
        out_shape=[jax.ShapeDtypeStruct((G, LP, LP), BF16)]
        + [jax.ShapeDtypeStruct((G, LP, N), BF16)] * 4
        + [jax.ShapeDtypeStruct((G, 1, N), F32)] * 2,
        scratch_shapes=[pltpu.VMEM((LP, N), F32), pltpu.VMEM((LP, N), F32)],
        compiler_params=_cp(("parallel",)),
        name="s5_params",
    )(v3(lam_re), v3(lam_im), ldt, btr, bti, c_re, c_im)


def _s5_main_kernel(*refs, nseq, nchunk, has_state):
    if has_state:
        (u_ref, m_ref, wr_ref, wi_ref, vr_ref, vi_ref, alr_ref, ali_ref, h0r_ref, h0i_ref,
         y_out, hfr_out, hfi_out, xr_sc, xi_sc, hr_sc, hi_sc) = refs
    else:
        (u_ref, m_ref, wr_ref, wi_ref, vr_ref, vi_ref, alr_ref, ali_ref,
         y_out, hfr_out, hfi_out, xr_sc, xi_sc, hr_sc, hi_sc) = refs
    ub = u_ref[...]
    xr_sc[...] = _dot(ub, wr_ref[...])
    xi_sc[...] = _dot(ub, wi_ref[...])
    ar = alr_ref[...]
    ai = ali_ref[...]
    n = ar.shape[-1]
    if has_state:
        init = (h0r_ref[...], h0i_ref[...])
    else:
        init = (jnp.zeros((nseq, n), F32), jnp.zeros((nseq, n), F32))

    def body(c, carry):
        sr, si = carry
        rows = pl.ds(pl.multiple_of(c * nseq, nseq), nseq)
        hr_sc[rows, :] = sr
        hi_sc[rows, :] = si
        return (ar * sr - ai * si + xr_sc[rows, :], ar * si + ai * sr + xi_sc[rows, :])

    sr, si = lax.fori_loop(0, nchunk, body, init)
    hfr_out[...] = sr
    hfi_out[...] = si
    y_out[...] = (_dot(ub, m_ref[...])
                  + _dot_nt(hr_sc[...].astype(BF16), vr_ref[...])
                  + _dot_nt(hi_sc[...].astype(BF16), vi_ref[...])).astype(y_out.dtype)


def _s5_main(u, params, h0, nseq, nchunk):
    m, wr, wi, vr, vi, alr, ali = params
    G, R, LP = u.shape
    N = alr.shape[-1]
    has_state = h0 is not None
    g3 = lambda g: (g, 0, 0)
    big = pl.BlockSpec((None, LP, N), g3)
    vec = pl.BlockSpec((None, 1, N), g3)
    st = pl.BlockSpec((None, nseq, N), g3)
    in_specs = [pl.BlockSpec((None, R, LP), g3), pl.BlockSpec((None, LP, LP), g3),
                big, big, big, big, vec, vec]
    args = [u, m, wr, wi, vr, vi, alr, ali]
    if has_state:
        in_specs += [st, st]
        args += list(h0)
    return pl.pallas_call(
        functools.partial(_s5_main_kernel, nseq=nseq, nchunk=nchunk, has_state=has_state),
        grid=(G,),
        in_specs=in_specs,
        out_specs=[pl.BlockSpec((None, R, LP), g3), st, st],
        out_shape=[jax.ShapeDtypeStruct((G, R, LP), BF16),
                   jax.ShapeDtypeStruct((G, nseq, N), F32),
                   jax.ShapeDtypeStruct((G, nseq, N), F32)],
        scratch_shapes=[pltpu.VMEM((R, N), F32)] * 4,
        compiler_params=_cp(("parallel",)),
        name="s5_main",
    )(*args)


def _ab_out_kernel(*refs, starts):
    i = pl.program_id(0)
    parts = []
    for st in starts:
        parts.append(_parts_load(refs[:len(st)], st, i))
        refs = refs[len(st):]
    x, attn, y = parts
    u_ref, d_ref, wglu_ref, bglu_ref, wo_ref, o_ref = refs
    y = y.astype(F32) + d_ref[...] * u_ref[...]
    z = y * (0.5 * (1.0 + jnp.tanh(math.sqrt(2.0 / math.pi) * (y + 0.044715 * (y * y * y)))))
    gate = 1.0 / (1.0 + jnp.exp(-(_dot(z.astype(BF16), wglu_ref[...]) + bglu_ref[...])))
    ssm = (z * gate).astype(BF16)
    na = attn.shape[1]
    o_ref[...] = x + _dot(attn, wo_ref[:na, :]) + _dot(ssm, wo_ref[na:, :])


def _ab_out(xs, attns, ys, u, d, wglu, bglu, wo, li, tm):
    D = xs[0].shape[1]
    T = sum(x.shape[0] for x in xs)
    row = lambda i: (i, 0)
    fix = lambda i: (0, 0)
    lay = lambda i: (li, 0, 0)
    specs, starts = [], []
    for parts in (xs, attns, ys):
        sp, st = _parts_specs([p.shape[0] for p in parts], tm, parts[0].shape[1])
        specs += sp
        starts.append(st)
    return pl.pallas_call(
        functools.partial(_ab_out_kernel, starts=tuple(starts)),
        grid=(T // tm,),
        in_specs=specs + [pl.BlockSpec((tm, u.shape[1]), row), pl.BlockSpec(d.shape, fix),
                          pl.BlockSpec((None,) + wglu.shape[1:], lay, pipeline_mode=RESIDENT),
                          pl.BlockSpec(bglu.shape, fix),
                          pl.BlockSpec((None,) + wo.shape[1:], lay, pipeline_mode=RESIDENT)],
        out_specs=pl.BlockSpec((tm, D), row),
        out_shape=jax.ShapeDtypeStruct((T, D), F32),
        compiler_params=_cp(("parallel",)),
        name="ab_out",
    )(*xs, *attns, *ys, u, d, wglu, bglu, wo)


def _mlp_kernel(x_ref, g_ref, wup_ref, wdn_ref, gf_ref, *rest, final_norm, o_starts):
    n_out = len(o_starts)
    o_refs = rest[:n_out]
    xn_sc = rest[n_out]
    acc = rest[n_out + 1] if n_out > 1 else o_refs[0]
    f = pl.program_id(1)

    @pl.when(f == 0)
    def _():
        xn_sc[...] = _rms(x_ref[...], g_ref[...]).astype(BF16)
        acc[...] = jnp.zeros_like(acc)

    a = jnp.maximum(_dot(xn_sc[...], wup_ref[...]), 0.0)
    acc[...] += _dot((a * a).astype(BF16), wdn_ref[...])

    @pl.when(f == pl.num_programs(1) - 1)
    def _():
        r = x_ref[...] + acc[...]
        if final_norm:
            r = _rms(r, gf_ref[...])
        if n_out > 1:
            _parts_store(o_refs, o_starts, pl.program_id(0), r)
        else:
            acc[...] = r


def _mlp(x, g, wup, wdn, li, gf, final_norm, out_rows, tm, tf):
    T, D = x.shape
    FF = wup.shape[2]
    assert sum(out_rows) == T
    o_specs, o_starts = _parts_specs(out_rows, tm, D)
    return pl.pallas_call(
        functools.partial(_mlp_kernel, final_norm=final_norm, o_starts=o_starts),
        grid=(pl.cdiv(T, tm), FF // tf),
        in_specs=[pl.BlockSpec((tm, D), lambda i, f: (i, 0)), pl.BlockSpec((1, D), lambda i, f: (0, 0)),
                  pl.BlockSpec((None, D, tf), lambda i, f: (li, 0, f)),
                  pl.BlockSpec((None, tf, D), lambda i, f: (li, f, 0)),
                  pl.BlockSpec((1, D), lambda i, f: (0, 0))],
        out_specs=o_specs,
        out_shape=[jax.ShapeDtypeStruct((r, D), F32) for r in out_rows],
        scratch_shapes=[pltpu.VMEM((tm, D), BF16)] + [pltpu.VMEM((tm, D), F32)] * (len(out_rows) > 1),
        compiler_params=_cp(("arbitrary" if len(out_rows) > 1 else "parallel", "arbitrary")),
        name="mlp",
    )(x, g, wup, wdn, gf)


def _ret_log_decay(head):
    hf = jnp.full((1, 1), head, jnp.int32).astype(F32)
    return jnp.log(1.0 - jnp.exp2(-5.0 - hf))


def _c_in_kernel(x_ref, g_ref, w_ref, cos_ref, sin_ref, o_ref, xn_sc, *, n_rot, n_q, n_lin, dk,
                 chunk_split_row, chunk_lens):
    i = pl.program_id(0)
    j = pl.program_id(1)

    @pl.when(j == 0)
    def _():
        xn_sc[...] = _rms(x_ref[...], g_ref[...]).astype(BF16)

    tm = x_ref.shape[0]
    tn = w_ref.shape[1]
    half = dk // 2

    def chunks():
        for s in range(0, tn, dk):
            yield s, _dot(xn_sc[...], w_ref[:, s:s + dk])

    @pl.when(j < n_rot)
    def _():
        cos = cos_ref[...]
        sin = sin_ref[...]
        c_len = jnp.where(i * tm >= chunk_split_row, chunk_lens[1], chunk_lens[0])
        pos = (lax.broadcasted_iota(jnp.int32, (tm, 1), 0) & (c_len - 1)).astype(F32) + 1.0
        sign = jnp.where(j < n_q, 1.0, -1.0)
        scale = jnp.where(j < n_q, 1.0, dk ** -0.5)
        for s, acc in chunks():
            lg = _ret_log_decay(((j * tn + s) // dk) % RET_HEADS)
            dec = jnp.exp((sign * lg) * pos) * scale
            x1 = acc[:, :half]
            x2 = acc[:, half:]
            o_ref[:, s:s + half] = ((x1 * cos - x2 * sin) * dec).astype(BF16)
            o_ref[:, s + half:s + dk] = ((x1 * sin + x2 * cos) * dec).astype(BF16)

    @pl.when((j >= n_rot) & (j < n_lin))
    def _():
        for s, acc in chunks():
            o_ref[:, s:s + dk] = acc.astype(BF16)

    @pl.when(j >= n_lin)
    def _():
        for s, acc in chunks():
            o_ref[:, s:s + dk] = (acc / (1.0 + jnp.exp(-acc))).astype(BF16)


def _c_in(x, g, w, li, cos, sin, dk, dv, tm, tn, chunk_split_row, chunk_lens, seq_len):
    T, D = x.shape
    N = w.shape[2]
    qk = RET_HEADS * dk
    n_q = qk // tn
    n_rot = 2 * qk // tn
    n_lin = (2 * qk + RET_HEADS * dv) // tn
    assert chunk_split_row % tm == 0
    assert all(tm % c == 0 and c & (c - 1) == 0 for c in chunk_lens)
    return pl.pallas_call(
        functools.partial(_c_in_kernel, n_rot=n_rot, n_q=n_q, n_lin=n_lin, dk=dk,
                          chunk_split_row=chunk_split_row, chunk_lens=chunk_lens),
        grid=(pl.cdiv(T, tm), N // tn),
        in_specs=[pl.BlockSpec((tm, D), lambda i, j: (i, 0)), pl.BlockSpec((1, D), lambda i, j: (0, 0)),
                  pl.BlockSpec((None, D, tn), lambda i, j: (li, 0, j)),
                  _pos_table_spec(tm, dk // 2, chunk_split_row, seq_len),
                  _pos_table_spec(tm, dk // 2, chunk_split_row, seq_len)],
        out_specs=pl.BlockSpec((tm, tn), lambda i, j: (i, j)),
        out_shape=jax.ShapeDtypeStruct((T, N), BF16),
        scratch_shapes=[pltpu.VMEM((tm, D), BF16)],
        compiler_params=_cp(("parallel", "arbitrary")),
        name="c_in",
    )(x, g, w, cos, sin)


def _ret_kernel(*refs, c_len, has_state, hpb, dk, dv):
    if has_state:
        q_ref, k_ref, v_ref, g_ref, gn_ref, s0_ref, o_ref, sf_ref, st_sc = refs
    else:
        q_ref, k_ref, v_ref, g_ref, gn_ref, o_ref, sf_ref, st_sc = refs
    hb = pl.program_id(1)
    c = pl.program_id(2)

    @pl.when(c == 0)
    def _():
        if has_state:
            st_sc[...] = s0_ref[...]
        else:
            st_sc[...] = jnp.zeros_like(st_sc)

    row = lax.broadcasted_iota(jnp.int32, (c_len, c_len), 0)
    col = lax.broadcasted_iota(jnp.int32, (c_len, c_len), 1)
    causal = row >= col
    for hh in range(hpb):
        q = q_ref[:, hh * dk:(hh + 1) * dk]
        k = k_ref[:, hh * dk:(hh + 1) * dk]
        v = v_ref[:, hh * dv:(hh + 1) * dv]
        inner = _dot(jnp.where(causal, _dot_nt(q, k), 0.0).astype(BF16), v)
        st = st_sc[hh]
        o = inner + _dot(q, st.astype(BF16))
        g_c = jnp.exp(_ret_log_decay(hb * hpb + hh) * float(c_len))
        st_sc[hh] = g_c * (st + _dot_tn(k, v))
        mu = jnp.mean(o, axis=-1, keepdims=True)
        d = o - mu
        var = jnp.mean(d * d, axis=-1, keepdims=True)
        on = d * lax.rsqrt(var + GN_EPS) * gn_ref[:, hh * dv:(hh + 1) * dv]
        o_ref[:, hh * dv:(hh + 1) * dv] = (on * g_ref[:, hh * dv:(hh + 1) * dv].astype(F32)).astype(BF16)

    @pl.when(c == pl.num_programs(2) - 1)
    def _():
        sf_ref[...] = st_sc[...]


def _retention(proj, gn, state0, o_prev, nb, s, c_len, row0, dk, dv, hpb):
    T = proj.shape[0]
    H = RET_HEADS
    nc = s // c_len
    rb0 = row0 // c_len
    nhb = H // hpb
    has_state = state0 is not None
    rowblk = lambda b, h, c: rb0 + b * nc + c
    wk, wv = hpb * dk, hpb * dv
    kb = H * dk // wk
    vb = 2 * H * dk // wv
    in_specs = [pl.BlockSpec((c_len, wk), lambda b, h, c: (rowblk(b, h, c), h)),
                pl.BlockSpec((c_len, wk), lambda b, h, c: (rowblk(b, h, c), kb + h)),
                pl.BlockSpec((c_len, wv), lambda b, h, c: (rowblk(b, h, c), vb + h)),
                pl.BlockSpec((c_len, wv), lambda b, h, c: (rowblk(b, h, c), vb + nhb + h)),
                pl.BlockSpec((1, wv), lambda b, h, c: (0, h))]
    args = [proj, proj, proj, proj, gn]
    if has_state:
        in_specs.append(pl.BlockSpec((None, hpb, dk, dv), lambda b, h, c: (b, h, 0, 0)))
        args.append(state0)
    aliases = {}
    if o_prev is not None:
        in_specs.append(pl.BlockSpec(memory_space=pl.ANY))
        args.append(o_prev)
        aliases = {len(args) - 1: 0}

    def kern(*refs):
        if o_prev is not None:
            n_in = len(args)
            refs = refs[:n_in - 1] + refs[n_in:]
        _ret_kernel(*refs, c_len=c_len, has_state=has_state, hpb=hpb, dk=dk, dv=dv)

    return pl.pallas_call(
        kern,
        grid=(nb, nhb, nc),
        in_specs=in_specs,
        out_specs=[pl.BlockSpec((c_len, wv), lambda b, h, c: (rowblk(b, h, c), h)),
                   pl.BlockSpec((None, hpb, dk, dv), lambda b, h, c: (b, h, 0, 0))],
        out_shape=[jax.ShapeDtypeStruct((T, H * dv), BF16),
                   jax.ShapeDtypeStruct((nb, H, dk, dv), F32)],
        scratch_shapes=[pltpu.VMEM((hpb, dk, dv), F32)],
        input_output_aliases=aliases,
        compiler_params=_cp(("parallel", "parallel", "arbitrary")),
        name="retention",
    )(*args)


def _c_out_kernel(x_ref, o_ref, w_ref, y_ref):
    y_ref[...] = x_ref[...] + _dot(o_ref[...], w_ref[...])


def _c_out(x, o, w, li, tm, tn):
    T, D = x.shape
    K = o.shape[1]
    return pl.pallas_call(
        _c_out_kernel,
        grid=(pl.cdiv(T, tm), D // tn),
        in_specs=[pl.BlockSpec((tm, tn), lambda i, j: (i, j)), pl.BlockSpec((tm, K), lambda i, j: (i, 0)),
                  pl.BlockSpec((None, K, tn), lambda i, j: (li, 0, j))],
        out_specs=pl.BlockSpec((tm, tn), lambda i, j: (i, j)),
        out_shape=jax.ShapeDtypeStruct((T, D), F32),
        compiler_params=_cp(("parallel", "arbitrary")),
        name="c_out",
    )(x, o, w)


def _rope_tables(pos, half):
    inv = ROPE_THETA ** (-jnp.arange(half, dtype=F32) / half)
    ang = pos.astype(F32)[:, None] * inv[None, :]
    return jnp.cos(ang), jnp.sin(ang)


def _swap_halves(w):
    half = w.shape[-1] // 2
    return jnp.concatenate([w[..., half:], w[..., :half]], axis=-1)


def _prep_w_in_ab(w):
    o1 = Q_LORA + KV_LORA
    o2 = o1 + MLA_ROPE
    kr = w[:, o1:o2]
    ks = _swap_halves(kr)
    return jnp.concatenate([w[:, :o1], kr, ks, ks, kr, w[:, o2:]], axis=1).astype(BF16)


def _prep_w_q_b(w):
    w = w.reshape(Q_LORA, MLA_HEADS, MLA_NOPE + MLA_ROPE)
    rope = w[..., MLA_NOPE:]
    swap = _swap_halves(rope)
    out = jnp.concatenate([w[..., :MLA_NOPE], rope, swap, swap, rope], axis=-1)
    return out.reshape(Q_LORA, -1).astype(BF16)


def _prep_w_kv_b(w):
    w = w.reshape(KV_LORA, MLA_HEADS, MLA_NOPE + MLA_V)
    return jnp.concatenate([w[..., :MLA_NOPE].reshape(KV_LORA, -1),
                            w[..., MLA_NOPE:].reshape(KV_LORA, -1)], axis=1).astype(BF16)


def kernel(x_prompt, x_sample, cache_mla_ckv, cache_mla_krope, state_s5_re, state_s5_im, state_ret,
           norm_mix, norm_mlp, norm_final, w_in_ab, q_a_norm, kv_a_norm, w_q_b, w_kv_b,
           s5_lam_re, s5_lam_im, s5_log_dt, s5_b_re, s5_b_im, s5_c_re, s5_c_im, s5_d, w_glu, b_glu,
           w_out_ab, w_in_c, ret_gn, w_out_c, w_up, w_down):
    BP, SP, D = x_prompt.shape
    BS, SS, _ = x_sample.shape
    past = cache_mla_ckv.shape[2]
    depth = norm_mix.shape[0]
    G, N = s5_lam_re.shape[1:]
    P, L = S5_GROUP, S5_L
    dk, dv = state_ret.shape[3:]
    TP, TS = BP * SP, BS * SS
    T = TP + TS

    x = (x_prompt.reshape(TP, D), x_sample.reshape(TS, D))
    pos = jnp.concatenate([jnp.arange(SP), jnp.tile(past + jnp.arange(SS), WIDE_TM // SS)])

    c32, s32 = _rope_tables(pos, MLA_ROPE // 2)
    z64 = jnp.zeros((pos.shape[0], LANE - MLA_ROPE), F32)
    cs1 = jnp.concatenate([c32, c32, z64], axis=1)
    cs2 = jnp.concatenate([-s32, s32, z64], axis=1)
    cos_r, sin_r = _rope_tables(pos, dk // 2)

    skp = -(-(past + SS) // LANE) * LANE
    w_glu_b, w_out_ab_b, w_in_c_b, w_out_c_b, w_up_b, w_down_b = (
        w.astype(BF16) for w in (w_glu, w_out_ab, w_in_c, w_out_c, w_up, w_down))
    ckv_p, kr_p, s5r_p, s5i_p, ret_p = [], [], [], [], []
    ckv_s, kr_s, s5r_s, s5i_s, ret_s = [], [], [], [], []

    for layer in range(depth):
        i = layer // 2
        g_mix = norm_mix[layer].reshape(1, D)
        if layer % 2 == 0:
            xs = x if isinstance(x, tuple) else (x,)
            q, ckv, kr, u, ub_p, ub_s = _ab_in(
                xs, g_mix, _prep_w_in_ab(w_in_ab[i]), q_a_norm[i].reshape(1, -1),
                kv_a_norm[i].reshape(1, -1), _prep_w_q_b(w_q_b[i]), cs1, cs2, AB_TM, TP, SP)
            ckv_p.append(ckv[:TP].reshape(BP, SP, KV_LORA))
            kr_p.append(kr[:TP, :MLA_ROPE].reshape(BP, SP, MLA_ROPE))
            ckv_s.append(ckv[TP:].reshape(BS, SS, KV_LORA))
            kr_s.append(kr[TP:, :MLA_ROPE].reshape(BS, SS, MLA_ROPE))
            pad = skp - past - SS
            ckv_hist = jnp.concatenate([cache_mla_ckv[i], ckv[TP:].reshape(BS, SS, KV_LORA),
                                        jnp.zeros((BS, pad, KV_LORA), F32)], axis=1)
            kr_past = jnp.pad(cache_mla_krope[i], ((0, 0), (0, 0), (0, LANE - MLA_ROPE)))
            kr_hist = jnp.concatenate([kr_past, kr[TP:].reshape(BS, SS, LANE),
                                       jnp.zeros((BS, pad, LANE), F32)], axis=1)
            w_kv = _prep_w_kv_b(w_kv_b[i])
            k_p, v_p = _kv_proj(ckv, kr, w_kv, TM)
            k_s, v_s = _kv_proj(ckv_hist.reshape(BS * skp, KV_LORA), kr_hist.reshape(BS * skp, LANE),
                                w_kv, skp)
            attn = (_attn_long(q, k_p, v_p, BP, SP, ATT_TK),
                    _attn_short(q, k_s, v_s, BS, SS, skp, TP, 0, past, past + SS))
            params = _s5_params(s5_lam_re[i], s5_lam_im[i], s5_log_dt[i], s5_b_re[i], s5_b_im[i],
                                s5_c_re[i], s5_c_im[i])
            ncp, ncs = SP // L, SS // L
            u_p = ub_p.reshape(BP, ncp, L, G, P).transpose(3, 1, 0, 2, 4).reshape(G, ncp * BP, L * P)
            u_s = ub_s.reshape(BS, ncs, L, G, P).transpose(3, 1, 0, 2, 4).reshape(G, ncs * BS, L * P)
            y_p, hr_p, hi_p = _s5_main(u_p, params, None, BP, ncp)
            h0 = (jnp.swapaxes(state_s5_re[i], 0, 1), jnp.swapaxes(state_s5_im[i], 0, 1))
            y_s, hr_s, hi_s = _s5_main(u_s, params, h0, BS, ncs)
            y_p = y_p.reshape(G, ncp, BP, L, P).transpose(2, 1, 3, 0, 4).reshape(TP, G * P)
            y_s = y_s.reshape(G, ncs, BS, L, P).transpose(2, 1, 3, 0, 4).reshape(TS, G * P)
            s5r_p.append(jnp.swapaxes(hr_p, 0, 1))
            s5i_p.append(jnp.swapaxes(hi_p, 0, 1))
            s5r_s.append(jnp.swapaxes(hr_s, 0, 1))
            s5i_s.append(jnp.swapaxes(hi_s, 0, 1))
            x = _ab_out(xs, attn, (y_p, y_s), u, s5_d[i].reshape(1, G * P), w_glu_b,
                        b_glu[i].reshape(1, -1), w_out_ab_b, i, AB_TM)
        else:
            proj = _c_in(x, g_mix, w_in_c_b, i, cos_r, sin_r, dk, dv, WIDE_TM, C_IN_TN, TP, (RET_C, SS), SP)
            gn = ret_gn[i].reshape(1, -1)
            o, st_p = _retention(proj, gn, None, None, BP, SP, RET_C, 0, dk, dv, RET_HPB)
            o, st_s = _retention(proj, gn, state_ret[i], o, BS, SS, SS, TP, dk, dv, RET_HPB)
            ret_p.append(st_p)
            ret_s.append(st_s)
            x = _c_out(x, o, w_out_c_b, i, WIDE_TM, C_OUT_TN)
        last = layer == depth - 1
        mlp_tm = TM if last or T % MLP_TM else MLP_TM
        outs = _mlp(x, norm_mlp[layer].reshape(1, D), w_up_b, w_down_b, layer,
                    norm_final.reshape(1, D), last, (TP, TS) if last else (T,), mlp_tm, MLP_TF)
        x = outs[0]

    return (outs[0].reshape(BP, SP, D), outs[1].reshape(BS, SS, D),
            jnp.stack(ckv_p), jnp.stack(kr_p), jnp.stack(s5r_p), jnp.stack(s5i_p), jnp.stack(ret_p),
            jnp.stack(ckv_s), jnp.stack(kr_s), jnp.stack(s5r_s), jnp.stack(s5i_s), jnp.stack(ret_s))
```

```python
import functools
import math

import jax
import jax.numpy as jnp
import numpy as np
from jax import lax
from jax.experimental import pallas as pl
from jax.experimental.pallas import tpu as pltpu

F32 = jnp.float32
BF16 = jnp.bfloat16

CHUNK = 64
MLA_V = 128
MLA_NOPE = 128
MLA_ROPE = 64
MLA_HEADS = 8
Q_LORA = 512
KV_LORA = 256
MLA_SCALE = (MLA_NOPE + MLA_ROPE) ** -0.5
S5_GROUP = 16
S5_STATE = 64
RET_HEADS = 8
ROPE_THETA = 10000.0
EPS = 1e-6
GN_EPS = 1e-5
LOG2E = 1.4426950408889634

LANE = 128
QK_PAD = 256
V_PAD = 256
S5_L = 64
TM = 512
WIDE_TM = 1024
AB_TM = 512
RESIDENT = pl.Buffered(1)
C_IN_TN = 2048
C_OUT_TN = 512
RET_C = 256
RET_HPB = 8
MLP_TF = 1024
MLP_TM = 640
ATT_TK = 512
NEG_BIG = -1e30
V7X_VMEM_BYTES = 64 * 2 ** 20
VMEM_LIMIT = V7X_VMEM_BYTES * 7 // 8


def _cp(sem):
    return pltpu.CompilerParams(dimension_semantics=sem, vmem_limit_bytes=VMEM_LIMIT)


def _rms(x, g):
    return x * lax.rsqrt(jnp.mean(x * x, axis=-1, keepdims=True) + EPS) * g


def _dot(a, b):
    return jnp.dot(a, b, preferred_element_type=F32)


def _dot_nt(a, b):
    return lax.dot_general(a, b, (((1,), (1,)), ((), ())), preferred_element_type=F32)


def _dot_tn(a, b):
    return lax.dot_general(a, b, (((0,), (0,)), ((), ())), preferred_element_type=F32)


def _parts_specs(rows, tm, width, row_axis=0):
    assert all(r % tm == 0 for r in rows[:-1])
    specs, starts, start = [], [], 0
    for r in rows:
        nblk = pl.cdiv(r, tm)

        def index_map(*g, start=start, nblk=nblk):
            return (jnp.clip(g[row_axis] - start, 0, nblk - 1), 0)

        specs.append(pl.BlockSpec((tm, width), index_map))
        starts.append(start)
        start += nblk
    return specs, tuple(starts)


def _parts_load(refs, starts, i):
    x = refs[0][...]
    for ref, s in zip(refs[1:], starts[1:]):
        x = jnp.where(i >= s, ref[...], x)
    return x


def _parts_store(refs, starts, i, val):
    ends = starts[1:] + (None,)
    for ref, s, e in zip(refs, starts, ends):
        cond = i >= s if e is None else (i >= s) & (i < e)

        @pl.when(cond)
        def _(ref=ref):
            ref[...] = val


def _ab_in_kernel(*refs, x_starts, ub_starts):
    n = len(x_starts)
    x_refs = refs[:n]
    (g_ref, win_ref, qn_ref, kvn_ref, wq_ref, cs1_ref, cs2_ref, wkv_ref,
     q_out, ckv_out, kr_out, u_out, k_out, v_out) = refs[n:n + 14]
    ub_outs = refs[n + 14:]
    x = _parts_load(x_refs, x_starts, pl.program_id(0))
    xn = _rms(x, g_ref[...]).astype(BF16)
    proj = _dot(xn, win_ref[...])
    cqn = _rms(proj[:, :Q_LORA], qn_ref[...]).astype(BF16)
    o = Q_LORA
    ckv = _rms(proj[:, o:o + KV_LORA], kvn_ref[...])
    ckv_out[...] = ckv
    o += KV_LORA
    cs1 = cs1_ref[...]
    cs2 = cs2_ref[...]
    kr = proj[:, o:o + LANE] * cs1 + proj[:, o + LANE:o + 2 * LANE] * cs2
    kr_out[...] = kr
    _store_keys_values(_dot(ckv.astype(BF16), wkv_ref[...]), kr.astype(BF16), k_out, v_out)
    o += 2 * LANE
    u_out[...] = proj[:, o:]
    _parts_store(ub_outs, ub_starts, pl.program_id(0), proj[:, o:].astype(BF16))
    q = _dot(cqn, wq_ref[...])
    qs = MLA_SCALE * LOG2E
    for h in range(MLA_HEADS):
        b = h * 3 * LANE
        q_out[:, h * QK_PAD:h * QK_PAD + LANE] = (q[:, b:b + LANE] * qs).astype(BF16)
        rope = q[:, b + LANE:b + 2 * LANE] * cs1 + q[:, b + 2 * LANE:b + 3 * LANE] * cs2
        q_out[:, h * QK_PAD + LANE:(h + 1) * QK_PAD] = (rope * qs).astype(BF16)


def _pos_table_spec(tm, width, long_rows, seq_len, axis=0):
    assert seq_len % tm == 0 and long_rows % tm == 0
    n_long, per_seq = long_rows // tm, seq_len // tm

    def index_map(*g):
        return (jnp.where(g[axis] < n_long, g[axis] % per_seq, per_seq), 0)

    return pl.BlockSpec((tm, width), index_map)


def _ab_in(xs, g, win, qn, kvn, wq, wkv, cs1, cs2, tm, long_rows, seq_len):
    D = xs[0].shape[1]
    T = sum(x.shape[0] for x in xs)
    row = lambda i: (i, 0)
    fix = lambda i: (0, 0)
    s5w = win.shape[1] - Q_LORA - KV_LORA - 2 * LANE
    x_specs, x_starts = _parts_specs([x.shape[0] for x in xs], tm, D)
    ub_rows = (long_rows, T - long_rows)
    ub_specs, ub_starts = _parts_specs(ub_rows, tm, s5w)
    return pl.pallas_call(
        functools.partial(_ab_in_kernel, x_starts=x_starts, ub_starts=ub_starts),
        grid=(T // tm,),
        in_specs=x_specs + [pl.BlockSpec((1, D), fix),
                  pl.BlockSpec(win.shape, fix, pipeline_mode=RESIDENT), pl.BlockSpec((1, Q_LORA), fix),
                  pl.BlockSpec((1, KV_LORA), fix), pl.BlockSpec(wq.shape, fix, pipeline_mode=RESIDENT),
                  _pos_table_spec(tm, LANE, long_rows, seq_len), _pos_table_spec(tm, LANE, long_rows, seq_len),
                  pl.BlockSpec(wkv.shape, fix, pipeline_mode=RESIDENT)],
        out_specs=[pl.BlockSpec((tm, MLA_HEADS * QK_PAD), row), pl.BlockSpec((tm, KV_LORA), row),
                   pl.BlockSpec((tm, LANE), row), pl.BlockSpec((tm, s5w), row),
                   pl.BlockSpec((tm, MLA_HEADS * QK_PAD), row), pl.BlockSpec((tm, MLA_HEADS * V_PAD), row)] + ub_specs,
        out_shape=[jax.ShapeDtypeStruct((T, MLA_HEADS * QK_PAD), BF16),
                   jax.ShapeDtypeStruct((T, KV_LORA), F32),
                   jax.ShapeDtypeStruct((T, LANE), F32),
                   jax.ShapeDtypeStruct((T, s5w), F32),
                   jax.ShapeDtypeStruct((T, MLA_HEADS * QK_PAD), BF16),
                   jax.ShapeDtypeStruct((T, MLA_HEADS * V_PAD), BF16)]
        + [jax.ShapeDtypeStruct((r, s5w), BF16) for r in ub_rows],
        compiler_params=_cp(("arbitrary",)),
        name="ab_in",
    )(*xs, g, win, qn, kvn, wq, cs1, cs2, wkv)


def _kv_kernel(ckv_ref, kr_ref, w_ref, k_out, v_out):
    _store_keys_values(_dot(ckv_ref[...].astype(BF16), w_ref[...]), kr_ref[...].astype(BF16), k_out, v_out)


def _store_keys_values(kv, kr, k_out, v_out):
    v0 = MLA_HEADS * MLA_NOPE
    ones = jnp.ones((kv.shape[0], V_PAD - MLA_V), BF16)
    for h in range(MLA_HEADS):
        k_out[:, h * QK_PAD:h * QK_PAD + LANE] = kv[:, h * LANE:(h + 1) * LANE].astype(BF16)
        k_out[:, h * QK_PAD + LANE:(h + 1) * QK_PAD] = kr
        v_out[:, h * V_PAD:h * V_PAD + MLA_V] = kv[:, v0 + h * MLA_V:v0 + (h + 1) * MLA_V].astype(BF16)
        v_out[:, h * V_PAD + MLA_V:(h + 1) * V_PAD] = ones


def _kv_proj(ckv, kr, w, tm):
    R = ckv.shape[0]
    row = lambda i: (i, 0)
    return pl.pallas_call(
        _kv_kernel,
        grid=(R // tm,),
        in_specs=[pl.BlockSpec((tm, KV_LORA), row), pl.BlockSpec((tm, LANE), row),
                  pl.BlockSpec(w.shape, lambda i: (0, 0))],
        out_specs=[pl.BlockSpec((tm, MLA_HEADS * QK_PAD), row),
                   pl.BlockSpec((tm, MLA_HEADS * V_PAD), row)],
        out_shape=[jax.ShapeDtypeStruct((R, MLA_HEADS * QK_PAD), BF16),
                   jax.ShapeDtypeStruct((R, MLA_HEADS * V_PAD), BF16)],
        compiler_params=_cp(("parallel",)),
        name="kv_proj",
    )(ckv, kr, w)


def _attn_long_kernel(q_ref, k_ref, v_ref, o_ref, s_sc, m_sc, acc_sc, *, tq, tk):
    qi = pl.program_id(2)

    def scores_to(slot, kt, r0=0):
        s_sc[slot, r0:, :] = _dot_nt(q_ref[r0:, :], k_ref[pl.ds(pl.multiple_of(kt * tk, tk), tk), :])

    def consume(slot, kt, masked, r0=0):
        s = s_sc[slot, r0:, :]
        if masked:
            qpos = qi * tq + r0 + lax.broadcasted_iota(jnp.int32, (tq - r0, tk), 0)
            kpos = kt * tk + lax.broadcasted_iota(jnp.int32, (tq - r0, tk), 1)
            s = jnp.where(kpos // CHUNK <= qpos // CHUNK, s, NEG_BIG)
        m = m_sc[r0:, :]
        m_new = jnp.maximum(m, jnp.max(s, axis=1, keepdims=True))
        alpha = jnp.exp2(m - m_new)
        p = jnp.exp2(s - jnp.concatenate([m_new] * (tk // LANE), axis=1))
        m_sc[r0:, :] = m_new
        v = v_ref[pl.ds(pl.multiple_of(kt * tk, tk), tk), :]
        acc_sc[r0:, :] = (jnp.concatenate([alpha] * (V_PAD // LANE), axis=1) * acc_sc[r0:, :]
                          + _dot(p.astype(BF16), v))

    m_sc[...] = jnp.full_like(m_sc, NEG_BIG)
    acc_sc[...] = jnp.zeros_like(acc_sc)
    scores_to(0, 0)

    def pair(j, carry):
        scores_to(1, 2 * j + 1)
        consume(0, 2 * j, False)
        scores_to(0, 2 * j + 2)
        consume(1, 2 * j + 1, False)
        return carry

    lax.fori_loop(0, qi, pair, 0)
    scores_to(1, 2 * qi + 1, tk)
    consume(0, 2 * qi, True)
    consume(1, 2 * qi + 1, True, tk)
    acc = acc_sc[...]
    o_ref[...] = (acc[:, :MLA_V] / acc[:, MLA_V:]).astype(BF16)


def _attn_long(q, k, v, nb, s, tk):
    tq = 2 * tk
    assert s % tq == 0 and V_PAD == 2 * MLA_V
    nq = s // tq
    return pl.pallas_call(
        functools.partial(_attn_long_kernel, tq=tq, tk=tk),
        grid=(nb, MLA_HEADS, nq),
        in_specs=[pl.BlockSpec((tq, QK_PAD), lambda b, h, i: (b * nq + i, h)),
                  pl.BlockSpec((s, QK_PAD), lambda b, h, i: (b, h)),
                  pl.BlockSpec((s, V_PAD), lambda b, h, i: (b, h))],
        out_specs=pl.BlockSpec((tq, MLA_V), lambda b, h, i: (b * nq + i, h)),
        out_shape=jax.ShapeDtypeStruct((nb * s, MLA_HEADS * MLA_V), BF16),
        scratch_shapes=[pltpu.VMEM((2, tq, tk), F32), pltpu.VMEM((tq, LANE), F32),
                        pltpu.VMEM((tq, V_PAD), F32)],
        compiler_params=_cp(("parallel", "parallel", "arbitrary")),
        name="attn_long",
    )(q, k, v)


def _attn_short_kernel(q_ref, k_ref, v_ref, o_ref, *, q_pos0, n_keys):
    sq = q_ref.shape[0]
    skp = k_ref.shape[0]
    s = _dot_nt(q_ref[...], k_ref[...])
    qpos = q_pos0 + lax.broadcasted_iota(jnp.int32, (sq, skp), 0)
    kpos = lax.broadcasted_iota(jnp.int32, (sq, skp), 1)
    ok = (kpos // CHUNK <= qpos // CHUNK) & (kpos < n_keys)
    s = jnp.where(ok, s, NEG_BIG)
    m = jnp.max(s, axis=1, keepdims=True)
    p = jnp.exp2(s - m)
    acc = _dot(p.astype(BF16), v_ref[...])
    o_ref[...] = (acc[:, :MLA_V] / acc[:, MLA_V:]).astype(BF16)


def _attn_short(q, k, v, nb, sq, skp, q_row0, k_row0, q_pos0, n_keys):
    qb0 = q_row0 // sq
    kb0 = k_row0 // skp
    return pl.pallas_call(
        functools.partial(_attn_short_kernel, q_pos0=q_pos0, n_keys=n_keys),
        grid=(nb, MLA_HEADS),
        in_specs=[pl.BlockSpec((sq, QK_PAD), lambda b, h: (qb0 + b, h)),
                  pl.BlockSpec((skp, QK_PAD), lambda b, h: (kb0 + b, h)),
                  pl.BlockSpec((skp, V_PAD), lambda b, h: (kb0 + b, h))],
        out_specs=pl.BlockSpec((sq, MLA_V), lambda b, h: (b, h)),
        out_shape=jax.ShapeDtypeStruct((nb * sq, MLA_HEADS * MLA_V), BF16),
        compiler_params=_cp(("parallel", "parallel")),
        name="attn_short",
    )(q, k, v)


def _s5_param_kernel(lr_ref, li_ref, ldt_ref, btr_ref, bti_ref, cr_ref, ci_ref,
                     m_out, wr_out, wi_out, vr_out, vi_out, alr_out, ali_out,
                     car_sc, cai_sc):
    L, P = S5_L, S5_GROUP
    lr = jnp.minimum(lr_ref[...], -1e-4)
    li = li_ref[...]
    dt = jnp.exp(ldt_ref[...])
    mag = jnp.exp(lr * dt)
    a_re = mag * jnp.cos(li * dt)
    a_im = mag * jnp.sin(li * dt)
    den = lr * lr + li * li
    f_re = ((a_re - 1.0) * lr + a_im * li) / den
    f_im = (a_im * lr - (a_re - 1.0) * li) / den
    btr = btr_ref[...]
    bti = bti_ref[...]
    bb_re = f_re * btr - f_im * bti
    bb_im = f_re * bti + f_im * btr
    cr = cr_ref[...]
    ci = ci_ref[...]

    n_pow = -(-(L + 1) // 8) * 8
    t = lax.broadcasted_iota(jnp.int32, (n_pow, 1), 0).astype(F32)
    pmag = jnp.exp((lr * dt) * t)
    pang = (li * dt) * t
    pw_re = pmag * jnp.cos(pang)
    pw_im = pmag * jnp.sin(pang)

    def apow(t):
        return pw_re[t:t + 1, :], pw_im[t:t + 1, :]

    for i in range(L):
        rows = slice(i * P, (i + 1) * P)
        pr, pi = apow(L - 1 - i)
        wr_out[rows, :] = (pr * bb_re - pi * bb_im).astype(BF16)
        wi_out[rows, :] = (pr * bb_im + pi * bb_re).astype(BF16)
        pr, pi = apow(i + 1)
        vr_out[rows, :] = (cr * pr - ci * pi).astype(BF16)
        vi_out[rows, :] = (-(cr * pi + ci * pr)).astype(BF16)
        pr, pi = apow(i)
        car_sc[rows, :] = cr * pr - ci * pi
        cai_sc[rows, :] = cr * pi + ci * pr
    pr, pi = apow(L)
    alr_out[...] = pr
    ali_out[...] = pi
    hp = lax.Precision.HIGHEST
    nt = (((1,), (1,)), ((), ()))
    krow = (lax.dot_general(bb_re, car_sc[...], nt, precision=hp, preferred_element_type=F32)
            - lax.dot_general(bb_im, cai_sc[...], nt, precision=hp, preferred_element_type=F32))
    lane = lax.broadcasted_iota(jnp.int32, (P, L * P), 1)
    for i in range(L):
        shifted = krow if i == 0 else pltpu.roll(krow, i * P, axis=1)
        m_out[i * P:(i + 1) * P, :] = jnp.where(lane >= i * P, shifted, 0.0).astype(BF16)


def _s5_params(lam_re, lam_im, log_dt, b_re, b_im, c_re, c_im):
    G, N = lam_re.shape
    P, L = S5_GROUP, S5_L
    LP = L * P
    v3 = lambda a: a.reshape(G, 1, N)
    ldt = jnp.broadcast_to(log_dt[:, None, None], (G, 1, N))
    btr = jnp.swapaxes(b_re, 1, 2)
    bti = jnp.swapaxes(b_im, 1, 2)
    vec = pl.BlockSpec((None, 1, N), lambda g: (g, 0, 0))
    mat = pl.BlockSpec((None, P, N), lambda g: (g, 0, 0))
    big = pl.BlockSpec((None, LP, N), lambda g: (g, 0, 0))
    return pl.pallas_call(
        _s5_param_kernel,
        grid=(G,),
        in_specs=[vec, vec, vec, mat, mat, mat, mat],
        out_specs=[pl.BlockSpec((None, LP, LP), lambda g: (g, 0, 0)), big, big, big, big, vec, vec],
        out_shape=[jax.ShapeDtypeStruct((G, LP, LP), BF16)]
        + [jax.ShapeDtypeStruct((G, LP, N), BF16)] * 4
        + [jax.ShapeDtypeStruct((G, 1, N), F32)] * 2,
        scratch_shapes=[pltpu.VMEM((LP, N), F32), pltpu.VMEM((LP, N), F32)],
        compiler_params=_cp(("parallel",)),
        name="s5_params",
    )(v3(lam_re), v3(lam_im), ldt, btr, bti, c_re, c_im)


def _s5_main_kernel(*refs, nseq, nchunk, has_state):
    if has_state:
        (u_ref, m_ref, wr_ref, wi_ref, vr_ref, vi_ref, alr_ref, ali_ref, h0r_ref, h0i_ref,
         y_out, hfr_out, hfi_out, xr_sc, xi_sc, hr_sc, hi_sc) = refs
    else:
        (u_ref, m_ref, wr_ref, wi_ref, vr_ref, vi_ref, alr_ref, ali_ref,
         y_out, hfr_out, hfi_out, xr_sc, xi_sc, hr_sc, hi_sc) = refs
    ub = u_ref[...]
    xr_sc[...] = _dot(ub, wr_ref[...])
    xi_sc[...] = _dot(ub, wi_ref[...])
    ar = alr_ref[...]
    ai = ali_ref[...]
    n = ar.shape[-1]
    if has_state:
        init = (h0r_ref[...], h0i_ref[...])
    else:
        init = (jnp.zeros((nseq, n), F32), jnp.zeros((nseq, n), F32))

    def body(c, carry):
        sr, si = carry
        rows = pl.ds(pl.multiple_of(c * nseq, nseq), nseq)
        hr_sc[rows, :] = sr
        hi_sc[rows, :] = si
        return (ar * sr - ai * si + xr_sc[rows, :], ar * si + ai * sr + xi_sc[rows, :])

    sr, si = lax.fori_loop(0, nchunk, body, init)
    hfr_out[...] = sr
    hfi_out[...] = si
    y_out[...] = (_dot(ub, m_ref[...])
                  + _dot_nt(hr_sc[...].astype(BF16), vr_ref[...])
                  + _dot_nt(hi_sc[...].astype(BF16), vi_ref[...])).astype(y_out.dtype)


def _s5_main(u, params, h0, nseq, nchunk):
    m, wr, wi, vr, vi, alr, ali = params
    G, R, LP = u.shape
    N = alr.shape[-1]
    has_state = h0 is not None
    g3 = lambda g: (g, 0, 0)
    big = pl.BlockSpec((None, LP, N), g3)
    vec = pl.BlockSpec((None, 1, N), g3)
    st = pl.BlockSpec((None, nseq, N), g3)
    in_specs = [pl.BlockSpec((None, R, LP), g3), pl.BlockSpec((None, LP, LP), g3),
                big, big, big, big, vec, vec]
    args = [u, m, wr, wi, vr, vi, alr, ali]
    if has_state:
        in_specs += [st, st]
        args += list(h0)
    return pl.pallas_call(
        functools.partial(_s5_main_kernel, nseq=nseq, nchunk=nchunk, has_state=has_state),
        grid=(G,),
        in_specs=in_specs,
        out_specs=[pl.BlockSpec((None, R, LP), g3), st, st],
        out_shape=[jax.ShapeDtypeStruct((G, R, LP), BF16),
                   jax.ShapeDtypeStruct((G, nseq, N), F32),
                   jax.ShapeDtypeStruct((G, nseq, N), F32)],
        scratch_shapes=[pltpu.VMEM((R, N), F32)] * 4,
        compiler_params=_cp(("parallel",)),
        name="s5_main",
    )(*args)


def _ab_out_kernel(*refs, starts):
    i = pl.program_id(0)
    parts = []
    for st in starts:
        parts.append(_parts_load(refs[:len(st)], st, i))
        refs = refs[len(st):]
    x, attn, y = parts
    u_ref, d_ref, wglu_ref, bglu_ref, wo_ref, o_ref = refs
    y = y.astype(F32) + d_ref[...] * u_ref[...]
    z = y * (0.5 * (1.0 + jnp.tanh(math.sqrt(2.0 / math.pi) * (y + 0.044715 * (y * y * y)))))
    gate = 1.0 / (1.0 + jnp.exp(-(_dot(z.astype(BF16), wglu_ref[...]) + bglu_ref[...])))
    ssm = (z * gate).astype(BF16)
    na = attn.shape[1]
    o_ref[...] = x + _dot(attn, wo_ref[:na, :]) + _dot(ssm, wo_ref[na:, :])


def _ab_out(xs, attns, ys, u, d, wglu, bglu, wo, li, tm):
    D = xs[0].shape[1]
    T = sum(x.shape[0] for x in xs)
    row = lambda i: (i, 0)
    fix = lambda i: (0, 0)
    lay = lambda i: (li, 0, 0)
    specs, starts = [], []
    for parts in (xs, attns, ys):
        sp, st = _parts_specs([p.shape[0] for p in parts], tm, parts[0].shape[1])
        specs += sp
        starts.append(st)
    return pl.pallas_call(
        functools.partial(_ab_out_kernel, starts=tuple(starts)),
        grid=(T // tm,),
        in_specs=specs + [pl.BlockSpec((tm, u.shape[1]), row), pl.BlockSpec(d.shape, fix),
                          pl.BlockSpec((None,) + wglu.shape[1:], lay, pipeline_mode=RESIDENT),
                          pl.BlockSpec(bglu.shape, fix),
                          pl.BlockSpec((None,) + wo.shape[1:], lay, pipeline_mode=RESIDENT)],
        out_specs=pl.BlockSpec((tm, D), row),
        out_shape=jax.ShapeDtypeStruct((T, D), F32),
        compiler_params=_cp(("parallel",)),
        name="ab_out",
    )(*xs, *attns, *ys, u, d, wglu, bglu, wo)


def _mlp_kernel(x_ref, g_ref, wup_ref, wdn_ref, gf_ref, *rest, final_norm, o_starts):
    n_out = len(o_starts)
    o_refs = rest[:n_out]
    xn_sc = rest[n_out]
    acc = rest[n_out + 1] if n_out > 1 else o_refs[0]
    f = pl.program_id(1)

    @pl.when(f == 0)
    def _():
        xn_sc[...] = _rms(x_ref[...], g_ref[...]).astype(BF16)
        acc[...] = jnp.zeros_like(acc)

    a = jnp.maximum(_dot(xn_sc[...], wup_ref[...]), 0.0)
    acc[...] += _dot((a * a).astype(BF16), wdn_ref[...])

    @pl.when(f == pl.num_programs(1) - 1)
    def _():
        r = x_ref[...] + acc[...]
        if final_norm:
            r = _rms(r, gf_ref[...])
        if n_out > 1:
            _parts_store(o_refs, o_starts, pl.program_id(0), r)
        else:
            acc[...] = r


def _mlp(x, g, wup, wdn, li, gf, final_norm, out_rows, tm, tf):
    T, D = x.shape
    FF = wup.shape[2]
    assert sum(out_rows) == T
    o_specs, o_starts = _parts_specs(out_rows, tm, D)
    return pl.pallas_call(
        functools.partial(_mlp_kernel, final_norm=final_norm, o_starts=o_starts),
        grid=(pl.cdiv(T, tm), FF // tf),
        in_specs=[pl.BlockSpec((tm, D), lambda i, f: (i, 0)), pl.BlockSpec((1, D), lambda i, f: (0, 0)),
                  pl.BlockSpec((None, D, tf), lambda i, f: (li, 0, f)),
                  pl.BlockSpec((None, tf, D), lambda i, f: (li, f, 0)),
                  pl.BlockSpec((1, D), lambda i, f: (0, 0))],
        out_specs=o_specs,
        out_shape=[jax.ShapeDtypeStruct((r, D), F32) for r in out_rows],
        scratch_shapes=[pltpu.VMEM((tm, D), BF16)] + [pltpu.VMEM((tm, D), F32)] * (len(out_rows) > 1),
        compiler_params=_cp(("arbitrary" if len(out_rows) > 1 else "parallel", "arbitrary")),
        name="mlp",
    )(x, g, wup, wdn, gf)


def _ret_log_decay(head):
    hf = jnp.full((1, 1), head, jnp.int32).astype(F32)
    return jnp.log(1.0 - jnp.exp2(-5.0 - hf))


def _c_in_kernel(x_ref, g_ref, w_ref, cos_ref, sin_ref, o_ref, xn_sc, *, n_rot, n_q, n_lin, dk,
                 chunk_split_row, chunk_lens):
    i = pl.program_id(0)
    j = pl.program_id(1)

    @pl.when(j == 0)
    def _():
        xn_sc[...] = _rms(x_ref[...], g_ref[...]).astype(BF16)

    tm = x_ref.shape[0]
    tn = w_ref.shape[1]
    half = dk // 2

    def chunks():
        for s in range(0, tn, dk):
            yield s, _dot(xn_sc[...], w_ref[:, s:s + dk])

    @pl.when(j < n_rot)
    def _():
        cos = cos_ref[...]
        sin = sin_ref[...]
        c_len = jnp.where(i * tm >= chunk_split_row, chunk_lens[1], chunk_lens[0])
        pos = (lax.broadcasted_iota(jnp.int32, (tm, 1), 0) & (c_len - 1)).astype(F32) + 1.0
        sign = jnp.where(j < n_q, 1.0, -1.0)
        scale = jnp.where(j < n_q, 1.0, dk ** -0.5)
        for s, acc in chunks():
            lg = _ret_log_decay(((j * tn + s) // dk) % RET_HEADS)
            dec = jnp.exp((sign * lg) * pos) * scale
            x1 = acc[:, :half]
            x2 = acc[:, half:]
            o_ref[:, s:s + half] = ((x1 * cos - x2 * sin) * dec).astype(BF16)
            o_ref[:, s + half:s + dk] = ((x1 * sin + x2 * cos) * dec).astype(BF16)

    @pl.when((j >= n_rot) & (j < n_lin))
    def _():
        for s, acc in chunks():
            o_ref[:, s:s + dk] = acc.astype(BF16)

    @pl.when(j >= n_lin)
    def _():
        for s, acc in chunks():
            o_ref[:, s:s + dk] = (acc / (1.0 + jnp.exp(-acc))).astype(BF16)


def _c_in(x, g, w, li, cos, sin, dk, dv, tm, tn, chunk_split_row, chunk_lens, seq_len):
    T, D = x.shape
    N = w.shape[2]
    qk = RET_HEADS * dk
    n_q = qk // tn
    n_rot = 2 * qk // tn
    n_lin = (2 * qk + RET_HEADS * dv) // tn
    assert chunk_split_row % tm == 0
    assert all(tm % c == 0 and c & (c - 1) == 0 for c in chunk_lens)
    return pl.pallas_call(
        functools.partial(_c_in_kernel, n_rot=n_rot, n_q=n_q, n_lin=n_lin, dk=dk,
                          chunk_split_row=chunk_split_row, chunk_lens=chunk_lens),
        grid=(pl.cdiv(T, tm), N // tn),
        in_specs=[pl.BlockSpec((tm, D), lambda i, j: (i, 0)), pl.BlockSpec((1, D), lambda i, j: (0, 0)),
                  pl.BlockSpec((None, D, tn), lambda i, j: (li, 0, j)),
                  _pos_table_spec(tm, dk // 2, chunk_split_row, seq_len),
                  _pos_table_spec(tm, dk // 2, chunk_split_row, seq_len)],
        out_specs=pl.BlockSpec((tm, tn), lambda i, j: (i, j)),
        out_shape=jax.ShapeDtypeStruct((T, N), BF16),
        scratch_shapes=[pltpu.VMEM((tm, D), BF16)],
        compiler_params=_cp(("parallel", "arbitrary")),
        name="c_in",
    )(x, g, w, cos, sin)


def _ret_kernel(*refs, c_len, has_state, hpb, dk, dv):
    if has_state:
        q_ref, k_ref, v_ref, g_ref, gn_ref, s0_ref, o_ref, sf_ref, st_sc = refs
    else:
        q_ref, k_ref, v_ref, g_ref, gn_ref, o_ref, sf_ref, st_sc = refs
    hb = pl.program_id(1)
    c = pl.program_id(2)

    @pl.when(c == 0)
    def _():
        if has_state:
            st_sc[...] = s0_ref[...]
        else:
            st_sc[...] = jnp.zeros_like(st_sc)

    row = lax.broadcasted_iota(jnp.int32, (c_len, c_len), 0)
    col = lax.broadcasted_iota(jnp.int32, (c_len, c_len), 1)
    causal = row >= col
    for hh in range(hpb):
        q = q_ref[:, hh * dk:(hh + 1) * dk]
        k = k_ref[:, hh * dk:(hh + 1) * dk]
        v = v_ref[:, hh * dv:(hh + 1) * dv]
        inner = _dot(jnp.where(causal, _dot_nt(q, k), 0.0).astype(BF16), v)
        st = st_sc[hh]
        o = inner + _dot(q, st.astype(BF16))
        g_c = jnp.exp(_ret_log_decay(hb * hpb + hh) * float(c_len))
        st_sc[hh] = g_c * (st + _dot_tn(k, v))
        mu = jnp.mean(o, axis=-1, keepdims=True)
        d = o - mu
        var = jnp.mean(d * d, axis=-1, keepdims=True)
        on = d * lax.rsqrt(var + GN_EPS) * gn_ref[:, hh * dv:(hh + 1) * dv]
        o_ref[:, hh * dv:(hh + 1) * dv] = (on * g_ref[:, hh * dv:(hh + 1) * dv].astype(F32)).astype(BF16)

    @pl.when(c == pl.num_programs(2) - 1)
    def _():
        sf_ref[...] = st_sc[...]


def _retention(proj, gn, state0, o_prev, nb, s, c_len, row0, dk, dv, hpb):
    T = proj.shape[0]
    H = RET_HEADS
    nc = s // c_len
    rb0 = row0 // c_len
    nhb = H // hpb
    has_state = state0 is not None
    rowblk = lambda b, h, c: rb0 + b * nc + c
    wk, wv = hpb * dk, hpb * dv
    kb = H * dk // wk
    vb = 2 * H * dk // wv
    in_specs = [pl.BlockSpec((c_len, wk), lambda b, h, c: (rowblk(b, h, c), h)),
                pl.BlockSpec((c_len, wk), lambda b, h, c: (rowblk(b, h, c), kb + h)),
                pl.BlockSpec((c_len, wv), lambda b, h, c: (rowblk(b, h, c), vb + h)),
                pl.BlockSpec((c_len, wv), lambda b, h, c: (rowblk(b, h, c), vb + nhb + h)),
                pl.BlockSpec((1, wv), lambda b, h, c: (0, h))]
    args = [proj, proj, proj, proj, gn]
    if has_state:
        in_specs.append(pl.BlockSpec((None, hpb, dk, dv), lambda b, h, c: (b, h, 0, 0)))
        args.append(state0)
    aliases = {}
    if o_prev is not None:
        in_specs.append(pl.BlockSpec(memory_space=pl.ANY))
        args.append(o_prev)
        aliases = {len(args) - 1: 0}

    def kern(*refs):
        if o_prev is not None:
            n_in = len(args)
            refs = refs[:n_in - 1] + refs[n_in:]
        _ret_kernel(*refs, c_len=c_len, has_state=has_state, hpb=hpb, dk=dk, dv=dv)

    return pl.pallas_call(
        kern,
        grid=(nb, nhb, nc),
        in_specs=in_specs,
        out_specs=[pl.BlockSpec((c_len, wv), lambda b, h, c: (rowblk(b, h, c), h)),
                   pl.BlockSpec((None, hpb, dk, dv), lambda b, h, c: (b, h, 0, 0))],
        out_shape=[jax.ShapeDtypeStruct((T, H * dv), BF16),
                   jax.ShapeDtypeStruct((nb, H, dk, dv), F32)],
        scratch_shapes=[pltpu.VMEM((hpb, dk, dv), F32)],
        input_output_aliases=aliases,
        compiler_params=_cp(("parallel", "parallel", "arbitrary")),
        name="retention",
    )(*args)


def _c_out_kernel(x_ref, o_ref, w_ref, y_ref):
    y_ref[...] = x_ref[...] + _dot(o_ref[...], w_ref[...])


def _c_out(x, o, w, li, tm, tn):
    T, D = x.shape
    K = o.shape[1]
    return pl.pallas_call(
        _c_out_kernel,
        grid=(pl.cdiv(T, tm), D // tn),
        in_specs=[pl.BlockSpec((tm, tn), lambda i, j: (i, j)), pl.BlockSpec((tm, K), lambda i, j: (i, 0)),
                  pl.BlockSpec((None, K, tn), lambda i, j: (li, 0, j))],
        out_specs=pl.BlockSpec((tm, tn), lambda i, j: (i, j)),
        out_shape=jax.ShapeDtypeStruct((T, D), F32),
        compiler_params=_cp(("parallel", "arbitrary")),
        name="c_out",
    )(x, o, w)


def _rope_tables(pos, half):
    inv = ROPE_THETA ** (-jnp.arange(half, dtype=F32) / half)
    ang = pos.astype(F32)[:, None] * inv[None, :]
    return jnp.cos(ang), jnp.sin(ang)


def _swap_halves(w):
    half = w.shape[-1] // 2
    return jnp.concatenate([w[..., half:], w[..., :half]], axis=-1)


def _prep_w_in_ab(w):
    o1 = Q_LORA + KV_LORA
    o2 = o1 + MLA_ROPE
    kr = w[:, o1:o2]
    ks = _swap_halves(kr)
    return jnp.concatenate([w[:, :o1], kr, ks, ks, kr, w[:, o2:]], axis=1).astype(BF16)


def _prep_w_q_b(w):
    w = w.reshape(Q_LORA, MLA_HEADS, MLA_NOPE + MLA_ROPE)
    rope = w[..., MLA_NOPE:]
    swap = _swap_halves(rope)
    out = jnp.concatenate([w[..., :MLA_NOPE], rope, swap, swap, rope], axis=-1)
    return out.reshape(Q_LORA, -1).astype(BF16)


def _prep_w_kv_b(w):
    w = w.reshape(KV_LORA, MLA_HEADS, MLA_NOPE + MLA_V)
    return jnp.concatenate([w[..., :MLA_NOPE].reshape(KV_LORA, -1),
                            w[..., MLA_NOPE:].reshape(KV_LORA, -1)], axis=1).astype(BF16)


def kernel(x_prompt, x_sample, cache_mla_ckv, cache_mla_krope, state_s5_re, state_s5_im, state_ret,
           norm_mix, norm_mlp, norm_final, w_in_ab, q_a_norm, kv_a_norm, w_q_b, w_kv_b,
           s5_lam_re, s5_lam_im, s5_log_dt, s5_b_re, s5_b_im, s5_c_re, s5_c_im, s5_d, w_glu, b_glu,
           w_out_ab, w_in_c, ret_gn, w_out_c, w_up, w_down):
    BP, SP, D = x_prompt.shape
    BS, SS, _ = x_sample.shape
    past = cache_mla_ckv.shape[2]
    depth = norm_mix.shape[0]
    G, N = s5_lam_re.shape[1:]
    P, L = S5_GROUP, S5_L
    dk, dv = state_ret.shape[3:]
    TP, TS = BP * SP, BS * SS
    T = TP + TS

    x = (x_prompt.reshape(TP, D), x_sample.reshape(TS, D))
    pos = jnp.concatenate([jnp.arange(SP), jnp.tile(past + jnp.arange(SS), WIDE_TM // SS)])

    c32, s32 = _rope_tables(pos, MLA_ROPE // 2)
    z64 = jnp.zeros((pos.shape[0], LANE - MLA_ROPE), F32)
    cs1 = jnp.concatenate([c32, c32, z64], axis=1)
    cs2 = jnp.concatenate([-s32, s32, z64], axis=1)
    cos_r, sin_r = _rope_tables(pos, dk // 2)

    skp = -(-(past + SS) // LANE) * LANE
    w_glu_b, w_out_ab_b, w_in_c_b, w_out_c_b, w_up_b, w_down_b = (
        w.astype(BF16) for w in (w_glu, w_out_ab, w_in_c, w_out_c, w_up, w_down))
    ckv_p, kr_p, s5r_p, s5i_p, ret_p = [], [], [], [], []
    ckv_s, kr_s, s5r_s, s5i_s, ret_s = [], [], [], [], []

    for layer in range(depth):
        i = layer // 2
        g_mix = norm_mix[layer].reshape(1, D)
        if layer % 2 == 0:
            xs = x if isinstance(x, tuple) else (x,)
            w_kv = _prep_w_kv_b(w_kv_b[i])
            q, ckv, kr, u, k_p, v_p, ub_p, ub_s = _ab_in(
                xs, g_mix, _prep_w_in_ab(w_in_ab[i]), q_a_norm[i].reshape(1, -1),
                kv_a_norm[i].reshape(1, -1), _prep_w_q_b(w_q_b[i]), w_kv, cs1, cs2, AB_TM, TP, SP)
            ckv_p.append(ckv[:TP].reshape(BP, SP, KV_LORA))
            kr_p.append(kr[:TP, :MLA_ROPE].reshape(BP, SP, MLA_ROPE))
            ckv_s.append(ckv[TP:].reshape(BS, SS, KV_LORA))
            kr_s.append(kr[TP:, :MLA_ROPE].reshape(BS, SS, MLA_ROPE))
            pad = skp - past - SS
            ckv_hist = jnp.concatenate([cache_mla_ckv[i], ckv[TP:].reshape(BS, SS, KV_LORA),
                                        jnp.zeros((BS, pad, KV_LORA), F32)], axis=1)
            kr_past = jnp.pad(cache_mla_krope[i], ((0, 0), (0, 0), (0, LANE - MLA_ROPE)))
            kr_hist = jnp.concatenate([kr_past, kr[TP:].reshape(BS, SS, LANE),
                                       jnp.zeros((BS, pad, LANE), F32)], axis=1)
            k_s, v_s = _kv_proj(ckv_hist.reshape(BS * skp, KV_LORA), kr_hist.reshape(BS * skp, LANE),
                                w_kv, skp)
            attn = (_attn_long(q, k_p, v_p, BP, SP, ATT_TK),
                    _attn_short(q, k_s, v_s, BS, SS, skp, TP, 0, past, past + SS))
            params = _s5_params(s5_lam_re[i], s5_lam_im[i], s5_log_dt[i], s5_b_re[i], s5_b_im[i],
                                s5_c_re[i], s5_c_im[i])
            ncp, ncs = SP // L, SS // L
            u_p = ub_p.reshape(BP, ncp, L, G, P).transpose(3, 1, 0, 2, 4).reshape(G, ncp * BP, L * P)
            u_s = ub_s.reshape(BS, ncs, L, G, P).transpose(3, 1, 0, 2, 4).reshape(G, ncs * BS, L * P)
            y_p, hr_p, hi_p = _s5_main(u_p, params, None, BP, ncp)
            h0 = (jnp.swapaxes(state_s5_re[i], 0, 1), jnp.swapaxes(state_s5_im[i], 0, 1))
            y_s, hr_s, hi_s = _s5_main(u_s, params, h0, BS, ncs)
            y_p = y_p.reshape(G, ncp, BP, L, P).transpose(2, 1, 3, 0, 4).reshape(TP, G * P)
            y_s = y_s.reshape(G, ncs, BS, L, P).transpose(2, 1, 3, 0, 4).reshape(TS, G * P)
            s5r_p.append(jnp.swapaxes(hr_p, 0, 1))
            s5i_p.append(jnp.swapaxes(hi_p, 0, 1))
            s5r_s.append(jnp.swapaxes(hr_s, 0, 1))
            s5i_s.append(jnp.swapaxes(hi_s, 0, 1))
            x = _ab_out(xs, attn, (y_p, y_s), u, s5_d[i].reshape(1, G * P), w_glu_b,
                        b_glu[i].reshape(1, -1), w_out_ab_b, i, AB_TM)
        else:
            proj = _c_in(x, g_mix, w_in_c_b, i, cos_r, sin_r, dk, dv, WIDE_TM, C_IN_TN, TP, (RET_C, SS), SP)
            gn = ret_gn[i].reshape(1, -1)
            o, st_p = _retention(proj, gn, None, None, BP, SP, RET_C, 0, dk, dv, RET_HPB)
            o, st_s = _retention(proj, gn, state_ret[i], o, BS, SS, SS, TP, dk, dv, RET_HPB)
            ret_p.append(st_p)
            ret_s.append(st_s)
            x = _c_out(x, o, w_out_c_b, i, WIDE_TM, C_OUT_TN)
        last = layer == depth - 1
        mlp_tm = TM if last or T % MLP_TM else MLP_TM
        outs = _mlp(x, norm_mlp[layer].reshape(1, D), w_up_b, w_down_b, layer,
                    norm_final.reshape(1, D), last, (TP, TS) if last else (T,), mlp_tm, MLP_TF)
        x = outs[0]

    return (outs[0].reshape(BP, SP, D), outs[1].reshape(BS, SS, D),
            jnp.stack(ckv_p), jnp.stack(kr_p), jnp.stack(s5r_p), jnp.stack(s5i_p), jnp.stack(ret_p),
            jnp.stack(ckv_s), jnp.stack(kr_s), jnp.stack(s5r_s), jnp.stack(s5i_s), jnp.stack(ret_s))
```
